```python
import jax, jax.numpy as jnp
from jax import lax
import numpy as np

D_MODEL = 1024
BATCH = 4
SEQ = 4096
DEPTH = 1

POOL_WIDTH = D_MODEL // 2
POOL_WINDOWS = (2, 4, 8, 16)
POOL_GROUP = POOL_WIDTH // len(POOL_WINDOWS)
LRU_WIDTH = D_MODEL - POOL_WIDTH
LRU_HEADS = 8
LRU_HEAD_DIM = LRU_WIDTH // LRU_HEADS
CONV_WIDTH = 4
LRU_C = 8.0
MIX_WIDTH = POOL_WIDTH + LRU_WIDTH
IN_WIDTH = POOL_WIDTH + 2 * LRU_WIDTH
D_FF = ((8 * D_MODEL // 3 + 255) // 256) * 256
EPS = 1e-6

kernel_name = "hymba_pool_rglru_block"


def rmsnorm(x, g):
    xf = x.astype(jnp.float32)
    y = xf * lax.rsqrt(jnp.mean(xf * xf, axis=-1, keepdims=True) + EPS)
    return (y * g.astype(jnp.float32)).astype(x.dtype)


def pool_mixer(u, pool_w, pool_scale):
    B, S, _ = u.shape
    uf = u.astype(jnp.float32)
    pos = jnp.arange(1, S + 1, dtype=jnp.float32)[None, :, None]
    outs = []
    for g, w in enumerate(POOL_WINDOWS):
        ug = uf[..., g * POOL_GROUP:(g + 1) * POOL_GROUP]
        cs = jnp.cumsum(ug, axis=1)
        lag = jnp.pad(cs[:, :S - w], ((0, 0), (w, 0), (0, 0)))
        mean = (cs - lag) / jnp.minimum(pos, float(w))
        outs.append(mean - ug)
    d = jnp.stack(outs, axis=2)
    y = jnp.einsum('bsgc,gcd->bsgd', d, pool_w.astype(jnp.float32)).reshape(B, S, POOL_WIDTH)
    return (y * pool_scale.astype(jnp.float32)).astype(u.dtype)


def causal_depthwise_conv(u, w, b):
    C = u.shape[-1]
    y = lax.conv_general_dilated(
        u, w[:, None, :].astype(u.dtype), window_strides=(1,),
        padding=[(CONV_WIDTH - 1, 0)], dimension_numbers=('NWC', 'WIO', 'NWC'),
        feature_group_count=C)
    return y + b.astype(u.dtype)


def _lin_combine(c1, c2):
    a1, b1 = c1
    a2, b2 = c2
    return a1 * a2, a2 * b1 + b2


def rg_lru(xc, w_a, b_a, w_i, b_i, lam):
    B, S, C = xc.shape
    xf = xc.astype(jnp.float32)
    xh = xf.reshape(B, S, LRU_HEADS, LRU_HEAD_DIM)
    r = jax.nn.sigmoid(jnp.einsum('bshc,hcd->bshd', xh, w_a.astype(jnp.float32)).reshape(B, S, C) + b_a.astype(jnp.float32))
    i = jax.nn.sigmoid(jnp.einsum('bshc,hcd->bshd', xh, w_i.astype(jnp.float32)).reshape(B, S, C) + b_i.astype(jnp.float32))
    log_a = -LRU_C * r * jax.nn.softplus(-lam.astype(jnp.float32))
    a = jnp.exp(log_a)
    mult = jnp.sqrt(jnp.maximum(-jnp.expm1(2.0 * log_a), 0.0))
    bterm = mult * (i * xf)
    _, h = lax.associative_scan(_lin_combine, (a, bterm), axis=1)
    return h.astype(xc.dtype)


def setup_inputs(seed: int = 0) -> dict:
    key = jax.random.key(seed)
    ks = jax.random.split(key, 24)
    f32 = jnp.float32

    def nrm(k, shape, fan_in):
        return jax.random.normal(k, shape, f32) * (fan_in ** -0.5)

    def gain(k, shape):
        return 1.0 + 0.05 * jax.random.normal(k, shape, f32)

    x = jax.random.normal(ks[0], (BATCH, SEQ, D_MODEL), f32)
    ln1_g = gain(ks[1], (DEPTH, D_MODEL))
    w_in = nrm(ks[2], (DEPTH, D_MODEL, IN_WIDTH), D_MODEL)
    pool_w = nrm(ks[3], (DEPTH, len(POOL_WINDOWS), POOL_GROUP, POOL_GROUP), POOL_GROUP)
    pool_scale = 1.0 + 0.1 * jax.random.normal(ks[4], (DEPTH, POOL_WIDTH), f32)
    conv_w = nrm(ks[5], (DEPTH, CONV_WIDTH, LRU_WIDTH), CONV_WIDTH)
    conv_b = 0.02 * jax.random.normal(ks[6], (DEPTH, LRU_WIDTH), f32)
    w_a = nrm(ks[7], (DEPTH, LRU_HEADS, LRU_HEAD_DIM, LRU_HEAD_DIM), LRU_HEAD_DIM)
    b_a = 0.02 * jax.random.normal(ks[8], (DEPTH, LRU_WIDTH), f32)
    w_i = nrm(ks[9], (DEPTH, LRU_HEADS, LRU_HEAD_DIM, LRU_HEAD_DIM), LRU_HEAD_DIM)
    b_i = 0.02 * jax.random.normal(ks[10], (DEPTH, LRU_WIDTH), f32)
    u = jax.random.uniform(ks[11], (DEPTH, LRU_WIDTH), f32, minval=0.9, maxval=0.999)
    s = u ** (1.0 / LRU_C)
    lam = jnp.log(s) - jnp.log1p(-s)
    gn_pool_g = gain(ks[12], (DEPTH, POOL_WIDTH))
    gn_lru_g = gain(ks[13], (DEPTH, LRU_WIDTH))
    w_out = nrm(ks[14], (DEPTH, MIX_WIDTH, D_MODEL), MIX_WIDTH)
    ln2_g = gain(ks[15], (DEPTH, D_MODEL))
    w_ffn_gate = nrm(ks[16], (DEPTH, D_MODEL, D_FF), D_MODEL)
    w_ffn_up = nrm(ks[17], (DEPTH, D_MODEL, D_FF), D_MODEL)
    w_ffn_down = nrm(ks[18], (DEPTH, D_FF, D_MODEL), D_FF)
    lnf_g = gain(ks[19], (D_MODEL,))
    return {"x": x, "ln1_g": ln1_g, "w_in": w_in, "pool_w": pool_w, "pool_scale": pool_scale,
            "conv_w": conv_w, "conv_b": conv_b, "w_a": w_a, "b_a": b_a, "w_i": w_i, "b_i": b_i,
            "lam": lam, "gn_pool_g": gn_pool_g, "gn_lru_g": gn_lru_g, "w_out": w_out,
            "ln2_g": ln2_g, "w_ffn_gate": w_ffn_gate, "w_ffn_up": w_ffn_up,
            "w_ffn_down": w_ffn_down, "lnf_g": lnf_g}


def reference(x, ln1_g, w_in, pool_w, pool_scale, conv_w, conv_b, w_a, b_a, w_i, b_i,
              lam, gn_pool_g, gn_lru_g, w_out, ln2_g, w_ffn_gate, w_ffn_up, w_ffn_down, lnf_g):
    h = x
    for l in range(DEPTH):
        n = rmsnorm(h, ln1_g[l])
        proj = n @ w_in[l]
        u_pool = proj[..., :POOL_WIDTH]
        u_lru = proj[..., POOL_WIDTH:POOL_WIDTH + LRU_WIDTH]
        u_gate = proj[..., POOL_WIDTH + LRU_WIDTH:]
        y_pool = pool_mixer(u_pool, pool_w[l], pool_scale[l])
        xc = causal_depthwise_conv(u_lru, conv_w[l], conv_b[l])
        y_lru = rg_lru(xc, w_a[l], b_a[l], w_i[l], b_i[l], lam[l]) * jax.nn.gelu(u_gate)
        mix = jnp.concatenate([rmsnorm(y_pool, gn_pool_g[l]), rmsnorm(y_lru, gn_lru_g[l])], axis=-1)
        h = h + mix @ w_out[l]
        n2 = rmsnorm(h, ln2_g[l])
        h = h + (jax.nn.silu(n2 @ w_ffn_gate[l]) * (n2 @ w_ffn_up[l])) @ w_ffn_down[l]
    return rmsnorm(h, lnf_g)
```

```python
import functools
import math

import jax
import jax.numpy as jnp
from jax import lax
from jax.experimental import pallas as pl
from jax.experimental.pallas import tpu as pltpu

D_MODEL = 1024
POOL_WINDOWS = (2, 4, 8, 16)
POOL_GROUP = 128
POOL_WIDTH = 512
LRU_WIDTH = 512
LRU_HEADS = 8
LRU_HEAD_DIM = 64
CONV_WIDTH = 4
LRU_C = 8.0
D_FF = 2816
EPS = 1e-6

LANES = 128
SUBLANES = 8
SEQ_TILE = 512
POOL_TAIL = 16
CONV_TAIL = 8
SUB_LEN = SEQ_TILE // SUBLANES
SUB_PITCH = SUB_LEN + SUBLANES
FF_CHUNKS = ((0, 768), (768, 1536), (1536, 2304), (2304, 2816))
VMEM_LIMIT_BYTES = 58 * 1024 * 1024


def _rms(x, g):
    ms = jnp.mean(x * x, axis=-1, keepdims=True)
    return x * lax.rsqrt(ms + EPS) * g


def _sigmoid(x):
    return 0.5 * (jnp.tanh(0.5 * x) + 1.0)


def _gelu_tanh(x):
    c = math.sqrt(2.0 / math.pi)
    return 0.5 * x * (1.0 + jnp.tanh(c * (x + 0.044715 * (x * x * x))))


def _dot(a, b):
    return jnp.dot(a, b, preferred_element_type=jnp.float32)


def _layer_kernel(x_ref, ln1_ref, w_in_ref, pool_w_ref, pool_scale_ref, conv_w_ref, conv_b_ref,
                  gate_w_ref, b_a_ref, b_i_ref, lam_ref, gn_pool_ref, gn_lru_ref, w_out_ref,
                  ln2_ref, wg_ref, wu_ref, wd_ref, lnf_ref, o_ref,
                  pbuf, lbuf, abuf, bbuf, hbuf, hcarry):
    s = pl.program_id(1)
    T = SEQ_TILE

    @pl.when(s == 0)
    def _():
        pbuf[0:POOL_TAIL, :] = jnp.zeros((POOL_TAIL, POOL_WIDTH), jnp.float32)
        lbuf[0:CONV_TAIL, :] = jnp.zeros((CONV_TAIL, LRU_WIDTH), jnp.float32)
        hcarry[...] = jnp.zeros_like(hcarry)

    x = x_ref[...]
    nb = _rms(x, ln1_ref[...]).astype(jnp.bfloat16)

    pbuf[POOL_TAIL:POOL_TAIL + T, :] = _dot(nb, w_in_ref[:, 0:POOL_WIDTH])
    lbuf[CONV_TAIL:CONV_TAIL + T, :] = _dot(nb, w_in_ref[:, POOL_WIDTH:POOL_WIDTH + LRU_WIDTH])
    u_gate = _dot(nb, w_in_ref[:, POOL_WIDTH + LRU_WIDTH:])

    pos = (s * T + 1 + lax.broadcasted_iota(jnp.int32, (T, POOL_GROUP), 0)).astype(jnp.float32)
    d_parts = []
    for g, w in enumerate(POOL_WINDOWS):
        cols = slice(g * POOL_GROUP, (g + 1) * POOL_GROUP)
        ug = pbuf[POOL_TAIL:POOL_TAIL + T, cols]
        acc = ug
        for k in range(1, w):
            acc = acc + pbuf[POOL_TAIL - k:POOL_TAIL - k + T, cols]
        mean = acc / jnp.minimum(pos, float(w))
        d_parts.append((mean - ug).astype(jnp.bfloat16))
    y_parts = []
    for p in range(2):
        d_pair = jnp.concatenate(d_parts[2 * p:2 * p + 2], axis=-1)
        y_parts.append(_dot(d_pair, pool_w_ref[p]))
    y_pool = jnp.concatenate(y_parts, axis=-1) * pool_scale_ref[...]
    mix_pool = _rms(y_pool, gn_pool_ref[...]).astype(jnp.bfloat16)

    xc = conv_b_ref[...] + conv_w_ref[CONV_WIDTH - 1:CONV_WIDTH, :] * lbuf[CONV_TAIL:CONV_TAIL + T, :]
    for k in range(CONV_WIDTH - 1):
        off = CONV_TAIL - (CONV_WIDTH - 1) + k
        xc = xc + conv_w_ref[k:k + 1, :] * lbuf[off:off + T, :]
    xcb = xc.astype(jnp.bfloat16)
    half = LRU_WIDTH // 2
    z = [_dot(xcb[:, q * half:(q + 1) * half], gate_w_ref[q]) for q in range(2)]
    za = jnp.concatenate([z[0][:, :half], z[1][:, :half]], axis=-1) + b_a_ref[...]
    zi = jnp.concatenate([z[0][:, half:], z[1][:, half:]], axis=-1) + b_i_ref[...]
    r = _sigmoid(za)
    gate_i = _sigmoid(zi)
    neg_lam = -lam_ref[...]
    softplus = jnp.maximum(neg_lam, 0.0) + jnp.log1p(jnp.exp(-jnp.abs(neg_lam)))
    log_a = (-LRU_C) * r * softplus
    a = jnp.exp(log_a)
    one_minus_a2 = -jnp.tanh(log_a) * (a * a + 1.0)
    bterm = jnp.sqrt(jnp.maximum(one_minus_a2, 0.0)) * (gate_i * xc)

    nq = LRU_WIDTH // LANES
    for q in range(nq):
        for j in range(SUBLANES):
            rows = slice(j * SUB_LEN, (j + 1) * SUB_LEN)
            dst = slice(j * SUB_PITCH, j * SUB_PITCH + SUB_LEN)
            abuf[q, dst, :] = a[rows, q * LANES:(q + 1) * LANES]
            bbuf[q, dst, :] = bterm[rows, q * LANES:(q + 1) * LANES]

    def strided(m):
        return pl.ds(m, SUBLANES, stride=SUB_PITCH)

    prod = [jnp.ones((SUBLANES, LANES), jnp.float32) for _ in range(nq)]
    loc = [jnp.zeros((SUBLANES, LANES), jnp.float32) for _ in range(nq)]
    for m in range(SUB_LEN):
        for q in range(nq):
            am = abuf[q, strided(m), :]
            loc[q] = am * loc[q] + bbuf[q, strided(m), :]
            prod[q] = am * prod[q]
    row = lax.broadcasted_iota(jnp.int32, (SUBLANES, LANES), 0)
    h_in = []
    for q in range(nq):
        carry = hcarry[:, q * LANES:(q + 1) * LANES]
        start = jnp.zeros((SUBLANES, LANES), jnp.float32)
        for j in range(SUBLANES):
            start = jnp.where(row == j, carry, start)
            carry = prod[q][j:j + 1, :] * carry + loc[q][j:j + 1, :]
        h_in.append(start)
        hcarry[:, q * LANES:(q + 1) * LANES] = carry
    for m in range(SUB_LEN):
        for q in range(nq):
            h_in[q] = abuf[q, strided(m), :] * h_in[q] + bbuf[q, strided(m), :]
            hbuf[q, strided(m), :] = h_in[q]
    h_lru = jnp.concatenate(
        [jnp.concatenate([hbuf[q, j * SUB_PITCH:j * SUB_PITCH + SUB_LEN, :] for j in range(SUBLANES)], axis=0)
         for q in range(nq)], axis=-1)

    y_lru = h_lru * _gelu_tanh(u_gate)
    mix_lru = _rms(y_lru, gn_lru_ref[...]).astype(jnp.bfloat16)

    mix = jnp.concatenate([mix_pool, mix_lru], axis=-1)
    h1 = x + _dot(mix, w_out_ref[...])

    n2 = _rms(h1, ln2_ref[...]).astype(jnp.bfloat16)
    ffn = None
    for c0, c1 in FF_CHUNKS:
        gte = _dot(n2, wg_ref[:, c0:c1])
        up = _dot(n2, wu_ref[:, c0:c1])
        act = (gte * _sigmoid(gte) * up).astype(jnp.bfloat16)
        part = _dot(act, wd_ref[c0:c1, :])
        ffn = part if ffn is None else ffn + part
    h2 = h1 + ffn
    o_ref[...] = _rms(h2, lnf_ref[...])

    pbuf[0:POOL_TAIL, :] = pbuf[T:T + POOL_TAIL, :]
    lbuf[0:CONV_TAIL, :] = lbuf[T:T + CONV_TAIL, :]


def _block_diag(blocks):
    n, r, c = blocks.shape
    eye = jnp.eye(n, dtype=blocks.dtype)
    return (eye[:, None, :, None] * blocks[:, :, None, :]).reshape(n * r, n * c)


def kernel(x, ln1_g, w_in, pool_w, pool_scale, conv_w, conv_b, w_a, b_a, w_i, b_i, lam, gn_pool_g,
           gn_lru_g, w_out, ln2_g, w_ffn_gate, w_ffn_up, w_ffn_down, lnf_g):
    B, S, D = x.shape
    assert D == D_MODEL and S % SEQ_TILE == 0 and ln1_g.shape[0] == 1
    bf16 = jnp.bfloat16
    row = lambda v: v.reshape(1, -1).astype(jnp.float32)

    pw = pool_w[0].astype(bf16)
    pool_w2 = jnp.stack([_block_diag(pw[0:2]), _block_diag(pw[2:4])])
    wa = w_a[0].astype(bf16)
    wi = w_i[0].astype(bf16)
    gate_w = jnp.stack([
        jnp.concatenate([_block_diag(wa[4 * q:4 * q + 4]), _block_diag(wi[4 * q:4 * q + 4])], axis=-1)
        for q in range(2)])

    vmem = pl.BlockSpec(memory_space=pltpu.VMEM)
    nq = LRU_WIDTH // LANES
    scan_rows = SUBLANES * SUB_PITCH
    out = pl.pallas_call(
        _layer_kernel,
        grid=(B, S // SEQ_TILE),
        in_specs=[pl.BlockSpec((None, SEQ_TILE, D), lambda b, s: (b, s, 0))] + [vmem] * 18,
        out_specs=pl.BlockSpec((None, SEQ_TILE, D), lambda b, s: (b, s, 0)),
        out_shape=jax.ShapeDtypeStruct((B, S, D), jnp.float32),
        scratch_shapes=[
            pltpu.VMEM((POOL_TAIL + SEQ_TILE, POOL_WIDTH), jnp.float32),
            pltpu.VMEM((CONV_TAIL + SEQ_TILE, LRU_WIDTH), jnp.float32),
            pltpu.VMEM((nq, scan_rows, LANES), jnp.float32),
            pltpu.VMEM((nq, scan_rows, LANES), jnp.float32),
            pltpu.VMEM((nq, scan_rows, LANES), jnp.float32),
            pltpu.VMEM((1, LRU_WIDTH), jnp.float32),
        ],
        compiler_params=pltpu.CompilerParams(
            dimension_semantics=("arbitrary", "arbitrary"),
            vmem_limit_bytes=VMEM_LIMIT_BYTES),
        name="hymba_layer",
    )(x, row(ln1_g), w_in[0].astype(bf16), pool_w2, row(pool_scale), conv_w[0].astype(jnp.float32),
      row(conv_b), gate_w, row(b_a), row(b_i), row(lam), row(gn_pool_g), row(gn_lru_g),
      w_out[0].astype(bf16), row(ln2_g), w_ffn_gate[0].astype(bf16), w_ffn_up[0].astype(bf16),
      w_ffn_down[0].astype(bf16), row(lnf_g))
    return out
```

```python
import functools
import math

import jax
import jax.numpy as jnp
from jax import lax
from jax.experimental import pallas as pl
from jax.experimental.pallas import tpu as pltpu

D_MODEL = 1024
POOL_WINDOWS = (2, 4, 8, 16)
POOL_GROUP = 128
POOL_WIDTH = 512
LRU_WIDTH = 512
LRU_HEADS = 8
LRU_HEAD_DIM = 64
CONV_WIDTH = 4
LRU_C = 8.0
D_FF = 2816
EPS = 1e-6

LANES = 128
SUBLANES = 8
SEQ_TILE = 512
POOL_TAIL = 16
CONV_TAIL = 8
SUB_LEN = SEQ_TILE // SUBLANES
SUB_PITCH = SUB_LEN + SUBLANES
FF_CHUNK = 256
N_FF_CHUNKS = D_FF // FF_CHUNK
FF_DOWN_GROUP = 3
VMEM_LIMIT_BYTES = 58 * 1024 * 1024


def _ffn_sequence():
    ops = []
    for c in range(N_FF_CHUNKS):
        ops.append(("u", c))
        if c % FF_DOWN_GROUP == 0 and c > 0:
            ops.append(("d", c - FF_DOWN_GROUP))
    last = (N_FF_CHUNKS - 1) // FF_DOWN_GROUP * FF_DOWN_GROUP
    return ops + [("d", last)]


def _rms(x, g):
    ms = jnp.mean(x * x, axis=-1, keepdims=True)
    return x * lax.rsqrt(ms + EPS) * g


def _sigmoid(x):
    return 0.5 * (jnp.tanh(0.5 * x) + 1.0)


def _gelu_tanh(x):
    c = math.sqrt(2.0 / math.pi)
    return 0.5 * x * (1.0 + jnp.tanh(c * (x + 0.044715 * (x * x * x))))


def _dot(a, b):
    return jnp.dot(a, b, preferred_element_type=jnp.float32)


def _layer_kernel(x_ref, ln1_ref, w_in_ref, pool_w_ref, pool_scale_ref, conv_w_ref, conv_b_ref,
                  gate_w_ref, b_a_ref, b_i_ref, lam_ref, gn_pool_ref, gn_lru_ref, w_out_ref,
                  ln2_ref, wg_ref, wu_ref, wd_ref, lnf_ref, o_ref,
                  pbuf, lbuf, abuf, bbuf, hbuf, hcarry, h1buf, n2buf, *, chunks_per_seq):
    i = pl.program_id(0)
    s = lax.rem(i, chunks_per_seq)
    T = SEQ_TILE

    @pl.when(s == 0)
    def _():
        pbuf[0:POOL_TAIL, :] = jnp.zeros((POOL_TAIL, POOL_WIDTH), jnp.float32)
        lbuf[0:CONV_TAIL, :] = jnp.zeros((CONV_TAIL, LRU_WIDTH), jnp.float32)
        hcarry[...] = jnp.zeros_like(hcarry)

    @pl.when(i == 0)
    def _():
        h1buf[1] = jnp.zeros((T, D_MODEL), jnp.float32)
        n2buf[1] = jnp.zeros((T, D_MODEL), jnp.bfloat16)

    slot = lax.rem(i, 2)
    prev = 1 - slot

    acts = {}
    ffn_parts = []
    pending = _ffn_sequence()

    def ffn(n):
        for _ in range(n):
            op, c = pending.pop(0)
            if op == "u":
                cols = slice(c * FF_CHUNK, (c + 1) * FF_CHUNK)
                n2 = n2buf[prev]
                half_g = 0.5 * _dot(n2, wg_ref[:, cols])
                up = _dot(n2, wu_ref[:, cols])
                acts[c] = (half_g * (jnp.tanh(half_g) + 1.0) * up).astype(jnp.bfloat16)
            else:
                group = range(c, min(c + FF_DOWN_GROUP, N_FF_CHUNKS))
                act = jnp.concatenate([acts.pop(k) for k in group], axis=-1)
                ffn_parts.append(_dot(act, wd_ref[c * FF_CHUNK:(group[-1] + 1) * FF_CHUNK, :]))

    ffn(1)
    x = x_ref[...]
    nb = _rms(x, ln1_ref[...]).astype(jnp.bfloat16)

    pbuf[POOL_TAIL:POOL_TAIL + T, :] = _dot(nb, w_in_ref[:, 0:POOL_WIDTH])
    lbuf[CONV_TAIL:CONV_TAIL + T, :] = _dot(nb, w_in_ref[:, POOL_WIDTH:POOL_WIDTH + LRU_WIDTH])

    ffn(1)
    n_groups = len(POOL_WINDOWS)
    level = pbuf[...]
    sums = []
    for g, w in enumerate(POOL_WINDOWS):
        assert w == 2 << g and w - 1 <= POOL_TAIL
        rest = level[:, (0 if g == 0 else POOL_GROUP):]
        rest = rest + pltpu.roll(rest, w // 2, axis=0)
        sums.append(rest[POOL_TAIL:, 0:POOL_GROUP])
        level = rest
    head_pos = (s * T + 1 + lax.broadcasted_iota(jnp.int32, (POOL_TAIL, POOL_GROUP), 0)).astype(jnp.float32)
    d_parts = []
    for g, w in enumerate(POOL_WINDOWS):
        ug = pbuf[POOL_TAIL:POOL_TAIL + T, g * POOL_GROUP:(g + 1) * POOL_GROUP]
        head = sums[g][0:POOL_TAIL] / jnp.minimum(head_pos, float(w))
        mean = jnp.concatenate([head, sums[g][POOL_TAIL:] * (1.0 / w)], axis=0)
        d_parts.append((mean - ug).astype(jnp.bfloat16))

    u_gate = _dot(nb, w_in_ref[:, POOL_WIDTH + LRU_WIDTH:])
    y_parts = []
    for p in range(n_groups // 2):
        d_pair = jnp.concatenate(d_parts[2 * p:2 * p + 2], axis=-1)
        y_parts.append(_dot(d_pair, pool_w_ref[p]))

    ffn(1)
    xc = conv_b_ref[...] + conv_w_ref[CONV_WIDTH - 1:CONV_WIDTH, :] * lbuf[CONV_TAIL:CONV_TAIL + T, :]
    for k in range(CONV_WIDTH - 1):
        off = CONV_TAIL - (CONV_WIDTH - 1) + k
        xc = xc + conv_w_ref[k:k + 1, :] * lbuf[off:off + T, :]
    xcb = xc.astype(jnp.bfloat16)
    half = LRU_WIDTH // 2
    z = [_dot(xcb[:, q * half:(q + 1) * half], gate_w_ref[q]) for q in range(2)]

    ffn(2)
    y_pool = jnp.concatenate(y_parts, axis=-1) * pool_scale_ref[...]
    mix_pool = _rms(y_pool, gn_pool_ref[...]).astype(jnp.bfloat16)

    za = jnp.concatenate([z[0][:, :half], z[1][:, :half]], axis=-1) + b_a_ref[...]
    zi = jnp.concatenate([z[0][:, half:], z[1][:, half:]], axis=-1) + b_i_ref[...]
    r = _sigmoid(za)
    gate_i = _sigmoid(zi)
    neg_lam = -lam_ref[...]
    softplus = jnp.maximum(neg_lam, 0.0) + jnp.log1p(jnp.exp(-jnp.abs(neg_lam)))
    neg_log_a = r * (LRU_C * softplus)
    a = jnp.exp(-neg_log_a)
    one_minus_a2 = jnp.tanh(neg_log_a) * (a * a + 1.0)
    bterm = jnp.sqrt(jnp.maximum(one_minus_a2, 0.0)) * (gate_i * xc)

    nq = LRU_WIDTH // LANES
    tiles_per_sub = SUB_LEN // SUBLANES
    for q in range(nq):
        lanes = slice(q * LANES, (q + 1) * LANES)
        for t in range(T // SUBLANES):
            j, m0 = divmod(t, tiles_per_sub)
            dst = pl.ds(m0 * SUBLANES * SUBLANES + j, SUBLANES, stride=SUBLANES)
            rows = slice(t * SUBLANES, (t + 1) * SUBLANES)
            abuf[q, dst, :] = a[rows, lanes]
            bbuf[q, dst, :] = bterm[rows, lanes]

    def tile(m):
        return slice(m * SUBLANES, (m + 1) * SUBLANES)

    ffn(2)
    prod = [jnp.ones((SUBLANES, LANES), jnp.float32) for _ in range(nq)]
    loc = [jnp.zeros((SUBLANES, LANES), jnp.float32) for _ in range(nq)]
    for m in range(SUB_LEN):
        for q in range(nq):
            am = abuf[q, tile(m), :]
            loc[q] = am * loc[q] + bbuf[q, tile(m), :]
            prod[q] = am * prod[q]
    row = lax.broadcasted_iota(jnp.int32, (SUBLANES, LANES), 0)
    h_in = []
    for q in range(nq):
        carry = hcarry[:, q * LANES:(q + 1) * LANES]
        start = jnp.zeros((SUBLANES, LANES), jnp.float32)
        for j in range(SUBLANES):
            start = jnp.where(row == j, carry, start)
            carry = prod[q][j:j + 1, :] * carry + loc[q][j:j + 1, :]
        h_in.append(start)
        hcarry[:, q * LANES:(q + 1) * LANES] = carry
    ffn(2)
    for m in range(SUB_LEN):
        for q in range(nq):
            h_in[q] = abuf[q, tile(m), :] * h_in[q] + bbuf[q, tile(m), :]
            hbuf[q, pl.ds(m, SUBLANES, stride=SUB_PITCH), :] = h_in[q]
    h_lru = jnp.concatenate(
        [jnp.concatenate([hbuf[q, j * SUB_PITCH:j * SUB_PITCH + SUB_LEN, :] for j in range(SUBLANES)], axis=0)
         for q in range(nq)], axis=-1)

    ffn(2)
    y_lru = h_lru * _gelu_tanh(u_gate)
    mix_lru = _rms(y_lru, gn_lru_ref[...]).astype(jnp.bfloat16)

    mix = jnp.concatenate([mix_pool, mix_lru], axis=-1)
    h1_new = x + _dot(mix, w_out_ref[...])
    ffn(2)
    h1buf[slot] = h1_new
    n2buf[slot] = _rms(h1_new, ln2_ref[...]).astype(jnp.bfloat16)

    ffn(len(pending))
    ffn_out = ffn_parts[0]
    for part in ffn_parts[1:]:
        ffn_out = ffn_out + part
    o_ref[...] = _rms(h1buf[prev] + ffn_out, lnf_ref[...])

    pbuf[0:POOL_TAIL, :] = pbuf[T:T + POOL_TAIL, :]
    lbuf[0:CONV_TAIL, :] = lbuf[T:T + CONV_TAIL, :]


def _block_diag(blocks):
    n, r, c = blocks.shape
    eye = jnp.eye(n, dtype=blocks.dtype)
    return (eye[:, None, :, None] * blocks[:, :, None, :]).reshape(n * r, n * c)


def kernel(x, ln1_g, w_in, pool_w, pool_scale, conv_w, conv_b, w_a, b_a, w_i, b_i, lam, gn_pool_g,
           gn_lru_g, w_out, ln2_g, w_ffn_gate, w_ffn_up, w_ffn_down, lnf_g):
    B, S, D = x.shape
    assert D == D_MODEL and S % SEQ_TILE == 0 and ln1_g.shape[0] == 1
    bf16 = jnp.bfloat16
    row = lambda v: v.reshape(1, -1).astype(jnp.float32)

    pw = pool_w[0].astype(bf16)
    pool_w2 = jnp.stack([_block_diag(pw[0:2]), _block_diag(pw[2:4])])
    wa = w_a[0].astype(bf16)
    wi = w_i[0].astype(bf16)
    gate_w = jnp.stack([
        jnp.concatenate([_block_diag(wa[4 * q:4 * q + 4]), _block_diag(wi[4 * q:4 * q + 4])], axis=-1)
        for q in range(2)])

    vmem = pl.BlockSpec(memory_space=pltpu.VMEM)
    nq = LRU_WIDTH // LANES
    scan_rows = SUBLANES * SUB_PITCH
    cps = S // SEQ_TILE
    n_chunks = B * cps

    def in_index(i):
        c = jnp.minimum(i, n_chunks - 1)
        return (c // cps, c % cps, 0)

    def out_index(i):
        c = jnp.maximum(i - 1, 0)
        return (c // cps, c % cps, 0)

    out = pl.pallas_call(
        functools.partial(_layer_kernel, chunks_per_seq=cps),
        grid=(n_chunks + 1,),
        in_specs=[pl.BlockSpec((None, SEQ_TILE, D), in_index)] + [vmem] * 18,
        out_specs=pl.BlockSpec((None, SEQ_TILE, D), out_index),
        out_shape=jax.ShapeDtypeStruct((B, S, D), jnp.float32),
        scratch_shapes=[
            pltpu.VMEM((POOL_TAIL + SEQ_TILE, POOL_WIDTH), jnp.float32),
            pltpu.VMEM((CONV_TAIL + SEQ_TILE, LRU_WIDTH), jnp.float32),
            pltpu.VMEM((nq, SEQ_TILE, LANES), jnp.float32),
            pltpu.VMEM((nq, SEQ_TILE, LANES), jnp.float32),
            pltpu.VMEM((nq, scan_rows, LANES), jnp.float32),
            pltpu.VMEM((1, LRU_WIDTH), jnp.float32),
            pltpu.VMEM((2, SEQ_TILE, D), jnp.float32),
            pltpu.VMEM((2, SEQ_TILE, D), jnp.bfloat16),
        ],
        compiler_params=pltpu.CompilerParams(
            dimension_semantics=("arbitrary",),
            vmem_limit_bytes=VMEM_LIMIT_BYTES),
        name="hymba_layer",
    )(x, row(ln1_g), w_in[0].astype(bf16), pool_w2, row(pool_scale), conv_w[0].astype(jnp.float32),
      row(conv_b), gate_w, row(b_a), row(b_i), row(lam), row(gn_pool_g), row(gn_lru_g),
      w_out[0].astype(bf16), row(ln2_g), w_ffn_gate[0].astype(bf16), w_ffn_up[0].astype(bf16),
      w_ffn_down[0].astype(bf16), row(lnf_g))
    return out
```

```python
import functools
import math

import jax
import jax.numpy as jnp
from jax import lax
from jax.experimental import pallas as pl
from jax.experimental.pallas import tpu as pltpu

D_MODEL = 1024
POOL_WINDOWS = (2, 4, 8, 16)
POOL_GROUP = 128
POOL_WIDTH = 512
LRU_WIDTH = 512
LRU_HEADS = 8
LRU_HEAD_DIM = 64
CONV_WIDTH = 4
LRU_C = 8.0
D_FF = 2816
IN_WIDTH = POOL_WIDTH + 2 * LRU_WIDTH
EPS = 1e-6

LANES = 128
SUBLANES = 8
SEQ_TILE = 512
POOL_TAIL = 16
CONV_TAIL = 8
SUB_LEN = SEQ_TILE // SUBLANES
SUB_PITCH = SUB_LEN + SUBLANES
FF_CHUNK = 256
N_FF_CHUNKS = D_FF // FF_CHUNK
FF_DOWN_GROUP = 3
HBM_WEIGHT_OPERANDS = (2, 13, 15, 16, 17)
WEIGHT_STAGE_ROWS = 256
VMEM_LIMIT_BYTES = 58 * 1024 * 1024


def _ffn_sequence():
    ops = []
    for c in range(N_FF_CHUNKS):
        ops.append(("u", c))
        if c % FF_DOWN_GROUP == 0 and c > 0:
            ops.append(("d", c - FF_DOWN_GROUP))
    last = (N_FF_CHUNKS - 1) // FF_DOWN_GROUP * FF_DOWN_GROUP
    return ops + [("d", last)]


def _rms(x, g):
    ms = jnp.mean(x * x, axis=-1, keepdims=True)
    return x * lax.rsqrt(ms + EPS) * g


def _sigmoid(x):
    return 0.5 * (jnp.tanh(0.5 * x) + 1.0)


def _gelu_tanh(x):
    c = math.sqrt(2.0 / math.pi)
    return 0.5 * x * (1.0 + jnp.tanh(c * (x + 0.044715 * (x * x * x))))


def _dot(a, b):
    return jnp.dot(a, b, preferred_element_type=jnp.float32)


def _load_weight_bf16(src_hbm, dst, stage, sems):
    rows, cols = src_hbm.shape
    n = rows // WEIGHT_STAGE_ROWS

    def copy(r, slot):
        src = src_hbm.at[pl.ds(r * WEIGHT_STAGE_ROWS, WEIGHT_STAGE_ROWS), :]
        return pltpu.make_async_copy(src, stage.at[slot, :, pl.ds(0, cols)], sems.at[slot])

    copy(0, 0).start()

    def body(r, carry):
        slot = lax.rem(r, 2)

        @pl.when(r + 1 < n)
        def _():
            copy(r + 1, 1 - slot).start()

        copy(r, slot).wait()
        row0 = pl.multiple_of(r * WEIGHT_STAGE_ROWS, WEIGHT_STAGE_ROWS)
        dst[pl.ds(row0, WEIGHT_STAGE_ROWS), :] = stage[slot, :, 0:cols].astype(jnp.bfloat16)
        return carry

    lax.fori_loop(0, n, body, 0)


def _layer_kernel(x_ref, ln1_ref, w_in_hbm, pool_w_ref, pool_scale_ref, conv_w_ref, conv_b_ref,
                  gate_w_ref, b_a_ref, b_i_ref, lam_ref, gn_pool_ref, gn_lru_ref, w_out_hbm,
                  ln2_ref, wg_hbm, wu_hbm, wd_hbm, lnf_ref, o_ref,
                  pbuf, lbuf, abuf, bbuf, hbuf, hcarry, h1buf, n2buf,
                  w_in_ref, w_out_ref, wg_ref, wu_ref, wd_ref, stage, stage_sems, *, chunks_per_seq):
    i = pl.program_id(0)
    s = lax.rem(i, chunks_per_seq)
    T = SEQ_TILE

    @pl.when(i == 0)
    def _():
        for src, dst in ((w_in_hbm, w_in_ref), (w_out_hbm, w_out_ref), (wg_hbm, wg_ref),
                         (wu_hbm, wu_ref), (wd_hbm, wd_ref)):
            _load_weight_bf16(src, dst, stage, stage_sems)

    @pl.when(s == 0)
    def _():
        pbuf[0:POOL_TAIL, :] = jnp.zeros((POOL_TAIL, POOL_WIDTH), jnp.float32)
        lbuf[0:CONV_TAIL, :] = jnp.zeros((CONV_TAIL, LRU_WIDTH), jnp.float32)
        hcarry[...] = jnp.zeros_like(hcarry)

    @pl.when(i == 0)
    def _():
        h1buf[1] = jnp.zeros((T, D_MODEL), jnp.float32)
        n2buf[1] = jnp.zeros((T, D_MODEL), jnp.bfloat16)

    slot = lax.rem(i, 2)
    prev = 1 - slot

    acts = {}
    ffn_parts = []
    pending = _ffn_sequence()

    def ffn(n):
        for _ in range(n):
            op, c = pending.pop(0)
            if op == "u":
                cols = slice(c * FF_CHUNK, (c + 1) * FF_CHUNK)
                n2 = n2buf[prev]
                half_g = 0.5 * _dot(n2, wg_ref[:, cols])
                up = _dot(n2, wu_ref[:, cols])
                acts[c] = (half_g * (jnp.tanh(half_g) + 1.0) * up).astype(jnp.bfloat16)
            else:
                group = range(c, min(c + FF_DOWN_GROUP, N_FF_CHUNKS))
                act = jnp.concatenate([acts.pop(k) for k in group], axis=-1)
                ffn_parts.append(_dot(act, wd_ref[c * FF_CHUNK:(group[-1] + 1) * FF_CHUNK, :]))

    ffn(1)
    x = x_ref[...]
    nb = _rms(x, ln1_ref[...]).astype(jnp.bfloat16)

    pbuf[POOL_TAIL:POOL_TAIL + T, :] = _dot(nb, w_in_ref[:, 0:POOL_WIDTH])
    lbuf[CONV_TAIL:CONV_TAIL + T, :] = _dot(nb, w_in_ref[:, POOL_WIDTH:POOL_WIDTH + LRU_WIDTH])

    ffn(1)
    n_groups = len(POOL_WINDOWS)
    level = pbuf[...]
    sums = []
    for g, w in enumerate(POOL_WINDOWS):
        assert w == 2 << g and w - 1 <= POOL_TAIL
        rest = level[:, (0 if g == 0 else POOL_GROUP):]
        rest = rest + pltpu.roll(rest, w // 2, axis=0)
        sums.append(rest[POOL_TAIL:, 0:POOL_GROUP])
        level = rest
    head_pos = (s * T + 1 + lax.broadcasted_iota(jnp.int32, (POOL_TAIL, POOL_GROUP), 0)).astype(jnp.float32)
    d_parts = []
    for g, w in enumerate(POOL_WINDOWS):
        ug = pbuf[POOL_TAIL:POOL_TAIL + T, g * POOL_GROUP:(g + 1) * POOL_GROUP]
        head = sums[g][0:POOL_TAIL] / jnp.minimum(head_pos, float(w))
        mean = jnp.concatenate([head, sums[g][POOL_TAIL:] * (1.0 / w)], axis=0)
        d_parts.append((mean - ug).astype(jnp.bfloat16))

    u_gate = _dot(nb, w_in_ref[:, POOL_WIDTH + LRU_WIDTH:])
    y_parts = []
    for p in range(n_groups // 2):
        d_pair = jnp.concatenate(d_parts[2 * p:2 * p + 2], axis=-1)
        y_parts.append(_dot(d_pair, pool_w_ref[p]))

    ffn(1)
    xc = conv_b_ref[...] + conv_w_ref[CONV_WIDTH - 1:CONV_WIDTH, :] * lbuf[CONV_TAIL:CONV_TAIL + T, :]
    for k in range(CONV_WIDTH - 1):
        off = CONV_TAIL - (CONV_WIDTH - 1) + k
        xc = xc + conv_w_ref[k:k + 1, :] * lbuf[off:off + T, :]
    xcb = xc.astype(jnp.bfloat16)
    half = LRU_WIDTH // 2
    z = [_dot(xcb[:, q * half:(q + 1) * half], gate_w_ref[q]) for q in range(2)]

    ffn(2)
    y_pool = jnp.concatenate(y_parts, axis=-1) * pool_scale_ref[...]
    mix_pool = _rms(y_pool, gn_pool_ref[...]).astype(jnp.bfloat16)

    za = jnp.concatenate([z[0][:, :half], z[1][:, :half]], axis=-1) + b_a_ref[...]
    zi = jnp.concatenate([z[0][:, half:], z[1][:, half:]], axis=-1) + b_i_ref[...]
    r = _sigmoid(za)
    gate_i = _sigmoid(zi)
    neg_lam = -lam_ref[...]
    softplus = jnp.maximum(neg_lam, 0.0) + jnp.log1p(jnp.exp(-jnp.abs(neg_lam)))
    neg_log_a = r * (LRU_C * softplus)
    a = jnp.exp(-neg_log_a)
    one_minus_a2 = jnp.tanh(neg_log_a) * (a * a + 1.0)
    bterm = jnp.sqrt(jnp.maximum(one_minus_a2, 0.0)) * (gate_i * xc)

    nq = LRU_WIDTH // LANES
    tiles_per_sub = SUB_LEN // SUBLANES
    for q in range(nq):
        lanes = slice(q * LANES, (q + 1) * LANES)
        for t in range(T // SUBLANES):
            j, m0 = divmod(t, tiles_per_sub)
            dst = pl.ds(m0 * SUBLANES * SUBLANES + j, SUBLANES, stride=SUBLANES)
            rows = slice(t * SUBLANES, (t + 1) * SUBLANES)
            abuf[q, dst, :] = a[rows, lanes]
            bbuf[q, dst, :] = bterm[rows, lanes]

    def tile(m):
        return slice(m * SUBLANES, (m + 1) * SUBLANES)

    ffn(2)
    prod = [jnp.ones((SUBLANES, LANES), jnp.float32) for _ in range(nq)]
    loc = [jnp.zeros((SUBLANES, LANES), jnp.float32) for _ in range(nq)]
    for m in range(SUB_LEN):
        for q in range(nq):
            am = abuf[q, tile(m), :]
            loc[q] = am * loc[q] + bbuf[q, tile(m), :]
            prod[q] = am * prod[q]
    row = lax.broadcasted_iota(jnp.int32, (SUBLANES, LANES), 0)
    h_in = []
    for q in range(nq):
        carry = hcarry[:, q * LANES:(q + 1) * LANES]
        start = jnp.zeros((SUBLANES, LANES), jnp.float32)
        for j in range(SUBLANES):
            start = jnp.where(row == j, carry, start)
            carry = prod[q][j:j + 1, :] * carry + loc[q][j:j + 1, :]
        h_in.append(start)
        hcarry[:, q * LANES:(q + 1) * LANES] = carry
    ffn(2)
    for m in range(SUB_LEN):
        for q in range(nq):
            h_in[q] = abuf[q, tile(m), :] * h_in[q] + bbuf[q, tile(m), :]
            hbuf[q, pl.ds(m, SUBLANES, stride=SUB_PITCH), :] = h_in[q]
    h_lru = jnp.concatenate(
        [jnp.concatenate([hbuf[q, j * SUB_PITCH:j * SUB_PITCH + SUB_LEN, :] for j in range(SUBLANES)], axis=0)
         for q in range(nq)], axis=-1)

    ffn(2)
    y_lru = h_lru * _gelu_tanh(u_gate)
    mix_lru = _rms(y_lru, gn_lru_ref[...]).astype(jnp.bfloat16)

    mix = jnp.concatenate([mix_pool, mix_lru], axis=-1)
    h1_new = x + _dot(mix, w_out_ref[...])
    ffn(2)
    h1buf[slot] = h1_new
    n2buf[slot] = _rms(h1_new, ln2_ref[...]).astype(jnp.bfloat16)

    ffn(len(pending))
    ffn_out = ffn_parts[0]
    for part in ffn_parts[1:]:
        ffn_out = ffn_out + part
    o_ref[...] = _rms(h1buf[prev] + ffn_out, lnf_ref[...])

    pbuf[0:POOL_TAIL, :] = pbuf[T:T + POOL_TAIL, :]
    lbuf[0:CONV_TAIL, :] = lbuf[T:T + CONV_TAIL, :]


def _block_diag(blocks):
    n, r, c = blocks.shape
    eye = jnp.eye(n, dtype=blocks.dtype)
    return (eye[:, None, :, None] * blocks[:, :, None, :]).reshape(n * r, n * c)


def kernel(x, ln1_g, w_in, pool_w, pool_scale, conv_w, conv_b, w_a, b_a, w_i, b_i, lam, gn_pool_g,
           gn_lru_g, w_out, ln2_g, w_ffn_gate, w_ffn_up, w_ffn_down, lnf_g):
    B, S, D = x.shape
    assert D == D_MODEL and S % SEQ_TILE == 0 and ln1_g.shape[0] == 1
    bf16 = jnp.bfloat16
    row = lambda v: v.reshape(1, -1).astype(jnp.float32)

    pw = pool_w[0].astype(bf16)
    pool_w2 = jnp.stack([_block_diag(pw[0:2]), _block_diag(pw[2:4])])
    wa = w_a[0].astype(bf16)
    wi = w_i[0].astype(bf16)
    gate_w = jnp.stack([
        jnp.concatenate([_block_diag(wa[4 * q:4 * q + 4]), _block_diag(wi[4 * q:4 * q + 4])], axis=-1)
        for q in range(2)])

    vmem = pl.BlockSpec(memory_space=pltpu.VMEM)
    hbm = pl.BlockSpec(memory_space=pl.ANY)
    nq = LRU_WIDTH // LANES
    scan_rows = SUBLANES * SUB_PITCH
    cps = S // SEQ_TILE
    n_chunks = B * cps

    def in_index(i):
        c = jnp.minimum(i, n_chunks - 1)
        return (c // cps, c % cps, 0)

    def out_index(i):
        c = jnp.maximum(i - 1, 0)
        return (c // cps, c % cps, 0)

    out = pl.pallas_call(
        functools.partial(_layer_kernel, chunks_per_seq=cps),
        grid=(n_chunks + 1,),
        in_specs=[pl.BlockSpec((None, SEQ_TILE, D), in_index)] + [
            hbm if k in HBM_WEIGHT_OPERANDS else vmem for k in range(1, 19)],
        out_specs=pl.BlockSpec((None, SEQ_TILE, D), out_index),
        out_shape=jax.ShapeDtypeStruct((B, S, D), jnp.float32),
        scratch_shapes=[
            pltpu.VMEM((POOL_TAIL + SEQ_TILE, POOL_WIDTH), jnp.float32),
            pltpu.VMEM((CONV_TAIL + SEQ_TILE, LRU_WIDTH), jnp.float32),
            pltpu.VMEM((nq, SEQ_TILE, LANES), jnp.float32),
            pltpu.VMEM((nq, SEQ_TILE, LANES), jnp.float32),
            pltpu.VMEM((nq, scan_rows, LANES), jnp.float32),
            pltpu.VMEM((1, LRU_WIDTH), jnp.float32),
            pltpu.VMEM((2, SEQ_TILE, D), jnp.float32),
            pltpu.VMEM((2, SEQ_TILE, D), jnp.bfloat16),
            pltpu.VMEM((D, IN_WIDTH), bf16),
            pltpu.VMEM((D, D), bf16),
            pltpu.VMEM((D, D_FF), bf16),
            pltpu.VMEM((D, D_FF), bf16),
            pltpu.VMEM((D_FF, D), bf16),
            pltpu.VMEM((2, WEIGHT_STAGE_ROWS, D_FF), jnp.float32),
            pltpu.SemaphoreType.DMA((2,)),
        ],
        compiler_params=pltpu.CompilerParams(
            dimension_semantics=("arbitrary",),
            vmem_limit_bytes=VMEM_LIMIT_BYTES),
        name="hymba_layer",
    )(x, row(ln1_g), w_in[0], pool_w2, row(pool_scale), conv_w[0].astype(jnp.float32),
      row(conv_b), gate_w, row(b_a), row(b_i), row(lam), row(gn_pool_g), row(gn_lru_g),
      w_out[0], row(ln2_g), w_ffn_gate[0], w_ffn_up[0], w_ffn_down[0], row(lnf_g))
    return out
```

```python
import functools
import math

import jax
import jax.numpy as jnp
from jax import lax
from jax.experimental import pallas as pl
from jax.experimental.pallas import tpu as pltpu

D_MODEL = 1024
POOL_WINDOWS = (2, 4, 8, 16)
POOL_GROUP = 128
POOL_WIDTH = 512
LRU_WIDTH = 512
LRU_HEADS = 8
LRU_HEAD_DIM = 64
CONV_WIDTH = 4
LRU_C = 8.0
D_FF = 2816
IN_WIDTH = POOL_WIDTH + 2 * LRU_WIDTH
EPS = 1e-6

LANES = 128
SUBLANES = 8
SEQ_TILE = 512
POOL_TAIL = 16
CONV_TAIL = 8
SUB_LEN = SEQ_TILE // SUBLANES
SUB_PITCH = SUB_LEN + SUBLANES
FF_CHUNK = 256
N_FF_CHUNKS = D_FF // FF_CHUNK
FF_DOWN_GROUP = 3
WEIGHT_STAGE_ROWS = 256
WEIGHT_STAGE_SLOTS = 3
ROW_LN1, ROW_LN2, ROW_LNF, ROW_POOL, ROW_LRU, ROW_GATE_B, ROW_LAM, ROW_CONV_W = 0, 1, 2, 3, 4, 5, 6, 8
VEC_ROWS = 16
VMEM_LIMIT_BYTES = 58 * 1024 * 1024


def _ffn_sequence():
    ops = []
    for c in range(N_FF_CHUNKS):
        ops.append(("u", c))
        if c % FF_DOWN_GROUP == 0 and c > 0:
            ops.append(("d", c - FF_DOWN_GROUP))
    last = (N_FF_CHUNKS - 1) // FF_DOWN_GROUP * FF_DOWN_GROUP
    return ops + [("d", last)]


def _rms(x, g):
    ms = jnp.mean(x * x, axis=-1, keepdims=True)
    return x * lax.rsqrt(ms + EPS) * g


def _sigmoid(x):
    return 0.5 * (jnp.tanh(0.5 * x) + 1.0)


def _gelu_tanh(x):
    c = math.sqrt(2.0 / math.pi)
    return 0.5 * x * (1.0 + jnp.tanh(c * (x + 0.044715 * (x * x * x))))


def _dot(a, b):
    return jnp.dot(a, b, preferred_element_type=jnp.float32)


def _load_weight_bf16(src_hbm, dst, stage, sems):
    rows, cols = src_hbm.shape
    n = rows // WEIGHT_STAGE_ROWS

    def copy(r, slot):
        src = src_hbm.at[pl.ds(r * WEIGHT_STAGE_ROWS, WEIGHT_STAGE_ROWS), :]
        return pltpu.make_async_copy(src, stage.at[slot, :, pl.ds(0, cols)], sems.at[slot])

    ahead = WEIGHT_STAGE_SLOTS - 1
    for r in range(min(ahead, n)):
        copy(r, r % WEIGHT_STAGE_SLOTS).start()

    def body(r, carry):
        slot = lax.rem(r, WEIGHT_STAGE_SLOTS)

        @pl.when(r + ahead < n)
        def _():
            copy(r + ahead, lax.rem(r + ahead, WEIGHT_STAGE_SLOTS)).start()

        copy(r, slot).wait()
        row0 = pl.multiple_of(r * WEIGHT_STAGE_ROWS, WEIGHT_STAGE_ROWS)
        dst[pl.ds(row0, WEIGHT_STAGE_ROWS), :] = stage[slot, :, 0:cols].astype(jnp.bfloat16)
        return carry

    lax.fori_loop(0, n, body, 0)


def _layer_kernel(x_ref, vec_ref, small_w_ref, w_in_hbm, w_out_hbm, wg_hbm, wu_hbm, wd_hbm, o_ref,
                  pbuf, lbuf, abuf, bbuf, hbuf, hcarry, h1buf, n2buf,
                  w_in_ref, w_out_ref, wg_ref, wu_ref, wd_ref, stage, stage_sems, *, chunks_per_seq):
    i = pl.program_id(0)
    s = lax.rem(i, chunks_per_seq)
    T = SEQ_TILE
    half = LRU_WIDTH // 2

    def vec(r, part=None):
        if part is None:
            return vec_ref[r:r + 1, :]
        return vec_ref[r:r + 1, part * LRU_WIDTH:(part + 1) * LRU_WIDTH]

    @pl.when(i == 0)
    def _():
        for src, dst in ((w_in_hbm, w_in_ref), (w_out_hbm, w_out_ref), (wg_hbm, wg_ref),
                         (wu_hbm, wu_ref), (wd_hbm, wd_ref)):
            _load_weight_bf16(src, dst, stage, stage_sems)

    @pl.when(s == 0)
    def _():
        pbuf[0:POOL_TAIL, :] = jnp.zeros((POOL_TAIL, POOL_WIDTH), jnp.float32)
        lbuf[0:CONV_TAIL, :] = jnp.zeros((CONV_TAIL, LRU_WIDTH), jnp.float32)
        hcarry[...] = jnp.zeros_like(hcarry)

    @pl.when(i == 0)
    def _():
        h1buf[1] = jnp.zeros((T, D_MODEL), jnp.float32)
        n2buf[1] = jnp.zeros((T, D_MODEL), jnp.bfloat16)

    slot = lax.rem(i, 2)
    prev = 1 - slot

    acts = {}
    ffn_parts = []
    pending = _ffn_sequence()

    def ffn(n):
        for _ in range(n):
            op, c = pending.pop(0)
            if op == "u":
                cols = slice(c * FF_CHUNK, (c + 1) * FF_CHUNK)
                n2 = n2buf[prev]
                half_g = 0.5 * _dot(n2, wg_ref[:, cols])
                up = _dot(n2, wu_ref[:, cols])
                acts[c] = (half_g * (jnp.tanh(half_g) + 1.0) * up).astype(jnp.bfloat16)
            else:
                group = range(c, min(c + FF_DOWN_GROUP, N_FF_CHUNKS))
                act = jnp.concatenate([acts.pop(k) for k in group], axis=-1)
                ffn_parts.append(_dot(act, wd_ref[c * FF_CHUNK:(group[-1] + 1) * FF_CHUNK, :]))

    ffn(1)
    x = x_ref[...]
    nb = _rms(x, vec(ROW_LN1)).astype(jnp.bfloat16)

    pbuf[POOL_TAIL:POOL_TAIL + T, :] = _dot(nb, w_in_ref[:, 0:POOL_WIDTH])
    lbuf[CONV_TAIL:CONV_TAIL + T, :] = _dot(nb, w_in_ref[:, POOL_WIDTH:POOL_WIDTH + LRU_WIDTH])

    ffn(1)
    n_groups = len(POOL_WINDOWS)
    level = pbuf[...]
    sums = []
    for g, w in enumerate(POOL_WINDOWS):
        assert w == 2 << g and w - 1 <= POOL_TAIL
        rest = level[:, (0 if g == 0 else POOL_GROUP):]
        rest = rest + pltpu.roll(rest, w // 2, axis=0)
        sums.append(rest[POOL_TAIL:, 0:POOL_GROUP])
        level = rest
    head_pos = (s * T + 1 + lax.broadcasted_iota(jnp.int32, (POOL_TAIL, POOL_GROUP), 0)).astype(jnp.float32)
    d_parts = []
    for g, w in enumerate(POOL_WINDOWS):
        ug = pbuf[POOL_TAIL:POOL_TAIL + T, g * POOL_GROUP:(g + 1) * POOL_GROUP]
        head = sums[g][0:POOL_TAIL] / jnp.minimum(head_pos, float(w))
        mean = jnp.concatenate([head, sums[g][POOL_TAIL:] * (1.0 / w)], axis=0)
        d_parts.append((mean - ug).astype(jnp.bfloat16))

    u_gate = _dot(nb, w_in_ref[:, POOL_WIDTH + LRU_WIDTH:])
    y_parts = []
    for p in range(n_groups // 2):
        d_pair = jnp.concatenate(d_parts[2 * p:2 * p + 2], axis=-1)
        y_parts.append(_dot(d_pair, small_w_ref[p, :, 2 * half:3 * half]))

    ffn(1)
    xc = vec(ROW_LRU, 0) + vec(ROW_CONV_W + CONV_WIDTH - 1, 0) * lbuf[CONV_TAIL:CONV_TAIL + T, :]
    for k in range(CONV_WIDTH - 1):
        off = CONV_TAIL - (CONV_WIDTH - 1) + k
        xc = xc + vec(ROW_CONV_W + k, 0) * lbuf[off:off + T, :]
    xcb = xc.astype(jnp.bfloat16)
    z = [_dot(xcb[:, q * half:(q + 1) * half], small_w_ref[q, :, 0:2 * half])
         for q in range(2)]

    ffn(2)
    y_pool = jnp.concatenate(y_parts, axis=-1) * vec(ROW_POOL, 0)
    mix_pool = _rms(y_pool, vec(ROW_POOL, 1)).astype(jnp.bfloat16)

    za = jnp.concatenate([z[0][:, :half], z[1][:, :half]], axis=-1) + vec(ROW_GATE_B, 0)
    zi = jnp.concatenate([z[0][:, half:], z[1][:, half:]], axis=-1) + vec(ROW_GATE_B, 1)
    r = _sigmoid(za)
    gate_i = _sigmoid(zi)
    neg_lam = -vec(ROW_LAM, 0)
    softplus = jnp.maximum(neg_lam, 0.0) + jnp.log1p(jnp.exp(-jnp.abs(neg_lam)))
    neg_log_a = r * (LRU_C * softplus)
    a = jnp.exp(-neg_log_a)
    one_minus_a2 = jnp.tanh(neg_log_a) * (a * a + 1.0)
    bterm = jnp.sqrt(jnp.maximum(one_minus_a2, 0.0)) * (gate_i * xc)

    nq = LRU_WIDTH // LANES
    tiles_per_sub = SUB_LEN // SUBLANES
    for q in range(nq):
        lanes = slice(q * LANES, (q + 1) * LANES)
        for t in range(T // SUBLANES):
            j, m0 = divmod(t, tiles_per_sub)
            dst = pl.ds(m0 * SUBLANES * SUBLANES + j, SUBLANES, stride=SUBLANES)
            rows = slice(t * SUBLANES, (t + 1) * SUBLANES)
            abuf[q, dst, :] = a[rows, lanes]
            bbuf[q, dst, :] = bterm[rows, lanes]

    def tile(m):
        return slice(m * SUBLANES, (m + 1) * SUBLANES)

    ffn(2)
    prod = [jnp.ones((SUBLANES, LANES), jnp.float32) for _ in range(nq)]
    loc = [jnp.zeros((SUBLANES, LANES), jnp.float32) for _ in range(nq)]
    for m in range(SUB_LEN):
        for q in range(nq):
            am = abuf[q, tile(m), :]
            loc[q] = am * loc[q] + bbuf[q, tile(m), :]
            prod[q] = am * prod[q]
    row = lax.broadcasted_iota(jnp.int32, (SUBLANES, LANES), 0)
    h_in = []
    for q in range(nq):
        carry = hcarry[:, q * LANES:(q + 1) * LANES]
        start = jnp.zeros((SUBLANES, LANES), jnp.float32)
        for j in range(SUBLANES):
            start = jnp.where(row == j, carry, start)
            carry = prod[q][j:j + 1, :] * carry + loc[q][j:j + 1, :]
        h_in.append(start)
        hcarry[:, q * LANES:(q + 1) * LANES] = carry
    ffn(2)
    for m in range(SUB_LEN):
        for q in range(nq):
            h_in[q] = abuf[q, tile(m), :] * h_in[q] + bbuf[q, tile(m), :]
            hbuf[q, pl.ds(m, SUBLANES, stride=SUB_PITCH), :] = h_in[q]
    h_lru = jnp.concatenate(
        [jnp.concatenate([hbuf[q, j * SUB_PITCH:j * SUB_PITCH + SUB_LEN, :] for j in range(SUBLANES)], axis=0)
         for q in range(nq)], axis=-1)

    ffn(2)
    y_lru = h_lru * _gelu_tanh(u_gate)
    mix_lru = _rms(y_lru, vec(ROW_LRU, 1)).astype(jnp.bfloat16)

    mix = jnp.concatenate([mix_pool, mix_lru], axis=-1)
    h1_new = x + _dot(mix, w_out_ref[...])
    ffn(2)
    h1buf[slot] = h1_new
    n2buf[slot] = _rms(h1_new, vec(ROW_LN2)).astype(jnp.bfloat16)

    ffn(len(pending))
    ffn_out = ffn_parts[0]
    for part in ffn_parts[1:]:
        ffn_out = ffn_out + part
    o_ref[...] = _rms(h1buf[prev] + ffn_out, vec(ROW_LNF))

    pbuf[0:POOL_TAIL, :] = pbuf[T:T + POOL_TAIL, :]
    lbuf[0:CONV_TAIL, :] = lbuf[T:T + CONV_TAIL, :]


def _block_diag(blocks):
    n, r, c = blocks.shape
    eye = jnp.eye(n, dtype=blocks.dtype)
    return (eye[:, None, :, None] * blocks[:, :, None, :]).reshape(n * r, n * c)


def kernel(x, ln1_g, w_in, pool_w, pool_scale, conv_w, conv_b, w_a, b_a, w_i, b_i, lam, gn_pool_g,
           gn_lru_g, w_out, ln2_g, w_ffn_gate, w_ffn_up, w_ffn_down, lnf_g):
    B, S, D = x.shape
    assert D == D_MODEL and S % SEQ_TILE == 0 and ln1_g.shape[0] == 1
    bf16 = jnp.bfloat16
    f32 = jnp.float32

    pair = lambda u, v: jnp.concatenate([u.reshape(-1), v.reshape(-1)]).astype(f32)
    zeros = jnp.zeros((LRU_WIDTH,), f32)
    rows = {ROW_LN1: ln1_g.reshape(-1), ROW_LN2: ln2_g.reshape(-1), ROW_LNF: lnf_g.reshape(-1),
            ROW_POOL: pair(pool_scale, gn_pool_g), ROW_LRU: pair(conv_b, gn_lru_g),
            ROW_GATE_B: pair(b_a, b_i), ROW_LAM: pair(lam, zeros)}
    for k in range(CONV_WIDTH):
        rows[ROW_CONV_W + k] = pair(conv_w[0, k], zeros)
    vec = jnp.stack([rows.get(r, jnp.zeros((D,), f32)).astype(f32) for r in range(VEC_ROWS)])

    wa = w_a[0].astype(bf16)
    wi = w_i[0].astype(bf16)
    pw = pool_w[0].astype(bf16)
    small_w = jnp.stack([
        jnp.concatenate([_block_diag(wa[4 * q:4 * q + 4]), _block_diag(wi[4 * q:4 * q + 4]),
                         _block_diag(pw[2 * q:2 * q + 2])], axis=-1)
        for q in range(2)])

    vmem = pl.BlockSpec(memory_space=pltpu.VMEM)
    hbm = pl.BlockSpec(memory_space=pl.ANY)
    nq = LRU_WIDTH // LANES
    scan_rows = SUBLANES * SUB_PITCH
    cps = S // SEQ_TILE
    n_chunks = B * cps

    def in_index(i):
        c = jnp.minimum(i, n_chunks - 1)
        return (c // cps, c % cps, 0)

    def out_index(i):
        c = jnp.maximum(i - 1, 0)
        return (c // cps, c % cps, 0)

    out = pl.pallas_call(
        functools.partial(_layer_kernel, chunks_per_seq=cps),
        grid=(n_chunks + 1,),
        in_specs=[pl.BlockSpec((None, SEQ_TILE, D), in_index), vmem, vmem] + [hbm] * 5,
        out_specs=pl.BlockSpec((None, SEQ_TILE, D), out_index),
        out_shape=jax.ShapeDtypeStruct((B, S, D), jnp.float32),
        scratch_shapes=[
            pltpu.VMEM((POOL_TAIL + SEQ_TILE, POOL_WIDTH), jnp.float32),
            pltpu.VMEM((CONV_TAIL + SEQ_TILE, LRU_WIDTH), jnp.float32),
            pltpu.VMEM((nq, SEQ_TILE, LANES), jnp.float32),
            pltpu.VMEM((nq, SEQ_TILE, LANES), jnp.float32),
            pltpu.VMEM((nq, scan_rows, LANES), jnp.float32),
            pltpu.VMEM((1, LRU_WIDTH), jnp.float32),
            pltpu.VMEM((2, SEQ_TILE, D), jnp.float32),
            pltpu.VMEM((2, SEQ_TILE, D), jnp.bfloat16),
            pltpu.VMEM((D, IN_WIDTH), bf16),
            pltpu.VMEM((D, D), bf16),
            pltpu.VMEM((D, D_FF), bf16),
            pltpu.VMEM((D, D_FF), bf16),
            pltpu.VMEM((D_FF, D), bf16),
            pltpu.VMEM((WEIGHT_STAGE_SLOTS, WEIGHT_STAGE_ROWS, D_FF), jnp.float32),
            pltpu.SemaphoreType.DMA((WEIGHT_STAGE_SLOTS,)),
        ],
        compiler_params=pltpu.CompilerParams(
            dimension_semantics=("arbitrary",),
            vmem_limit_bytes=VMEM_LIMIT_BYTES),
        name="hymba_layer",
    )(x, vec, small_w, w_in[0], w_out[0], w_ffn_gate[0], w_ffn_up[0], w_ffn_down[0])
    return out
```

```python
import functools
import math

import jax
import jax.numpy as jnp
from jax import lax
from jax.experimental import pallas as pl
from jax.experimental.pallas import tpu as pltpu

D_MODEL = 1024
POOL_WINDOWS = (2, 4, 8, 16)
POOL_GROUP = 128
POOL_WIDTH = 512
LRU_WIDTH = 512
LRU_HEADS = 8
LRU_HEAD_DIM = 64
CONV_WIDTH = 4
LRU_C = 8.0
D_FF = 2816
IN_WIDTH = POOL_WIDTH + 2 * LRU_WIDTH
EPS = 1e-6

LANES = 128
SUBLANES = 8
SEQ_TILE = 512
POOL_TAIL = 16
CONV_TAIL = 8
SUB_LEN = SEQ_TILE // SUBLANES
SUB_PITCH = SUB_LEN + SUBLANES
FF_CHUNK = 256
N_FF_CHUNKS = D_FF // FF_CHUNK
FF_DOWN_GROUP = 3
HBM_WEIGHT_OPERANDS = (2, 14, 16, 17, 18)
WEIGHT_STAGE_ROWS = 256
WEIGHT_STAGE_SLOTS = 3
VMEM_LIMIT_BYTES = 58 * 1024 * 1024


def _ffn_sequence():
    ops = []
    for c in range(N_FF_CHUNKS):
        ops.append(("u", c))
        if c % FF_DOWN_GROUP == 0 and c > 0:
            ops.append(("d", c - FF_DOWN_GROUP))
    last = (N_FF_CHUNKS - 1) // FF_DOWN_GROUP * FF_DOWN_GROUP
    return ops + [("d", last)]


def _rms(x, g):
    ms = jnp.mean(x * x, axis=-1, keepdims=True)
    return x * lax.rsqrt(ms + EPS) * g


def _sigmoid(x):
    return 0.5 * (jnp.tanh(0.5 * x) + 1.0)


def _gelu_tanh(x):
    c = math.sqrt(2.0 / math.pi)
    return 0.5 * x * (1.0 + jnp.tanh(c * (x + 0.044715 * (x * x * x))))


def _dot(a, b):
    return jnp.dot(a, b, preferred_element_type=jnp.float32)


def _load_weight_bf16(src_hbm, dst, stage, sems):
    rows, cols = src_hbm.shape
    n = rows // WEIGHT_STAGE_ROWS

    def copy(r, slot):
        src = src_hbm.at[pl.ds(r * WEIGHT_STAGE_ROWS, WEIGHT_STAGE_ROWS), :]
        return pltpu.make_async_copy(src, stage.at[slot, :, pl.ds(0, cols)], sems.at[slot])

    ahead = WEIGHT_STAGE_SLOTS - 1
    for r in range(min(ahead, n)):
        copy(r, r % WEIGHT_STAGE_SLOTS).start()

    def body(r, carry):
        slot = lax.rem(r, WEIGHT_STAGE_SLOTS)

        @pl.when(r + ahead < n)
        def _():
            copy(r + ahead, lax.rem(r + ahead, WEIGHT_STAGE_SLOTS)).start()

        copy(r, slot).wait()
        row0 = pl.multiple_of(r * WEIGHT_STAGE_ROWS, WEIGHT_STAGE_ROWS)
        dst[pl.ds(row0, WEIGHT_STAGE_ROWS), :] = stage[slot, :, 0:cols].astype(jnp.bfloat16)
        return carry

    lax.fori_loop(0, n, body, 0)


def _build_small_weights(pool_w_ref, w_a_ref, w_i_ref, small_w_ref):
    half = LRU_WIDTH // 2
    heads = half // LRU_HEAD_DIM
    bf16 = jnp.bfloat16
    c_idx = lax.broadcasted_iota(jnp.int32, (LRU_HEAD_DIM, half), 0)
    j_idx = lax.broadcasted_iota(jnp.int32, (LRU_HEAD_DIM, half), 1)
    spread = (lax.rem(j_idx, LRU_HEAD_DIM) == c_idx).astype(bf16)
    col_head = j_idx // LRU_HEAD_DIM
    for q in range(2):
        for part, w_ref in enumerate((w_a_ref, w_i_ref)):
            for h in range(heads):
                rep = _dot(w_ref[0, heads * q + h].astype(bf16), spread)
                blk = jnp.where(col_head == h, rep, 0.0).astype(bf16)
                small_w_ref[q, h * LRU_HEAD_DIM:(h + 1) * LRU_HEAD_DIM, part * half:(part + 1) * half] = blk
        small_w_ref[q, :, 2 * half:3 * half] = jnp.zeros((half, half), bf16)
        for g in range(2):
            rows = slice(g * POOL_GROUP, (g + 1) * POOL_GROUP)
            cols = slice(2 * half + g * POOL_GROUP, 2 * half + (g + 1) * POOL_GROUP)
            small_w_ref[q, rows, cols] = pool_w_ref[0, 2 * q + g].astype(bf16)


def _layer_kernel(x_ref, ln1_ref, w_in_hbm, pool_w_ref, pool_scale_ref, conv_w_ref, conv_b_ref,
                  w_a_ref, b_a_ref, w_i_ref, b_i_ref, lam_ref, gn_pool_ref, gn_lru_ref, w_out_hbm,
                  ln2_ref, wg_hbm, wu_hbm, wd_hbm, lnf_ref, o_ref,
                  pbuf, lbuf, abuf, bbuf, hbuf, hcarry, h1buf, n2buf,
                  w_in_ref, w_out_ref, wg_ref, wu_ref, wd_ref, small_w_ref, stage, stage_sems,
                  *, chunks_per_seq):
    i = pl.program_id(0)
    s = lax.rem(i, chunks_per_seq)
    T = SEQ_TILE
    half = LRU_WIDTH // 2

    @pl.when(i == 0)
    def _():
        for src, dst in ((w_in_hbm, w_in_ref), (w_out_hbm, w_out_ref), (wg_hbm, wg_ref),
                         (wu_hbm, wu_ref), (wd_hbm, wd_ref)):
            _load_weight_bf16(src, dst, stage, stage_sems)
        _build_small_weights(pool_w_ref, w_a_ref, w_i_ref, small_w_ref)

    @pl.when(s == 0)
    def _():
        pbuf[0:POOL_TAIL, :] = jnp.zeros((POOL_TAIL, POOL_WIDTH), jnp.float32)
        lbuf[0:CONV_TAIL, :] = jnp.zeros((CONV_TAIL, LRU_WIDTH), jnp.float32)
        hcarry[...] = jnp.zeros_like(hcarry)

    @pl.when(i == 0)
    def _():
        h1buf[1] = jnp.zeros((T, D_MODEL), jnp.float32)
        n2buf[1] = jnp.zeros((T, D_MODEL), jnp.bfloat16)

    slot = lax.rem(i, 2)
    prev = 1 - slot

    acts = {}
    ffn_parts = []
    pending = _ffn_sequence()

    def ffn(n):
        for _ in range(n):
            op, c = pending.pop(0)
            if op == "u":
                cols = slice(c * FF_CHUNK, (c + 1) * FF_CHUNK)
                n2 = n2buf[prev]
                half_g = 0.5 * _dot(n2, wg_ref[:, cols])
                up = _dot(n2, wu_ref[:, cols])
                acts[c] = (half_g * (jnp.tanh(half_g) + 1.0) * up).astype(jnp.bfloat16)
            else:
                group = range(c, min(c + FF_DOWN_GROUP, N_FF_CHUNKS))
                act = jnp.concatenate([acts.pop(k) for k in group], axis=-1)
                ffn_parts.append(_dot(act, wd_ref[c * FF_CHUNK:(group[-1] + 1) * FF_CHUNK, :]))

    ffn(1)
    x = x_ref[...]
    nb = _rms(x, ln1_ref[...]).astype(jnp.bfloat16)

    pbuf[POOL_TAIL:POOL_TAIL + T, :] = _dot(nb, w_in_ref[:, 0:POOL_WIDTH])
    lbuf[CONV_TAIL:CONV_TAIL + T, :] = _dot(nb, w_in_ref[:, POOL_WIDTH:POOL_WIDTH + LRU_WIDTH])

    ffn(1)
    n_groups = len(POOL_WINDOWS)
    level = pbuf[...]
    sums = []
    for g, w in enumerate(POOL_WINDOWS):
        assert w == 2 << g and w - 1 <= POOL_TAIL
        rest = level[:, (0 if g == 0 else POOL_GROUP):]
        rest = rest + pltpu.roll(rest, w // 2, axis=0)
        sums.append(rest[POOL_TAIL:, 0:POOL_GROUP])
        level = rest
    head_pos = (s * T + 1 + lax.broadcasted_iota(jnp.int32, (POOL_TAIL, POOL_GROUP), 0)).astype(jnp.float32)
    d_parts = []
    for g, w in enumerate(POOL_WINDOWS):
        ug = pbuf[POOL_TAIL:POOL_TAIL + T, g * POOL_GROUP:(g + 1) * POOL_GROUP]
        head = sums[g][0:POOL_TAIL] / jnp.minimum(head_pos, float(w))
        mean = jnp.concatenate([head, sums[g][POOL_TAIL:] * (1.0 / w)], axis=0)
        d_parts.append((mean - ug).astype(jnp.bfloat16))

    u_gate = _dot(nb, w_in_ref[:, POOL_WIDTH + LRU_WIDTH:])
    y_parts = []
    for p in range(n_groups // 2):
        d_pair = jnp.concatenate(d_parts[2 * p:2 * p + 2], axis=-1)
        y_parts.append(_dot(d_pair, small_w_ref[p, :, 2 * half:3 * half]))

    ffn(1)
    xc = conv_b_ref[...] + conv_w_ref[0, CONV_WIDTH - 1:CONV_WIDTH, :] * lbuf[CONV_TAIL:CONV_TAIL + T, :]
    for k in range(CONV_WIDTH - 1):
        off = CONV_TAIL - (CONV_WIDTH - 1) + k
        xc = xc + conv_w_ref[0, k:k + 1, :] * lbuf[off:off + T, :]
    xcb = xc.astype(jnp.bfloat16)
    z = [_dot(xcb[:, q * half:(q + 1) * half], small_w_ref[q, :, 0:2 * half])
         for q in range(2)]

    ffn(2)
    y_pool = jnp.concatenate(y_parts, axis=-1) * pool_scale_ref[...]
    mix_pool = _rms(y_pool, gn_pool_ref[...]).astype(jnp.bfloat16)

    za = jnp.concatenate([z[0][:, :half], z[1][:, :half]], axis=-1) + b_a_ref[...]
    zi = jnp.concatenate([z[0][:, half:], z[1][:, half:]], axis=-1) + b_i_ref[...]
    r = _sigmoid(za)
    gate_i = _sigmoid(zi)
    neg_lam = -lam_ref[...]
    softplus = jnp.maximum(neg_lam, 0.0) + jnp.log1p(jnp.exp(-jnp.abs(neg_lam)))
    neg_log_a = r * (LRU_C * softplus)
    a = jnp.exp(-neg_log_a)
    one_minus_a2 = jnp.tanh(neg_log_a) * (a * a + 1.0)
    bterm = jnp.sqrt(jnp.maximum(one_minus_a2, 0.0)) * (gate_i * xc)

    nq = LRU_WIDTH // LANES
    tiles_per_sub = SUB_LEN // SUBLANES
    for q in range(nq):
        lanes = slice(q * LANES, (q + 1) * LANES)
        for t in range(T // SUBLANES):
            j, m0 = divmod(t, tiles_per_sub)
            dst = pl.ds(m0 * SUBLANES * SUBLANES + j, SUBLANES, stride=SUBLANES)
            rows = slice(t * SUBLANES, (t + 1) * SUBLANES)
            abuf[q, dst, :] = a[rows, lanes]
            bbuf[q, dst, :] = bterm[rows, lanes]

    def tile(m):
        return slice(m * SUBLANES, (m + 1) * SUBLANES)

    ffn(2)
    prod = [jnp.ones((SUBLANES, LANES), jnp.float32) for _ in range(nq)]
    loc = [jnp.zeros((SUBLANES, LANES), jnp.float32) for _ in range(nq)]
    for m in range(SUB_LEN):
        for q in range(nq):
            am = abuf[q, tile(m), :]
            loc[q] = am * loc[q] + bbuf[q, tile(m), :]
            prod[q] = am * prod[q]
    row = lax.broadcasted_iota(jnp.int32, (SUBLANES, LANES), 0)
    h_in = []
    for q in range(nq):
        carry = hcarry[:, q * LANES:(q + 1) * LANES]
        start = jnp.zeros((SUBLANES, LANES), jnp.float32)
        for j in range(SUBLANES):
            start = jnp.where(row == j, carry, start)
            carry = prod[q][j:j + 1, :] * carry + loc[q][j:j + 1, :]
        h_in.append(start)
        hcarry[:, q * LANES:(q + 1) * LANES] = carry
    ffn(2)
    for m in range(SUB_LEN):
        for q in range(nq):
            h_in[q] = abuf[q, tile(m), :] * h_in[q] + bbuf[q, tile(m), :]
            hbuf[q, pl.ds(m, SUBLANES, stride=SUB_PITCH), :] = h_in[q]
    h_lru = jnp.concatenate(
        [jnp.concatenate([hbuf[q, j * SUB_PITCH:j * SUB_PITCH + SUB_LEN, :] for j in range(SUBLANES)], axis=0)
         for q in range(nq)], axis=-1)

    ffn(2)
    y_lru = h_lru * _gelu_tanh(u_gate)
    mix_lru = _rms(y_lru, gn_lru_ref[...]).astype(jnp.bfloat16)

    mix = jnp.concatenate([mix_pool, mix_lru], axis=-1)
    h1_new = x + _dot(mix, w_out_ref[...])
    ffn(2)
    h1buf[slot] = h1_new
    n2buf[slot] = _rms(h1_new, ln2_ref[...]).astype(jnp.bfloat16)

    ffn(len(pending))
    ffn_out = ffn_parts[0]
    for part in ffn_parts[1:]:
        ffn_out = ffn_out + part
    o_ref[...] = _rms(h1buf[prev] + ffn_out, lnf_ref[...])

    pbuf[0:POOL_TAIL, :] = pbuf[T:T + POOL_TAIL, :]
    lbuf[0:CONV_TAIL, :] = lbuf[T:T + CONV_TAIL, :]


def kernel(x, ln1_g, w_in, pool_w, pool_scale, conv_w, conv_b, w_a, b_a, w_i, b_i, lam, gn_pool_g,
           gn_lru_g, w_out, ln2_g, w_ffn_gate, w_ffn_up, w_ffn_down, lnf_g):
    B, S, D = x.shape
    assert D == D_MODEL and S % SEQ_TILE == 0 and ln1_g.shape[0] == 1
    bf16 = jnp.bfloat16
    vmem = pl.BlockSpec(memory_space=pltpu.VMEM)
    hbm = pl.BlockSpec(memory_space=pl.ANY)
    nq = LRU_WIDTH // LANES
    scan_rows = SUBLANES * SUB_PITCH
    cps = S // SEQ_TILE
    n_chunks = B * cps

    def in_index(i):
        c = jnp.minimum(i, n_chunks - 1)
        return (c // cps, c % cps, 0)

    def out_index(i):
        c = jnp.maximum(i - 1, 0)
        return (c // cps, c % cps, 0)

    out = pl.pallas_call(
        functools.partial(_layer_kernel, chunks_per_seq=cps),
        grid=(n_chunks + 1,),
        in_specs=[pl.BlockSpec((None, SEQ_TILE, D), in_index)] + [
            hbm if k in HBM_WEIGHT_OPERANDS else vmem for k in range(1, 20)],
        out_specs=pl.BlockSpec((None, SEQ_TILE, D), out_index),
        out_shape=jax.ShapeDtypeStruct((B, S, D), jnp.float32),
        scratch_shapes=[
            pltpu.VMEM((POOL_TAIL + SEQ_TILE, POOL_WIDTH), jnp.float32),
            pltpu.VMEM((CONV_TAIL + SEQ_TILE, LRU_WIDTH), jnp.float32),
            pltpu.VMEM((nq, SEQ_TILE, LANES), jnp.float32),
            pltpu.VMEM((nq, SEQ_TILE, LANES), jnp.float32),
            pltpu.VMEM((nq, scan_rows, LANES), jnp.float32),
            pltpu.VMEM((1, LRU_WIDTH), jnp.float32),
            pltpu.VMEM((2, SEQ_TILE, D), jnp.float32),
            pltpu.VMEM((2, SEQ_TILE, D), jnp.bfloat16),
            pltpu.VMEM((D, IN_WIDTH), bf16),
            pltpu.VMEM((D, D), bf16),
            pltpu.VMEM((D, D_FF), bf16),
            pltpu.VMEM((D, D_FF), bf16),
            pltpu.VMEM((D_FF, D), bf16),
            pltpu.VMEM((2, LRU_WIDTH // 2, LRU_WIDTH + LRU_WIDTH // 2), bf16),
            pltpu.VMEM((WEIGHT_STAGE_SLOTS, WEIGHT_STAGE_ROWS, D_FF), jnp.float32),
            pltpu.SemaphoreType.DMA((WEIGHT_STAGE_SLOTS,)),
        ],
        compiler_params=pltpu.CompilerParams(
            dimension_semantics=("arbitrary",),
            vmem_limit_bytes=VMEM_LIMIT_BYTES),
        name="hymba_layer",
    )(x, ln1_g, w_in[0], pool_w, pool_scale, conv_w, conv_b, w_a, b_a, w_i, b_i, lam, gn_pool_g, gn_lru_g,
      w_out[0], ln2_g, w_ffn_gate[0], w_ffn_up[0], w_ffn_down[0], lnf_g.reshape(1, D))
    return out
```

```python
import functools
import math

import jax
import jax.numpy as jnp
from jax import lax
from jax.experimental import pallas as pl
from jax.experimental.pallas import tpu as pltpu

D_MODEL = 1024
POOL_WINDOWS = (2, 4, 8, 16)
POOL_GROUP = 128
POOL_WIDTH = 512
LRU_WIDTH = 512
LRU_HEADS = 8
LRU_HEAD_DIM = 64
CONV_WIDTH = 4
LRU_C = 8.0
D_FF = 2816
IN_WIDTH = POOL_WIDTH + 2 * LRU_WIDTH
EPS = 1e-6

LANES = 128
SUBLANES = 8
SEQ_TILE = 512
POOL_TAIL = 16
CONV_TAIL = 8
SUB_LEN = SEQ_TILE // SUBLANES
SUB_PITCH = SUB_LEN + SUBLANES
FF_CHUNK = 256
N_FF_CHUNKS = D_FF // FF_CHUNK
FF_DOWN_GROUP = 6
HBM_WEIGHT_OPERANDS = (2, 14, 16, 17, 18)
WEIGHT_STAGE_ROWS = 256
WEIGHT_STAGE_SLOTS = 3
VMEM_LIMIT_BYTES = 58 * 1024 * 1024


def _ffn_sequence():
    ops = []
    for c in range(N_FF_CHUNKS):
        ops.append(("u", c))
        if c % FF_DOWN_GROUP == 0 and c > 0:
            ops.append(("d", c - FF_DOWN_GROUP))
    last = (N_FF_CHUNKS - 1) // FF_DOWN_GROUP * FF_DOWN_GROUP
    return ops + [("d", last)]


def _rms(x, g):
    ms = jnp.mean(x * x, axis=-1, keepdims=True)
    return x * lax.rsqrt(ms + EPS) * g


def _sigmoid(x):
    return 0.5 * (jnp.tanh(0.5 * x) + 1.0)


def _gelu_tanh_times(x, h):
    c = math.sqrt(2.0 / math.pi)
    t = jnp.tanh(x * ((x * x) * (0.044715 * c) + c))
    return (h * x) * (0.5 * t + 0.5)


def _dot(a, b):
    return jnp.dot(a, b, preferred_element_type=jnp.float32)


def _load_weight_bf16(src_hbm, dst, stage, sems, scale=None):
    rows, cols = src_hbm.shape
    n = rows // WEIGHT_STAGE_ROWS

    def copy(r, slot):
        src = src_hbm.at[pl.ds(r * WEIGHT_STAGE_ROWS, WEIGHT_STAGE_ROWS), :]
        return pltpu.make_async_copy(src, stage.at[slot, :, pl.ds(0, cols)], sems.at[slot])

    ahead = WEIGHT_STAGE_SLOTS - 1
    for r in range(min(ahead, n)):
        copy(r, r % WEIGHT_STAGE_SLOTS).start()

    def body(r, carry):
        slot = lax.rem(r, WEIGHT_STAGE_SLOTS)

        @pl.when(r + ahead < n)
        def _():
            copy(r + ahead, lax.rem(r + ahead, WEIGHT_STAGE_SLOTS)).start()

        copy(r, slot).wait()
        row0 = pl.multiple_of(r * WEIGHT_STAGE_ROWS, WEIGHT_STAGE_ROWS)
        w = stage[slot, :, 0:cols]
        dst[pl.ds(row0, WEIGHT_STAGE_ROWS), :] = (w if scale is None else w * scale).astype(jnp.bfloat16)
        return carry

    lax.fori_loop(0, n, body, 0)


def _build_small_weights(pool_w_ref, w_a_ref, w_i_ref, small_w_ref):
    half = LRU_WIDTH // 2
    heads = half // LRU_HEAD_DIM
    bf16 = jnp.bfloat16
    c_idx = lax.broadcasted_iota(jnp.int32, (LRU_HEAD_DIM, half), 0)
    j_idx = lax.broadcasted_iota(jnp.int32, (LRU_HEAD_DIM, half), 1)
    spread = (lax.rem(j_idx, LRU_HEAD_DIM) == c_idx).astype(bf16)
    col_head = j_idx // LRU_HEAD_DIM
    for q in range(2):
        for part, w_ref in enumerate((w_a_ref, w_i_ref)):
            for h in range(heads):
                rep = _dot(w_ref[0, heads * q + h].astype(bf16), spread)
                blk = jnp.where(col_head == h, rep, 0.0).astype(bf16)
                small_w_ref[q, h * LRU_HEAD_DIM:(h + 1) * LRU_HEAD_DIM, part * half:(part + 1) * half] = blk
        small_w_ref[q, :, 2 * half:3 * half] = jnp.zeros((half, half), bf16)
        for g in range(2):
            rows = slice(g * POOL_GROUP, (g + 1) * POOL_GROUP)
            cols = slice(2 * half + g * POOL_GROUP, 2 * half + (g + 1) * POOL_GROUP)
            small_w_ref[q, rows, cols] = pool_w_ref[0, 2 * q + g].astype(bf16)


def _layer_kernel(x_ref, ln1_ref, w_in_hbm, pool_w_ref, pool_scale_ref, conv_w_ref, conv_b_ref,
                  w_a_ref, b_a_ref, w_i_ref, b_i_ref, lam_ref, gn_pool_ref, gn_lru_ref, w_out_hbm,
                  ln2_ref, wg_hbm, wu_hbm, wd_hbm, lnf_ref, o_ref,
                  pbuf, lbuf, abuf, bbuf, hbuf, hcarry, h1buf, n2buf,
                  w_in_ref, w_out_ref, wg_ref, wu_ref, wd_ref, small_w_ref, stage, stage_sems,
                  *, chunks_per_seq):
    i = pl.program_id(0)
    s = lax.rem(i, chunks_per_seq)
    T = SEQ_TILE
    half = LRU_WIDTH // 2

    @pl.when(i == 0)
    def _():
        for src, dst, scale in ((w_in_hbm, w_in_ref, None), (w_out_hbm, w_out_ref, None),
                                (wg_hbm, wg_ref, 0.5), (wu_hbm, wu_ref, None), (wd_hbm, wd_ref, None)):
            _load_weight_bf16(src, dst, stage, stage_sems, scale)
        _build_small_weights(pool_w_ref, w_a_ref, w_i_ref, small_w_ref)

    @pl.when(s == 0)
    def _():
        pbuf[0:POOL_TAIL, :] = jnp.zeros((POOL_TAIL, POOL_WIDTH), jnp.float32)
        lbuf[0:CONV_TAIL, :] = jnp.zeros((CONV_TAIL, LRU_WIDTH), jnp.float32)
        hcarry[...] = jnp.zeros_like(hcarry)

    @pl.when(i == 0)
    def _():
        h1buf[1] = jnp.zeros((T, D_MODEL), jnp.float32)
        n2buf[1] = jnp.zeros((T, D_MODEL), jnp.bfloat16)

    slot = lax.rem(i, 2)
    prev = 1 - slot

    acts = {}
    ffn_parts = []
    pending = _ffn_sequence()

    def ffn(n):
        for _ in range(n):
            op, c = pending.pop(0)
            if op == "u":
                cols = slice(c * FF_CHUNK, (c + 1) * FF_CHUNK)
                n2 = n2buf[prev]
                half_g = _dot(n2, wg_ref[:, cols])
                up = _dot(n2, wu_ref[:, cols])
                acts[c] = (half_g * (jnp.tanh(half_g) + 1.0) * up).astype(jnp.bfloat16)
            else:
                group = range(c, min(c + FF_DOWN_GROUP, N_FF_CHUNKS))
                act = jnp.concatenate([acts.pop(k) for k in group], axis=-1)
                ffn_parts.append(_dot(act, wd_ref[c * FF_CHUNK:(group[-1] + 1) * FF_CHUNK, :]))

    ffn(1)
    x = x_ref[...]
    nb = _rms(x, ln1_ref[...]).astype(jnp.bfloat16)

    pbuf[POOL_TAIL:POOL_TAIL + T, :] = _dot(nb, w_in_ref[:, 0:POOL_WIDTH])
    lbuf[CONV_TAIL:CONV_TAIL + T, :] = _dot(nb, w_in_ref[:, POOL_WIDTH:POOL_WIDTH + LRU_WIDTH])

    ffn(1)
    n_groups = len(POOL_WINDOWS)
    level = pbuf[...]
    sums = []
    for g, w in enumerate(POOL_WINDOWS):
        assert w == 2 << g and w - 1 <= POOL_TAIL
        rest = level[:, (0 if g == 0 else POOL_GROUP):]
        rest = rest + pltpu.roll(rest, w // 2, axis=0)
        sums.append(rest[POOL_TAIL:, 0:POOL_GROUP])
        level = rest
    head_pos = (s * T + 1 + lax.broadcasted_iota(jnp.int32, (POOL_TAIL, POOL_GROUP), 0)).astype(jnp.float32)
    d_parts = []
    for g, w in enumerate(POOL_WINDOWS):
        ug = pbuf[POOL_TAIL:POOL_TAIL + T, g * POOL_GROUP:(g + 1) * POOL_GROUP]
        head = sums[g][0:POOL_TAIL] / jnp.minimum(head_pos, float(w))
        mean = jnp.concatenate([head, sums[g][POOL_TAIL:] * (1.0 / w)], axis=0)
        d_parts.append((mean - ug).astype(jnp.bfloat16))

    u_gate = _dot(nb, w_in_ref[:, POOL_WIDTH + LRU_WIDTH:])
    y_parts = []
    for p in range(n_groups // 2):
        d_pair = jnp.concatenate(d_parts[2 * p:2 * p + 2], axis=-1)
        y_parts.append(_dot(d_pair, small_w_ref[p, :, 2 * half:3 * half]))

    ffn(1)
    assert CONV_WIDTH == 4
    u_ext = lbuf[...]
    u_prev = pltpu.roll(u_ext, 1, axis=0)
    tap = lambda k: conv_w_ref[0, k:k + 1, :]
    older = tap(1) * u_ext + tap(0) * u_prev
    xc = (conv_b_ref[...] + tap(3) * u_ext + tap(2) * u_prev + pltpu.roll(older, 2, axis=0))[CONV_TAIL:]
    xcb = xc.astype(jnp.bfloat16)
    z = [_dot(xcb[:, q * half:(q + 1) * half], small_w_ref[q, :, 0:2 * half])
         for q in range(2)]

    ffn(2)
    y_pool = jnp.concatenate(y_parts, axis=-1) * pool_scale_ref[...]
    mix_pool = _rms(y_pool, gn_pool_ref[...]).astype(jnp.bfloat16)

    za = jnp.concatenate([z[0][:, :half], z[1][:, :half]], axis=-1) + b_a_ref[...]
    zi = jnp.concatenate([z[0][:, half:], z[1][:, half:]], axis=-1) + b_i_ref[...]
    gate_i = _sigmoid(zi)
    neg_lam = -lam_ref[...]
    softplus = jnp.maximum(neg_lam, 0.0) + jnp.log1p(jnp.exp(-jnp.abs(neg_lam)))
    k_row = (0.5 * LRU_C) * softplus
    neg_log_a = jnp.tanh(0.5 * za) * k_row + k_row
    a = jnp.exp(-neg_log_a)
    one_minus_a2 = jnp.tanh(neg_log_a) * (a * a + 1.0)
    mult = jnp.where(one_minus_a2 > 0.0, one_minus_a2 * lax.rsqrt(one_minus_a2), 0.0)
    bterm = mult * (gate_i * xc)

    nq = LRU_WIDTH // LANES
    tiles_per_sub = SUB_LEN // SUBLANES
    for q in range(nq):
        lanes = slice(q * LANES, (q + 1) * LANES)
        for t in range(T // SUBLANES):
            j, m0 = divmod(t, tiles_per_sub)
            dst = pl.ds(m0 * SUBLANES * SUBLANES + j, SUBLANES, stride=SUBLANES)
            rows = slice(t * SUBLANES, (t + 1) * SUBLANES)
            abuf[q, dst, :] = a[rows, lanes]
            bbuf[q, dst, :] = bterm[rows, lanes]

    def tile(m):
        return slice(m * SUBLANES, (m + 1) * SUBLANES)

    ffn(2)
    prod = [jnp.ones((SUBLANES, LANES), jnp.float32) for _ in range(nq)]
    loc = [jnp.zeros((SUBLANES, LANES), jnp.float32) for _ in range(nq)]
    for m in range(SUB_LEN):
        for q in range(nq):
            am = abuf[q, tile(m), :]
            loc[q] = am * loc[q] + bbuf[q, tile(m), :]
            prod[q] = am * prod[q]
    row = lax.broadcasted_iota(jnp.int32, (SUBLANES, LANES), 0)
    h_in = []
    for q in range(nq):
        carry = hcarry[:, q * LANES:(q + 1) * LANES]
        start = jnp.zeros((SUBLANES, LANES), jnp.float32)
        for j in range(SUBLANES):
            start = jnp.where(row == j, carry, start)
            carry = prod[q][j:j + 1, :] * carry + loc[q][j:j + 1, :]
        h_in.append(start)
        hcarry[:, q * LANES:(q + 1) * LANES] = carry
    ffn(2)
    for m in range(SUB_LEN):
        for q in range(nq):
            h_in[q] = abuf[q, tile(m), :] * h_in[q] + bbuf[q, tile(m), :]
            hbuf[q, pl.ds(m, SUBLANES, stride=SUB_PITCH), :] = h_in[q]
    h_lru = jnp.concatenate(
        [jnp.concatenate([hbuf[q, j * SUB_PITCH:j * SUB_PITCH + SUB_LEN, :] for j in range(SUBLANES)], axis=0)
         for q in range(nq)], axis=-1)

    ffn(2)
    y_lru = _gelu_tanh_times(u_gate, h_lru)
    mix_lru = _rms(y_lru, gn_lru_ref[...]).astype(jnp.bfloat16)

    mix = jnp.concatenate([mix_pool, mix_lru], axis=-1)
    h1_new = x + _dot(mix, w_out_ref[...])
    ffn(2)
    h1buf[slot] = h1_new
    n2buf[slot] = _rms(h1_new, ln2_ref[...]).astype(jnp.bfloat16)

    ffn(len(pending))
    ffn_out = ffn_parts[0]
    for part in ffn_parts[1:]:
        ffn_out = ffn_out + part
    o_ref[...] = _rms(h1buf[prev] + ffn_out, lnf_ref[...])

    pbuf[0:POOL_TAIL, :] = pbuf[T:T + POOL_TAIL, :]
    lbuf[0:CONV_TAIL, :] = lbuf[T:T + CONV_TAIL, :]


def kernel(x, ln1_g, w_in, pool_w, pool_scale, conv_w, conv_b, w_a, b_a, w_i, b_i, lam, gn_pool_g,
           gn_lru_g, w_out, ln2_g, w_ffn_gate, w_ffn_up, w_ffn_down, lnf_g):
    B, S, D = x.shape
    assert D == D_MODEL and S % SEQ_TILE == 0 and ln1_g.shape[0] == 1
    bf16 = jnp.bfloat16
    vmem = pl.BlockSpec(memory_space=pltpu.VMEM)
    hbm = pl.BlockSpec(memory_space=pl.ANY)
    nq = LRU_WIDTH // LANES
    scan_rows = SUBLANES * SUB_PITCH
    cps = S // SEQ_TILE
    n_chunks = B * cps

    def in_index(i):
        c = jnp.minimum(i, n_chunks - 1)
        return (c // cps, c % cps, 0)

    def out_index(i):
        c = jnp.maximum(i - 1, 0)
        return (c // cps, c % cps, 0)

    out = pl.pallas_call(
        functools.partial(_layer_kernel, chunks_per_seq=cps),
        grid=(n_chunks + 1,),
        in_specs=[pl.BlockSpec((None, SEQ_TILE, D), in_index)] + [
            hbm if k in HBM_WEIGHT_OPERANDS else vmem for k in range(1, 20)],
        out_specs=pl.BlockSpec((None, SEQ_TILE, D), out_index),
        out_shape=jax.ShapeDtypeStruct((B, S, D), jnp.float32),
        scratch_shapes=[
            pltpu.VMEM((POOL_TAIL + SEQ_TILE, POOL_WIDTH), jnp.float32),
            pltpu.VMEM((CONV_TAIL + SEQ_TILE, LRU_WIDTH), jnp.float32),
            pltpu.VMEM((nq, SEQ_TILE, LANES), jnp.float32),
            pltpu.VMEM((nq, SEQ_TILE, LANES), jnp.float32),
            pltpu.VMEM((nq, scan_rows, LANES), jnp.float32),
            pltpu.VMEM((1, LRU_WIDTH), jnp.float32),
            pltpu.VMEM((2, SEQ_TILE, D), jnp.float32),
            pltpu.VMEM((2, SEQ_TILE, D), jnp.bfloat16),
            pltpu.VMEM((D, IN_WIDTH), bf16),
            pltpu.VMEM((D, D), bf16),
            pltpu.VMEM((D, D_FF), bf16),
            pltpu.VMEM((D, D_FF), bf16),
            pltpu.VMEM((D_FF, D), bf16),
            pltpu.VMEM((2, LRU_WIDTH // 2, LRU_WIDTH + LRU_WIDTH // 2), bf16),
            pltpu.VMEM((WEIGHT_STAGE_SLOTS, WEIGHT_STAGE_ROWS, D_FF), jnp.float32),
            pltpu.SemaphoreType.DMA((WEIGHT_STAGE_SLOTS,)),
        ],
        compiler_params=pltpu.CompilerParams(
            dimension_semantics=("arbitrary",),
            vmem_limit_bytes=VMEM_LIMIT_BYTES),
        name="hymba_layer",
    )(x, ln1_g, w_in[0], pool_w, pool_scale, conv_w, conv_b, w_a, b_a, w_i, b_i, lam, gn_pool_g, gn_lru_g,
      w_out[0], ln2_g, w_ffn_gate[0], w_ffn_up[0], w_ffn_down[0], lnf_g.reshape(1, D))
    return out
```

```python
import functools
import math

import jax
import jax.numpy as jnp
from jax import lax
from jax.experimental import pallas as pl
from jax.experimental.pallas import tpu as pltpu

D_MODEL = 1024
POOL_WINDOWS = (2, 4, 8, 16)
POOL_GROUP = 128
POOL_WIDTH = 512
LRU_WIDTH = 512
LRU_HEADS = 8
LRU_HEAD_DIM = 64
CONV_WIDTH = 4
LRU_C = 8.0
D_FF = 2816
IN_WIDTH = POOL_WIDTH + 2 * LRU_WIDTH
EPS = 1e-6

LANES = 128
SUBLANES = 8
SEQ_TILE = 512
SUB_LEN = SEQ_TILE // SUBLANES
POOL_HEAD = 16
CONV_HEAD = CONV_WIDTH - 1
FF_CHUNK = 256
N_FF_CHUNKS = D_FF // FF_CHUNK
FF_DOWN_GROUP = 6
HBM_WEIGHT_OPERANDS = (2, 14, 16, 17, 18)
WEIGHT_STAGE_ROWS = 256
WEIGHT_STAGE_SLOTS = 3
VMEM_LIMIT_BYTES = 58 * 1024 * 1024


def _ffn_sequence():
    ops = []
    for c in range(N_FF_CHUNKS):
        ops.append(("u", c))
        if c % FF_DOWN_GROUP == 0 and c > 0:
            ops.append(("d", c - FF_DOWN_GROUP))
    last = (N_FF_CHUNKS - 1) // FF_DOWN_GROUP * FF_DOWN_GROUP
    return ops + [("d", last)]


def _rms(x, g):
    ms = jnp.mean(x * x, axis=-1, keepdims=True)
    return x * lax.rsqrt(ms + EPS) * g


def _sigmoid(x):
    return 0.5 * (jnp.tanh(0.5 * x) + 1.0)


def _gelu_tanh_times(x, h):
    c = math.sqrt(2.0 / math.pi)
    t = jnp.tanh(x * ((x * x) * (0.044715 * c) + c))
    return (h * x) * (0.5 * t + 0.5)


def _dot(a, b):
    return jnp.dot(a, b, preferred_element_type=jnp.float32)


def _history_tiles(cur_tail, prev_tail):
    rows, cols = cur_tail.shape
    shape3 = (rows // SUBLANES, SUBLANES, cols)
    sub = lax.broadcasted_iota(jnp.int32, shape3, 1)
    cur = pltpu.roll(cur_tail.reshape(shape3), 1, axis=1)
    prv = pltpu.roll(prev_tail.reshape(shape3), 1, axis=1)
    return jnp.where(sub == 0, prv, cur).reshape(rows, cols)


def _load_weight_bf16(src_hbm, dst, stage, sems, scale=None):
    rows, cols = src_hbm.shape
    n = rows // WEIGHT_STAGE_ROWS

    def copy(r, slot):
        src = src_hbm.at[pl.ds(r * WEIGHT_STAGE_ROWS, WEIGHT_STAGE_ROWS), :]
        return pltpu.make_async_copy(src, stage.at[slot, :, pl.ds(0, cols)], sems.at[slot])

    ahead = WEIGHT_STAGE_SLOTS - 1
    for r in range(min(ahead, n)):
        copy(r, r % WEIGHT_STAGE_SLOTS).start()

    def body(r, carry):
        slot = lax.rem(r, WEIGHT_STAGE_SLOTS)

        @pl.when(r + ahead < n)
        def _():
            copy(r + ahead, lax.rem(r + ahead, WEIGHT_STAGE_SLOTS)).start()

        copy(r, slot).wait()
        row0 = pl.multiple_of(r * WEIGHT_STAGE_ROWS, WEIGHT_STAGE_ROWS)
        w = stage[slot, :, 0:cols]
        dst[pl.ds(row0, WEIGHT_STAGE_ROWS), :] = (w if scale is None else w * scale).astype(jnp.bfloat16)
        return carry

    lax.fori_loop(0, n, body, 0)


def _build_small_weights(pool_w_ref, w_a_ref, w_i_ref, small_w_ref):
    half = LRU_WIDTH // 2
    heads = half // LRU_HEAD_DIM
    bf16 = jnp.bfloat16
    c_idx = lax.broadcasted_iota(jnp.int32, (LRU_HEAD_DIM, half), 0)
    j_idx = lax.broadcasted_iota(jnp.int32, (LRU_HEAD_DIM, half), 1)
    spread = (lax.rem(j_idx, LRU_HEAD_DIM) == c_idx).astype(bf16)
    col_head = j_idx // LRU_HEAD_DIM
    for q in range(2):
        for part, w_ref in enumerate((w_a_ref, w_i_ref)):
            for h in range(heads):
                rep = _dot(w_ref[0, heads * q + h].astype(bf16), spread)
                blk = jnp.where(col_head == h, rep, 0.0).astype(bf16)
                small_w_ref[q, h * LRU_HEAD_DIM:(h + 1) * LRU_HEAD_DIM, part * half:(part + 1) * half] = blk
        small_w_ref[q, :, 2 * half:3 * half] = jnp.zeros((half, half), bf16)
        for g in range(2):
            rows = slice(g * POOL_GROUP, (g + 1) * POOL_GROUP)
            cols = slice(2 * half + g * POOL_GROUP, 2 * half + (g + 1) * POOL_GROUP)
            small_w_ref[q, rows, cols] = pool_w_ref[0, 2 * q + g].astype(bf16)


def _layer_kernel(x_hbm, ln1_ref, w_in_hbm, pool_w_ref, pool_scale_ref, conv_w_ref, conv_b_ref,
                  w_a_ref, b_a_ref, w_i_ref, b_i_ref, lam_ref, gn_pool_ref, gn_lru_ref, w_out_hbm,
                  ln2_ref, wg_hbm, wu_hbm, wd_hbm, lnf_ref, o_hbm,
                  xbuf, obuf, x_sems, o_sems, ptail, ltail, hcarry, h1buf, n2buf,
                  w_in_ref, w_out_ref, wg_ref, wu_ref, wd_ref, small_w_ref, stage, stage_sems,
                  *, chunks_per_seq, n_chunks):
    i = pl.program_id(0)
    s = lax.rem(i, chunks_per_seq)
    T = SEQ_TILE
    half = LRU_WIDTH // 2
    slot = lax.rem(i, 2)
    prev = 1 - slot

    def hbm_rows(ref, c, j):
        b = lax.div(c, chunks_per_seq)
        t0 = lax.rem(c, chunks_per_seq) * T + j * SUB_LEN
        return ref.at[b, pl.ds(t0, SUB_LEN), :]

    def x_copy(c, j, buf):
        return pltpu.make_async_copy(hbm_rows(x_hbm, c, j), xbuf.at[buf, :, j, :], x_sems.at[buf, j])

    def o_copy(c, j, buf):
        return pltpu.make_async_copy(obuf.at[buf, :, j, :], hbm_rows(o_hbm, c, j), o_sems.at[buf, j])

    @pl.when(i == 0)
    def _():
        for j in range(SUBLANES):
            x_copy(i, j, 0).start()

    @pl.when(i + 1 < n_chunks)
    def _():
        for j in range(SUBLANES):
            x_copy(i + 1, j, prev).start()

    @pl.when(i == 0)
    def _():
        for src, dst, scale in ((w_in_hbm, w_in_ref, None), (w_out_hbm, w_out_ref, None),
                                (wg_hbm, wg_ref, 0.5), (wu_hbm, wu_ref, None), (wd_hbm, wd_ref, None)):
            _load_weight_bf16(src, dst, stage, stage_sems, scale)
        _build_small_weights(pool_w_ref, w_a_ref, w_i_ref, small_w_ref)
        h1buf[1] = jnp.zeros((T, D_MODEL), jnp.float32)
        n2buf[1] = jnp.zeros((T, D_MODEL), jnp.bfloat16)

    @pl.when(s == 0)
    def _():
        ptail[...] = jnp.zeros_like(ptail)
        ltail[...] = jnp.zeros_like(ltail)
        hcarry[...] = jnp.zeros_like(hcarry)

    @pl.when(i < n_chunks)
    def _():
        for j in range(SUBLANES):
            x_copy(i, j, slot).wait()

    acts = {}
    ffn_parts = []
    pending = _ffn_sequence()

    def ffn(n):
        for _ in range(n):
            op, c = pending.pop(0)
            if op == "u":
                cols = slice(c * FF_CHUNK, (c + 1) * FF_CHUNK)
                n2 = n2buf[prev]
                half_g = _dot(n2, wg_ref[:, cols])
                up = _dot(n2, wu_ref[:, cols])
                acts[c] = (half_g * (jnp.tanh(half_g) + 1.0) * up).astype(jnp.bfloat16)
            else:
                group = range(c, min(c + FF_DOWN_GROUP, N_FF_CHUNKS))
                act = jnp.concatenate([acts.pop(k) for k in group], axis=-1)
                ffn_parts.append(_dot(act, wd_ref[c * FF_CHUNK:(group[-1] + 1) * FF_CHUNK, :]))

    def x_rows():
        return xbuf[slot].reshape(T, D_MODEL)

    ffn(1)
    nb = _rms(x_rows(), ln1_ref[...]).astype(jnp.bfloat16)

    u_pool = _dot(nb, w_in_ref[:, 0:POOL_WIDTH])
    u_lru = _dot(nb, w_in_ref[:, POOL_WIDTH:POOL_WIDTH + LRU_WIDTH])

    ffn(1)
    n_groups = len(POOL_WINDOWS)
    head_rows = POOL_HEAD * SUBLANES
    pool_tail = u_pool[T - head_rows:, :]
    level = jnp.concatenate([_history_tiles(pool_tail, ptail[...]), u_pool], axis=0)
    ptail[...] = pool_tail
    hist = POOL_HEAD
    sums = []
    for g, w in enumerate(POOL_WINDOWS):
        assert w == 2 << g and w - 1 <= POOL_HEAD
        rest = level[:, (0 if g == 0 else POOL_GROUP):]
        k = (w // 2) * SUBLANES
        rest = rest[k:, :] + rest[:-k, :]
        hist -= w // 2
        sums.append(rest[hist * SUBLANES:, 0:POOL_GROUP])
        level = rest
    shape3 = (POOL_HEAD, SUBLANES, POOL_GROUP)
    head_pos = (s * T + 1 + lax.broadcasted_iota(jnp.int32, shape3, 0)
                + SUB_LEN * lax.broadcasted_iota(jnp.int32, shape3, 1))
    head_pos = head_pos.astype(jnp.float32).reshape(head_rows, POOL_GROUP)
    d_parts = []
    for g, w in enumerate(POOL_WINDOWS):
        ug = u_pool[:, g * POOL_GROUP:(g + 1) * POOL_GROUP]
        head = sums[g][0:head_rows] / jnp.minimum(head_pos, float(w))
        mean = jnp.concatenate([head, sums[g][head_rows:] * (1.0 / w)], axis=0)
        d_parts.append((mean - ug).astype(jnp.bfloat16))

    u_gate = _dot(nb, w_in_ref[:, POOL_WIDTH + LRU_WIDTH:])
    y_parts = []
    for p in range(n_groups // 2):
        d_pair = jnp.concatenate(d_parts[2 * p:2 * p + 2], axis=-1)
        y_parts.append(_dot(d_pair, small_w_ref[p, :, 2 * half:3 * half]))

    ffn(1)
    conv_rows = CONV_HEAD * SUBLANES
    lru_tail = u_lru[T - conv_rows:, :]
    u_ext = jnp.concatenate([_history_tiles(lru_tail, ltail[...]), u_lru], axis=0)
    ltail[...] = lru_tail
    xc = conv_b_ref[...] + conv_w_ref[0, CONV_WIDTH - 1:CONV_WIDTH, :] * u_lru
    for k in range(CONV_WIDTH - 1):
        xc = xc + conv_w_ref[0, k:k + 1, :] * u_ext[k * SUBLANES:k * SUBLANES + T, :]
    xcb = xc.astype(jnp.bfloat16)
    z = [_dot(xcb[:, q * half:(q + 1) * half], small_w_ref[q, :, 0:2 * half])
         for q in range(2)]

    ffn(2)
    y_pool = jnp.concatenate(y_parts, axis=-1) * pool_scale_ref[...]
    mix_pool = _rms(y_pool, gn_pool_ref[...]).astype(jnp.bfloat16)

    za = jnp.concatenate([z[0][:, :half], z[1][:, :half]], axis=-1) + b_a_ref[...]
    zi = jnp.concatenate([z[0][:, half:], z[1][:, half:]], axis=-1) + b_i_ref[...]
    gate_i = _sigmoid(zi)
    neg_lam = -lam_ref[...]
    softplus = jnp.maximum(neg_lam, 0.0) + jnp.log1p(jnp.exp(-jnp.abs(neg_lam)))
    k_row = (0.5 * LRU_C) * softplus
    neg_log_a = jnp.tanh(0.5 * za) * k_row + k_row
    a = jnp.exp(-neg_log_a)
    one_minus_a2 = jnp.tanh(neg_log_a) * (a * a + 1.0)
    mult = jnp.where(one_minus_a2 > 0.0, one_minus_a2 * lax.rsqrt(one_minus_a2), 0.0)
    bterm = mult * (gate_i * xc)

    nq = LRU_WIDTH // LANES

    def tile(v, m, q):
        return v[m * SUBLANES:(m + 1) * SUBLANES, q * LANES:(q + 1) * LANES]

    ffn(2)
    prod = [jnp.ones((SUBLANES, LANES), jnp.float32) for _ in range(nq)]
    loc = [jnp.zeros((SUBLANES, LANES), jnp.float32) for _ in range(nq)]
    for m in range(SUB_LEN):
        for q in range(nq):
            am = tile(a, m, q)
            loc[q] = am * loc[q] + tile(bterm, m, q)
            prod[q] = am * prod[q]
    row = lax.broadcasted_iota(jnp.int32, (SUBLANES, LANES), 0)
    h_in = []
    for q in range(nq):
        carry = hcarry[:, q * LANES:(q + 1) * LANES]
        start = jnp.zeros((SUBLANES, LANES), jnp.float32)
        for j in range(SUBLANES):
            start = jnp.where(row == j, carry, start)
            carry = prod[q][j:j + 1, :] * carry + loc[q][j:j + 1, :]
        h_in.append(start)
        hcarry[:, q * LANES:(q + 1) * LANES] = carry
    ffn(2)
    h_tiles = [[] for _ in range(nq)]
    for m in range(SUB_LEN):
        for q in range(nq):
            h_in[q] = tile(a, m, q) * h_in[q] + tile(bterm, m, q)
            h_tiles[q].append(h_in[q])
    h_lru = jnp.concatenate([jnp.concatenate(h_tiles[q], axis=0) for q in range(nq)], axis=-1)

    ffn(2)
    y_lru = _gelu_tanh_times(u_gate, h_lru)
    mix_lru = _rms(y_lru, gn_lru_ref[...]).astype(jnp.bfloat16)

    mix = jnp.concatenate([mix_pool, mix_lru], axis=-1)
    h1_new = x_rows() + _dot(mix, w_out_ref[...])
    ffn(2)
    h1buf[slot] = h1_new
    n2buf[slot] = _rms(h1_new, ln2_ref[...]).astype(jnp.bfloat16)

    ffn(len(pending))
    ffn_out = ffn_parts[0]
    for part in ffn_parts[1:]:
        ffn_out = ffn_out + part
    out = _rms(h1buf[prev] + ffn_out, lnf_ref[...])
    obuf[prev] = out.reshape(SUB_LEN, SUBLANES, D_MODEL)

    @pl.when(i >= 1)
    def _():
        for j in range(SUBLANES):
            o_copy(i - 1, j, prev).start()

    @pl.when(i >= 2)
    def _():
        for j in range(SUBLANES):
            o_copy(i - 2, j, slot).wait()

    @pl.when(i == n_chunks)
    def _():
        for j in range(SUBLANES):
            o_copy(i - 1, j, prev).wait()


def kernel(x, ln1_g, w_in, pool_w, pool_scale, conv_w, conv_b, w_a, b_a, w_i, b_i, lam, gn_pool_g,
           gn_lru_g, w_out, ln2_g, w_ffn_gate, w_ffn_up, w_ffn_down, lnf_g):
    B, S, D = x.shape
    assert D == D_MODEL and S % SEQ_TILE == 0 and ln1_g.shape[0] == 1
    bf16 = jnp.bfloat16
    f32 = jnp.float32
    vmem = pl.BlockSpec(memory_space=pltpu.VMEM)
    hbm = pl.BlockSpec(memory_space=pl.ANY)
    cps = S // SEQ_TILE
    n_chunks = B * cps

    out = pl.pallas_call(
        functools.partial(_layer_kernel, chunks_per_seq=cps, n_chunks=n_chunks),
        grid=(n_chunks + 1,),
        in_specs=[hbm] + [hbm if k in HBM_WEIGHT_OPERANDS else vmem for k in range(1, 20)],
        out_specs=hbm,
        out_shape=jax.ShapeDtypeStruct((B, S, D), f32),
        scratch_shapes=[
            pltpu.VMEM((2, SUB_LEN, SUBLANES, D), f32),
            pltpu.VMEM((2, SUB_LEN, SUBLANES, D), f32),
            pltpu.SemaphoreType.DMA((2, SUBLANES)),
            pltpu.SemaphoreType.DMA((2, SUBLANES)),
            pltpu.VMEM((POOL_HEAD * SUBLANES, POOL_WIDTH), f32),
            pltpu.VMEM((CONV_HEAD * SUBLANES, LRU_WIDTH), f32),
            pltpu.VMEM((1, LRU_WIDTH), f32),
            pltpu.VMEM((2, SEQ_TILE, D), f32),
            pltpu.VMEM((2, SEQ_TILE, D), bf16),
            pltpu.VMEM((D, IN_WIDTH), bf16),
            pltpu.VMEM((D, D), bf16),
            pltpu.VMEM((D, D_FF), bf16),
            pltpu.VMEM((D, D_FF), bf16),
            pltpu.VMEM((D_FF, D), bf16),
            pltpu.VMEM((2, LRU_WIDTH // 2, LRU_WIDTH + LRU_WIDTH // 2), bf16),
            pltpu.VMEM((WEIGHT_STAGE_SLOTS, WEIGHT_STAGE_ROWS, D_FF), f32),
            pltpu.SemaphoreType.DMA((WEIGHT_STAGE_SLOTS,)),
        ],
        compiler_params=pltpu.CompilerParams(
            dimension_semantics=("arbitrary",),
            vmem_limit_bytes=VMEM_LIMIT_BYTES),
        name="hymba_layer",
    )(x, ln1_g, w_in[0], pool_w, pool_scale, conv_w, conv_b, w_a, b_a, w_i, b_i, lam, gn_pool_g, gn_lru_g,
      w_out[0], ln2_g, w_ffn_gate[0], w_ffn_up[0], w_ffn_down[0], lnf_g.reshape(1, D))
    return out
```

```python
import functools
import math

import jax
import jax.numpy as jnp
from jax import lax
from jax.experimental import pallas as pl
from jax.experimental.pallas import tpu as pltpu

D_MODEL = 1024
POOL_WINDOWS = (2, 4, 8, 16)
POOL_GROUP = 128
POOL_WIDTH = 512
LRU_WIDTH = 512
LRU_HEADS = 8
LRU_HEAD_DIM = 64
CONV_WIDTH = 4
LRU_C = 8.0
D_FF = 2816
IN_WIDTH = POOL_WIDTH + 2 * LRU_WIDTH
EPS = 1e-6

LANES = 128
SUBLANES = 8
SEQ_TILE = 512
SUB_LEN = SEQ_TILE // SUBLANES
POOL_HEAD = 16
CONV_HEAD = CONV_WIDTH - 1
GATE_BLOCK_TILES = 8
FF_CHUNK = 256
N_FF_CHUNKS = D_FF // FF_CHUNK
FF_DOWN_PARTS = 4
FF_DOWN_GROUP = 6
HBM_WEIGHT_OPERANDS = (2, 14, 16, 17, 18)
WEIGHT_STAGE_ROWS = 256
WEIGHT_STAGE_SLOTS = 3
VMEM_LIMIT_BYTES = 58 * 1024 * 1024


def _ffn_sequence():
    ops = []
    for c in range(N_FF_CHUNKS):
        ops.append(("u", c))
        if c % FF_DOWN_GROUP == 0 and c > 0:
            ops += [("d", c - FF_DOWN_GROUP, part) for part in range(FF_DOWN_PARTS)]
    last = (N_FF_CHUNKS - 1) // FF_DOWN_GROUP * FF_DOWN_GROUP
    return ops + [("d", last, part) for part in range(FF_DOWN_PARTS)]


def _rms(x, g):
    ms = jnp.mean(x * x, axis=-1, keepdims=True)
    return x * lax.rsqrt(ms + EPS) * g


def _sigmoid(x):
    return 0.5 * (jnp.tanh(0.5 * x) + 1.0)


def _gelu_tanh_times(x, h):
    c = math.sqrt(2.0 / math.pi)
    t = jnp.tanh(x * ((x * x) * (0.044715 * c) + c))
    return (h * x) * (0.5 * t + 0.5)


def _dot(a, b):
    return jnp.dot(a, b, preferred_element_type=jnp.float32)


def _history_tiles(cur_tail, prev_tail):
    rows, cols = cur_tail.shape
    shape3 = (rows // SUBLANES, SUBLANES, cols)
    sub = lax.broadcasted_iota(jnp.int32, shape3, 1)
    cur = pltpu.roll(cur_tail.reshape(shape3), 1, axis=1)
    prv = pltpu.roll(prev_tail.reshape(shape3), 1, axis=1)
    return jnp.where(sub == 0, prv, cur).reshape(rows, cols)


def _load_weight_bf16(src_hbm, dst, stage, sems, scale=None):
    rows, cols = src_hbm.shape
    n = rows // WEIGHT_STAGE_ROWS

    def copy(r, slot):
        src = src_hbm.at[pl.ds(r * WEIGHT_STAGE_ROWS, WEIGHT_STAGE_ROWS), :]
        return pltpu.make_async_copy(src, stage.at[slot, :, pl.ds(0, cols)], sems.at[slot])

    ahead = WEIGHT_STAGE_SLOTS - 1
    for r in range(min(ahead, n)):
        copy(r, r % WEIGHT_STAGE_SLOTS).start()

    def body(r, carry):
        slot = lax.rem(r, WEIGHT_STAGE_SLOTS)

        @pl.when(r + ahead < n)
        def _():
            copy(r + ahead, lax.rem(r + ahead, WEIGHT_STAGE_SLOTS)).start()

        copy(r, slot).wait()
        row0 = pl.multiple_of(r * WEIGHT_STAGE_ROWS, WEIGHT_STAGE_ROWS)
        w = stage[slot, :, 0:cols]
        dst[pl.ds(row0, WEIGHT_STAGE_ROWS), :] = (w if scale is None else w * scale).astype(jnp.bfloat16)
        return carry

    lax.fori_loop(0, n, body, 0)


def _build_small_weights(pool_w_ref, w_a_ref, w_i_ref, small_w_ref):
    half = LRU_WIDTH // 2
    heads = half // LRU_HEAD_DIM
    bf16 = jnp.bfloat16
    c_idx = lax.broadcasted_iota(jnp.int32, (LRU_HEAD_DIM, half), 0)
    j_idx = lax.broadcasted_iota(jnp.int32, (LRU_HEAD_DIM, half), 1)
    spread = (lax.rem(j_idx, LRU_HEAD_DIM) == c_idx).astype(bf16)
    col_head = j_idx // LRU_HEAD_DIM
    for q in range(2):
        for part, w_ref in enumerate((w_a_ref, w_i_ref)):
            for h in range(heads):
                rep = _dot(w_ref[0, heads * q + h].astype(bf16), spread)
                blk = jnp.where(col_head == h, rep, 0.0).astype(bf16)
                small_w_ref[q, h * LRU_HEAD_DIM:(h + 1) * LRU_HEAD_DIM, part * half:(part + 1) * half] = blk
        small_w_ref[q, :, 2 * half:3 * half] = jnp.zeros((half, half), bf16)
        for g in range(2):
            rows = slice(g * POOL_GROUP, (g + 1) * POOL_GROUP)
            cols = slice(2 * half + g * POOL_GROUP, 2 * half + (g + 1) * POOL_GROUP)
            small_w_ref[q, rows, cols] = pool_w_ref[0, 2 * q + g].astype(bf16)


def _layer_kernel(x_hbm, ln1_ref, w_in_hbm, pool_w_ref, pool_scale_ref, conv_w_ref, conv_b_ref,
                  w_a_ref, b_a_ref, w_i_ref, b_i_ref, lam_ref, gn_pool_ref, gn_lru_ref, w_out_hbm,
                  ln2_ref, wg_hbm, wu_hbm, wd_hbm, lnf_ref, o_hbm,
                  xbuf, obuf, x_sems, o_sems, ptail, ltail, hcarry, h1buf, n2buf,
                  w_in_ref, w_out_ref, wg_ref, wu_ref, wd_ref, small_w_ref, stage, stage_sems,
                  *, chunks_per_seq, n_chunks):
    i = pl.program_id(0)
    s = lax.rem(i, chunks_per_seq)
    T = SEQ_TILE
    half = LRU_WIDTH // 2
    slot = lax.rem(i, 2)
    prev = 1 - slot

    def hbm_rows(ref, c, j):
        b = lax.div(c, chunks_per_seq)
        t0 = lax.rem(c, chunks_per_seq) * T + j * SUB_LEN
        return ref.at[b, pl.ds(t0, SUB_LEN), :]

    def x_copy(c, j, buf):
        return pltpu.make_async_copy(hbm_rows(x_hbm, c, j), xbuf.at[buf, :, j, :], x_sems.at[buf, j])

    def o_copy(c, j, buf):
        return pltpu.make_async_copy(obuf.at[buf, :, j, :], hbm_rows(o_hbm, c, j), o_sems.at[buf, j])

    @pl.when(i == 0)
    def _():
        for j in range(SUBLANES):
            x_copy(i, j, 0).start()

    @pl.when(i + 1 < n_chunks)
    def _():
        for j in range(SUBLANES):
            x_copy(i + 1, j, prev).start()

    @pl.when(i == 0)
    def _():
        for src, dst, scale in ((w_in_hbm, w_in_ref, None), (w_out_hbm, w_out_ref, None),
                                (wg_hbm, wg_ref, 0.5), (wu_hbm, wu_ref, None), (wd_hbm, wd_ref, None)):
            _load_weight_bf16(src, dst, stage, stage_sems, scale)
        _build_small_weights(pool_w_ref, w_a_ref, w_i_ref, small_w_ref)
        h1buf[1] = jnp.zeros((T, D_MODEL), jnp.float32)
        n2buf[1] = jnp.zeros((T, D_MODEL), jnp.bfloat16)

    @pl.when(s == 0)
    def _():
        ptail[...] = jnp.zeros_like(ptail)
        ltail[...] = jnp.zeros_like(ltail)
        hcarry[...] = jnp.zeros_like(hcarry)

    @pl.when(i < n_chunks)
    def _():
        for j in range(SUBLANES):
            x_copy(i, j, slot).wait()

    acts = {}
    group_acts = {}
    ffn_parts = {}
    pending = _ffn_sequence()
    part_width = D_MODEL // FF_DOWN_PARTS

    def ffn(n):
        for _ in range(n):
            op, c, *rest = pending.pop(0)
            if op == "u":
                cols = slice(c * FF_CHUNK, (c + 1) * FF_CHUNK)
                n2 = n2buf[prev]
                half_g = _dot(n2, wg_ref[:, cols])
                up = _dot(n2, wu_ref[:, cols])
                acts[c] = (half_g * (jnp.tanh(half_g) + 1.0) * up).astype(jnp.bfloat16)
            else:
                part, = rest
                c1 = min(c + FF_DOWN_GROUP, N_FF_CHUNKS)
                if c not in group_acts:
                    group_acts[c] = jnp.concatenate([acts.pop(k) for k in range(c, c1)], axis=-1)
                w_part = wd_ref[c * FF_CHUNK:c1 * FF_CHUNK, part * part_width:(part + 1) * part_width]
                ffn_parts[c, part] = _dot(group_acts[c], w_part)

    def x_rows():
        return xbuf[slot].reshape(T, D_MODEL)

    ffn(1)
    nb = _rms(x_rows(), ln1_ref[...]).astype(jnp.bfloat16)

    u_pool = _dot(nb, w_in_ref[:, 0:POOL_WIDTH])
    u_lru = _dot(nb, w_in_ref[:, POOL_WIDTH:POOL_WIDTH + LRU_WIDTH])

    ffn(1)
    n_groups = len(POOL_WINDOWS)
    head_rows = POOL_HEAD * SUBLANES
    pool_tail = u_pool[T - head_rows:, :]
    level = jnp.concatenate([_history_tiles(pool_tail, ptail[...]), u_pool], axis=0)
    ptail[...] = pool_tail
    hist = POOL_HEAD
    sums = []
    for g, w in enumerate(POOL_WINDOWS):
        assert w == 2 << g and w - 1 <= POOL_HEAD
        rest = level[:, (0 if g == 0 else POOL_GROUP):]
        k = (w // 2) * SUBLANES
        rest = rest[k:, :] + rest[:-k, :]
        hist -= w // 2
        sums.append(rest[hist * SUBLANES:, 0:POOL_GROUP])
        level = rest
    shape3 = (POOL_HEAD, SUBLANES, POOL_GROUP)
    head_pos = (s * T + 1 + lax.broadcasted_iota(jnp.int32, shape3, 0)
                + SUB_LEN * lax.broadcasted_iota(jnp.int32, shape3, 1))
    head_pos = head_pos.astype(jnp.float32).reshape(head_rows, POOL_GROUP)
    d_parts = []
    for g, w in enumerate(POOL_WINDOWS):
        ug = u_pool[:, g * POOL_GROUP:(g + 1) * POOL_GROUP]
        head = sums[g][0:head_rows] / jnp.minimum(head_pos, float(w))
        mean = jnp.concatenate([head, sums[g][head_rows:] * (1.0 / w)], axis=0)
        d_parts.append((mean - ug).astype(jnp.bfloat16))

    u_gate = _dot(nb, w_in_ref[:, POOL_WIDTH + LRU_WIDTH:])
    y_parts = []
    for p in range(n_groups // 2):
        d_pair = jnp.concatenate(d_parts[2 * p:2 * p + 2], axis=-1)
        y_parts.append(_dot(d_pair, small_w_ref[p, :, 2 * half:3 * half]))

    ffn(5)
    conv_rows = CONV_HEAD * SUBLANES
    lru_tail = u_lru[T - conv_rows:, :]
    u_ext = jnp.concatenate([_history_tiles(lru_tail, ltail[...]), u_lru], axis=0)
    ltail[...] = lru_tail
    xc = conv_b_ref[...] + conv_w_ref[0, CONV_WIDTH - 1:CONV_WIDTH, :] * u_lru
    for k in range(CONV_WIDTH - 1):
        xc = xc + conv_w_ref[0, k:k + 1, :] * u_ext[k * SUBLANES:k * SUBLANES + T, :]
    xcb = xc.astype(jnp.bfloat16)

    y_pool = jnp.concatenate(y_parts, axis=-1) * pool_scale_ref[...]
    mix_pool = _rms(y_pool, gn_pool_ref[...]).astype(jnp.bfloat16)

    neg_lam = -lam_ref[...]
    softplus = jnp.maximum(neg_lam, 0.0) + jnp.log1p(jnp.exp(-jnp.abs(neg_lam)))
    k_row = (0.5 * LRU_C) * softplus

    def gates(rows):
        z = [_dot(xcb[rows, q * half:(q + 1) * half], small_w_ref[q, :, 0:2 * half])
             for q in range(2)]
        za = jnp.concatenate([z[0][:, :half], z[1][:, :half]], axis=-1) + b_a_ref[...]
        zi = jnp.concatenate([z[0][:, half:], z[1][:, half:]], axis=-1) + b_i_ref[...]
        neg_log_a = jnp.tanh(0.5 * za) * k_row + k_row
        a = jnp.exp(-neg_log_a)
        one_minus_a2 = jnp.tanh(neg_log_a) * (a * a + 1.0)
        mult = jnp.where(one_minus_a2 > 0.0, one_minus_a2 * lax.rsqrt(one_minus_a2), 0.0)
        return a, mult * (_sigmoid(zi) * xc[rows, :])

    nq = LRU_WIDTH // LANES
    a_tiles, b_tiles = [], []
    prod = [jnp.ones((SUBLANES, LANES), jnp.float32) for _ in range(nq)]
    loc = [jnp.zeros((SUBLANES, LANES), jnp.float32) for _ in range(nq)]
    for blk in range(SUB_LEN // GATE_BLOCK_TILES):
        blk_rows = GATE_BLOCK_TILES * SUBLANES
        a_blk, b_blk = gates(slice(blk * blk_rows, (blk + 1) * blk_rows))
        ffn(1)
        for t in range(GATE_BLOCK_TILES):
            rows = slice(t * SUBLANES, (t + 1) * SUBLANES)
            a_tiles.append([a_blk[rows, q * LANES:(q + 1) * LANES] for q in range(nq)])
            b_tiles.append([b_blk[rows, q * LANES:(q + 1) * LANES] for q in range(nq)])
            for q in range(nq):
                loc[q] = a_tiles[-1][q] * loc[q] + b_tiles[-1][q]
                prod[q] = a_tiles[-1][q] * prod[q]
    row = lax.broadcasted_iota(jnp.int32, (SUBLANES, LANES), 0)
    h_in = []
    for q in range(nq):
        carry = hcarry[:, q * LANES:(q + 1) * LANES]
        start = jnp.zeros((SUBLANES, LANES), jnp.float32)
        for j in range(SUBLANES):
            start = jnp.where(row == j, carry, start)
            carry = prod[q][j:j + 1, :] * carry + loc[q][j:j + 1, :]
        h_in.append(start)
        hcarry[:, q * LANES:(q + 1) * LANES] = carry
    h_tiles = [[] for _ in range(nq)]
    for m in range(SUB_LEN):
        for q in range(nq):
            h_in[q] = a_tiles[m][q] * h_in[q] + b_tiles[m][q]
            h_tiles[q].append(h_in[q])
    h_lru = jnp.concatenate([jnp.concatenate(h_tiles[q], axis=0) for q in range(nq)], axis=-1)

    ffn(2)
    y_lru = _gelu_tanh_times(u_gate, h_lru)
    mix_lru = _rms(y_lru, gn_lru_ref[...]).astype(jnp.bfloat16)

    mix = jnp.concatenate([mix_pool, mix_lru], axis=-1)
    h1_new = x_rows() + _dot(mix, w_out_ref[...])
    ffn(2)
    h1buf[slot] = h1_new
    n2buf[slot] = _rms(h1_new, ln2_ref[...]).astype(jnp.bfloat16)

    ffn(len(pending))
    groups = sorted({c for c, _ in ffn_parts})
    ffn_out = jnp.concatenate(
        [functools.reduce(lambda u, v: u + v, [ffn_parts[c, part] for c in groups])
         for part in range(FF_DOWN_PARTS)], axis=-1)
    out = _rms(h1buf[prev] + ffn_out, lnf_ref[...])
    obuf[prev] = out.reshape(SUB_LEN, SUBLANES, D_MODEL)

    @pl.when(i >= 1)
    def _():
        for j in range(SUBLANES):
            o_copy(i - 1, j, prev).start()

    @pl.when(i >= 2)
    def _():
        for j in range(SUBLANES):
            o_copy(i - 2, j, slot).wait()

    @pl.when(i == n_chunks)
    def _():
        for j in range(SUBLANES):
            o_copy(i - 1, j, prev).wait()


def kernel(x, ln1_g, w_in, pool_w, pool_scale, conv_w, conv_b, w_a, b_a, w_i, b_i, lam, gn_pool_g,
           gn_lru_g, w_out, ln2_g, w_ffn_gate, w_ffn_up, w_ffn_down, lnf_g):
    B, S, D = x.shape
    assert D == D_MODEL and S % SEQ_TILE == 0 and ln1_g.shape[0] == 1
    bf16 = jnp.bfloat16
    f32 = jnp.float32
    vmem = pl.BlockSpec(memory_space=pltpu.VMEM)
    hbm = pl.BlockSpec(memory_space=pl.ANY)
    cps = S // SEQ_TILE
    n_chunks = B * cps

    out = pl.pallas_call(
        functools.partial(_layer_kernel, chunks_per_seq=cps, n_chunks=n_chunks),
        grid=(n_chunks + 1,),
        in_specs=[hbm] + [hbm if k in HBM_WEIGHT_OPERANDS else vmem for k in range(1, 20)],
        out_specs=hbm,
        out_shape=jax.ShapeDtypeStruct((B, S, D), f32),
        scratch_shapes=[
            pltpu.VMEM((2, SUB_LEN, SUBLANES, D), f32),
            pltpu.VMEM((2, SUB_LEN, SUBLANES, D), f32),
            pltpu.SemaphoreType.DMA((2, SUBLANES)),
            pltpu.SemaphoreType.DMA((2, SUBLANES)),
            pltpu.VMEM((POOL_HEAD * SUBLANES, POOL_WIDTH), f32),
            pltpu.VMEM((CONV_HEAD * SUBLANES, LRU_WIDTH), f32),
            pltpu.VMEM((1, LRU_WIDTH), f32),
            pltpu.VMEM((2, SEQ_TILE, D), f32),
            pltpu.VMEM((2, SEQ_TILE, D), bf16),
            pltpu.VMEM((D, IN_WIDTH), bf16),
            pltpu.VMEM((D, D), bf16),
            pltpu.VMEM((D, D_FF), bf16),
            pltpu.VMEM((D, D_FF), bf16),
            pltpu.VMEM((D_FF, D), bf16),
            pltpu.VMEM((2, LRU_WIDTH // 2, LRU_WIDTH + LRU_WIDTH // 2), bf16),
            pltpu.VMEM((WEIGHT_STAGE_SLOTS, WEIGHT_STAGE_ROWS, D_FF), f32),
            pltpu.SemaphoreType.DMA((WEIGHT_STAGE_SLOTS,)),
        ],
        compiler_params=pltpu.CompilerParams(
            dimension_semantics=("arbitrary",),
            vmem_limit_bytes=VMEM_LIMIT_BYTES),
        name="hymba_layer",
    )(x, ln1_g, w_in[0], pool_w, pool_scale, conv_w, conv_b, w_a, b_a, w_i, b_i, lam, gn_pool_g, gn_lru_g,
      w_out[0], ln2_g, w_ffn_gate[0], w_ffn_up[0], w_ffn_down[0], lnf_g.reshape(1, D))
    return out
```

```python
import functools
import math

import jax
import jax.numpy as jnp
from jax import lax
from jax.experimental import pallas as pl
from jax.experimental.pallas import tpu as pltpu

D_MODEL = 1024
POOL_WINDOWS = (2, 4, 8, 16)
POOL_GROUP = 128
POOL_WIDTH = 512
LRU_WIDTH = 512
LRU_HEADS = 8
LRU_HEAD_DIM = 64
CONV_WIDTH = 4
LRU_C = 8.0
D_FF = 2816
IN_WIDTH = POOL_WIDTH + 2 * LRU_WIDTH
EPS = 1e-6

LANES = 128
SUBLANES = 8
SEQ_TILE = 512
POOL_TAIL = 16
CONV_TAIL = 8
SUB_LEN = SEQ_TILE // SUBLANES
SUB_PITCH = SUB_LEN + SUBLANES
FF_CHUNK = 256
N_FF_CHUNKS = D_FF // FF_CHUNK
FF_DOWN_GROUP = 6
HBM_WEIGHT_OPERANDS = (2, 14, 16, 17, 18)
WEIGHT_STAGE_ROWS = 256
WEIGHT_STAGE_SLOTS = 3
VMEM_LIMIT_BYTES = 58 * 1024 * 1024


def _ffn_sequence():
    ops = []
    for c in range(N_FF_CHUNKS):
        ops.append(("u", c))
        if c % FF_DOWN_GROUP == 0 and c > 0:
            ops.append(("d", c - FF_DOWN_GROUP))
    last = (N_FF_CHUNKS - 1) // FF_DOWN_GROUP * FF_DOWN_GROUP
    return ops + [("d", last)]


def _rms(x, g):
    ms = jnp.mean(x * x, axis=-1, keepdims=True)
    return x * lax.rsqrt(ms + EPS) * g


def _sigmoid(x):
    return 0.5 * (jnp.tanh(0.5 * x) + 1.0)


def _gelu_tanh_times(x, h):
    c = math.sqrt(2.0 / math.pi)
    t = jnp.tanh(x * ((x * x) * (0.044715 * c) + c))
    return (h * x) * (0.5 * t + 0.5)


def _dot(a, b):
    return jnp.dot(a, b, preferred_element_type=jnp.float32)


def _load_weight_bf16(src_hbm, dst, stage, sems, scale=None):
    rows, cols = src_hbm.shape
    n = rows // WEIGHT_STAGE_ROWS

    def copy(r, slot):
        src = src_hbm.at[pl.ds(r * WEIGHT_STAGE_ROWS, WEIGHT_STAGE_ROWS), :]
        return pltpu.make_async_copy(src, stage.at[slot, :, pl.ds(0, cols)], sems.at[slot])

    ahead = WEIGHT_STAGE_SLOTS - 1
    for r in range(min(ahead, n)):
        copy(r, r % WEIGHT_STAGE_SLOTS).start()

    def body(r, carry):
        slot = lax.rem(r, WEIGHT_STAGE_SLOTS)

        @pl.when(r + ahead < n)
        def _():
            copy(r + ahead, lax.rem(r + ahead, WEIGHT_STAGE_SLOTS)).start()

        copy(r, slot).wait()
        row0 = pl.multiple_of(r * WEIGHT_STAGE_ROWS, WEIGHT_STAGE_ROWS)
        w = stage[slot, :, 0:cols]
        dst[pl.ds(row0, WEIGHT_STAGE_ROWS), :] = (w if scale is None else w * scale).astype(jnp.bfloat16)
        return carry

    lax.fori_loop(0, n, body, 0)


def _build_small_weights(pool_w_ref, w_a_ref, w_i_ref, small_w_ref):
    half = LRU_WIDTH // 2
    heads = half // LRU_HEAD_DIM
    bf16 = jnp.bfloat16
    c_idx = lax.broadcasted_iota(jnp.int32, (LRU_HEAD_DIM, half), 0)
    j_idx = lax.broadcasted_iota(jnp.int32, (LRU_HEAD_DIM, half), 1)
    spread = (lax.rem(j_idx, LRU_HEAD_DIM) == c_idx).astype(bf16)
    col_head = j_idx // LRU_HEAD_DIM
    for q in range(2):
        for part, w_ref in enumerate((w_a_ref, w_i_ref)):
            for h in range(heads):
                rep = _dot(w_ref[0, heads * q + h].astype(bf16), spread)
                blk = jnp.where(col_head == h, rep, 0.0).astype(bf16)
                small_w_ref[q, h * LRU_HEAD_DIM:(h + 1) * LRU_HEAD_DIM, part * half:(part + 1) * half] = blk
        small_w_ref[q, :, 2 * half:3 * half] = jnp.zeros((half, half), bf16)
        for g in range(2):
            rows = slice(g * POOL_GROUP, (g + 1) * POOL_GROUP)
            cols = slice(2 * half + g * POOL_GROUP, 2 * half + (g + 1) * POOL_GROUP)
            small_w_ref[q, rows, cols] = pool_w_ref[0, 2 * q + g].astype(bf16)


def _layer_kernel(x_ref, ln1_ref, w_in_hbm, pool_w_ref, pool_scale_ref, conv_w_ref, conv_b_ref,
                  w_a_ref, b_a_ref, w_i_ref, b_i_ref, lam_ref, gn_pool_ref, gn_lru_ref, w_out_hbm,
                  ln2_ref, wg_hbm, wu_hbm, wd_hbm, lnf_ref, o_ref,
                  pbuf, lbuf, abuf, bbuf, hbuf, hcarry, h1buf, n2buf,
                  w_in_ref, w_out_ref, wg_ref, wu_ref, wd_ref, small_w_ref, stage, stage_sems,
                  *, chunks_per_seq, n_chunks):
    i = pl.program_id(0)
    s = lax.rem(i, chunks_per_seq)
    T = SEQ_TILE
    half = LRU_WIDTH // 2

    @pl.when(i == 0)
    def _():
        for src, dst, scale in ((w_in_hbm, w_in_ref, None), (w_out_hbm, w_out_ref, None),
                                (wg_hbm, wg_ref, 0.5), (wu_hbm, wu_ref, None), (wd_hbm, wd_ref, None)):
            _load_weight_bf16(src, dst, stage, stage_sems, scale)
        _build_small_weights(pool_w_ref, w_a_ref, w_i_ref, small_w_ref)

    slot = lax.rem(i, 2)
    prev = 1 - slot

    def mixer():
        @pl.when(s == 0)
        def _():
            pbuf[0:POOL_TAIL, :] = jnp.zeros((POOL_TAIL, POOL_WIDTH), jnp.float32)
            lbuf[0:CONV_TAIL, :] = jnp.zeros((CONV_TAIL, LRU_WIDTH), jnp.float32)
            hcarry[...] = jnp.zeros_like(hcarry)

        yield 1
        x = x_ref[...]
        nb = _rms(x, ln1_ref[...]).astype(jnp.bfloat16)

        pbuf[POOL_TAIL:POOL_TAIL + T, :] = _dot(nb, w_in_ref[:, 0:POOL_WIDTH])
        lbuf[CONV_TAIL:CONV_TAIL + T, :] = _dot(nb, w_in_ref[:, POOL_WIDTH:POOL_WIDTH + LRU_WIDTH])

        yield 1
        n_groups = len(POOL_WINDOWS)
        level = pbuf[...]
        sums = []
        for g, w in enumerate(POOL_WINDOWS):
            assert w == 2 << g and w - 1 <= POOL_TAIL
            rest = level[:, (0 if g == 0 else POOL_GROUP):]
            rest = rest + pltpu.roll(rest, w // 2, axis=0)
            sums.append(rest[POOL_TAIL:, 0:POOL_GROUP])
            level = rest
        head_pos = (s * T + 1 + lax.broadcasted_iota(jnp.int32, (POOL_TAIL, POOL_GROUP), 0)).astype(jnp.float32)
        d_parts = []
        for g, w in enumerate(POOL_WINDOWS):
            ug = pbuf[POOL_TAIL:POOL_TAIL + T, g * POOL_GROUP:(g + 1) * POOL_GROUP]
            head = sums[g][0:POOL_TAIL] / jnp.minimum(head_pos, float(w))
            mean = jnp.concatenate([head, sums[g][POOL_TAIL:] * (1.0 / w)], axis=0)
            d_parts.append((mean - ug).astype(jnp.bfloat16))

        u_gate = _dot(nb, w_in_ref[:, POOL_WIDTH + LRU_WIDTH:])
        y_parts = []
        for p in range(n_groups // 2):
            d_pair = jnp.concatenate(d_parts[2 * p:2 * p + 2], axis=-1)
            y_parts.append(_dot(d_pair, small_w_ref[p, :, 2 * half:3 * half]))

        yield 1
        assert CONV_WIDTH == 4
        u_ext = lbuf[...]
        u_prev = pltpu.roll(u_ext, 1, axis=0)
        tap = lambda k: conv_w_ref[0, k:k + 1, :]
        older = tap(1) * u_ext + tap(0) * u_prev
        xc = (conv_b_ref[...] + tap(3) * u_ext + tap(2) * u_prev + pltpu.roll(older, 2, axis=0))[CONV_TAIL:]
        xcb = xc.astype(jnp.bfloat16)
        z = [_dot(xcb[:, q * half:(q + 1) * half], small_w_ref[q, :, 0:2 * half])
             for q in range(2)]

        yield 2
        y_pool = jnp.concatenate(y_parts, axis=-1) * pool_scale_ref[...]
        mix_pool = _rms(y_pool, gn_pool_ref[...]).astype(jnp.bfloat16)

        za = jnp.concatenate([z[0][:, :half], z[1][:, :half]], axis=-1) + b_a_ref[...]
        zi = jnp.concatenate([z[0][:, half:], z[1][:, half:]], axis=-1) + b_i_ref[...]
        gate_i = _sigmoid(zi)
        neg_lam = -lam_ref[...]
        softplus = jnp.maximum(neg_lam, 0.0) + jnp.log1p(jnp.exp(-jnp.abs(neg_lam)))
        k_row = (0.5 * LRU_C) * softplus
        neg_log_a = jnp.tanh(0.5 * za) * k_row + k_row
        a = jnp.exp(-neg_log_a)
        one_minus_a2 = jnp.tanh(neg_log_a) * (a * a + 1.0)
        mult = jnp.where(one_minus_a2 > 0.0, one_minus_a2 * lax.rsqrt(one_minus_a2), 0.0)
        bterm = mult * (gate_i * xc)

        nq = LRU_WIDTH // LANES
        tiles_per_sub = SUB_LEN // SUBLANES
        for q in range(nq):
            lanes = slice(q * LANES, (q + 1) * LANES)
            for t in range(T // SUBLANES):
                j, m0 = divmod(t, tiles_per_sub)
                dst = pl.ds(m0 * SUBLANES * SUBLANES + j, SUBLANES, stride=SUBLANES)
                rows = slice(t * SUBLANES, (t + 1) * SUBLANES)
                abuf[q, dst, :] = a[rows, lanes]
                bbuf[q, dst, :] = bterm[rows, lanes]

        def tile(m):
            return slice(m * SUBLANES, (m + 1) * SUBLANES)

        yield 2
        prod = [jnp.ones((SUBLANES, LANES), jnp.float32) for _ in range(nq)]
        loc = [jnp.zeros((SUBLANES, LANES), jnp.float32) for _ in range(nq)]
        for m in range(SUB_LEN):
            for q in range(nq):
                am = abuf[q, tile(m), :]
                loc[q] = am * loc[q] + bbuf[q, tile(m), :]
                prod[q] = am * prod[q]
        row = lax.broadcasted_iota(jnp.int32, (SUBLANES, LANES), 0)
        h_in = []
        for q in range(nq):
            carry = hcarry[:, q * LANES:(q + 1) * LANES]
            start = jnp.zeros((SUBLANES, LANES), jnp.float32)
            for j in range(SUBLANES):
                start = jnp.where(row == j, carry, start)
                carry = prod[q][j:j + 1, :] * carry + loc[q][j:j + 1, :]
            h_in.append(start)
            hcarry[:, q * LANES:(q + 1) * LANES] = carry
        yield 2
        for m in range(SUB_LEN):
            for q in range(nq):
                h_in[q] = abuf[q, tile(m), :] * h_in[q] + bbuf[q, tile(m), :]
                hbuf[q, pl.ds(m, SUBLANES, stride=SUB_PITCH), :] = h_in[q]
        h_lru = jnp.concatenate(
            [jnp.concatenate([hbuf[q, j * SUB_PITCH:j * SUB_PITCH + SUB_LEN, :] for j in range(SUBLANES)], axis=0)
             for q in range(nq)], axis=-1)

        yield 2
        y_lru = _gelu_tanh_times(u_gate, h_lru)
        mix_lru = _rms(y_lru, gn_lru_ref[...]).astype(jnp.bfloat16)

        mix = jnp.concatenate([mix_pool, mix_lru], axis=-1)
        h1_new = x + _dot(mix, w_out_ref[...])
        yield 2
        h1buf[slot] = h1_new
        n2buf[slot] = _rms(h1_new, ln2_ref[...]).astype(jnp.bfloat16)

        pbuf[0:POOL_TAIL, :] = pbuf[T:T + POOL_TAIL, :]
        lbuf[0:CONV_TAIL, :] = lbuf[T:T + CONV_TAIL, :]

    def make_ffn():
        acts = {}
        ffn_parts = []
        pending = _ffn_sequence()

        def ffn(n):
            for _ in range(n):
                op, c = pending.pop(0)
                if op == "u":
                    cols = slice(c * FF_CHUNK, (c + 1) * FF_CHUNK)
                    n2 = n2buf[prev]
                    half_g = _dot(n2, wg_ref[:, cols])
                    up = _dot(n2, wu_ref[:, cols])
                    acts[c] = (half_g * (jnp.tanh(half_g) + 1.0) * up).astype(jnp.bfloat16)
                else:
                    group = range(c, min(c + FF_DOWN_GROUP, N_FF_CHUNKS))
                    act = jnp.concatenate([acts.pop(k) for k in group], axis=-1)
                    ffn_parts.append(_dot(act, wd_ref[c * FF_CHUNK:(group[-1] + 1) * FF_CHUNK, :]))

        def finish():
            ffn(len(pending))
            ffn_out = ffn_parts[0]
            for part in ffn_parts[1:]:
                ffn_out = ffn_out + part
            o_ref[...] = _rms(h1buf[prev] + ffn_out, lnf_ref[...])

        return ffn, finish

    @pl.when(i == 0)
    def _():
        for _ in mixer():
            pass

    @pl.when(jnp.logical_and(i > 0, i < n_chunks))
    def _():
        ffn, finish = make_ffn()
        for n in mixer():
            ffn(n)
        finish()

    @pl.when(i == n_chunks)
    def _():
        _, finish = make_ffn()
        finish()


def kernel(x, ln1_g, w_in, pool_w, pool_scale, conv_w, conv_b, w_a, b_a, w_i, b_i, lam, gn_pool_g,
           gn_lru_g, w_out, ln2_g, w_ffn_gate, w_ffn_up, w_ffn_down, lnf_g):
    B, S, D = x.shape
    assert D == D_MODEL and S % SEQ_TILE == 0 and ln1_g.shape[0] == 1
    bf16 = jnp.bfloat16
    vmem = pl.BlockSpec(memory_space=pltpu.VMEM)
    hbm = pl.BlockSpec(memory_space=pl.ANY)
    nq = LRU_WIDTH // LANES
    scan_rows = SUBLANES * SUB_PITCH
    cps = S // SEQ_TILE
    n_chunks = B * cps

    def in_index(i):
        c = jnp.minimum(i, n_chunks - 1)
        return (c // cps, c % cps, 0)

    def out_index(i):
        c = jnp.maximum(i - 1, 0)
        return (c // cps, c % cps, 0)

    out = pl.pallas_call(
        functools.partial(_layer_kernel, chunks_per_seq=cps, n_chunks=n_chunks),
        grid=(n_chunks + 1,),
        in_specs=[pl.BlockSpec((None, SEQ_TILE, D), in_index)] + [
            hbm if k in HBM_WEIGHT_OPERANDS else vmem for k in range(1, 20)],
        out_specs=pl.BlockSpec((None, SEQ_TILE, D), out_index),
        out_shape=jax.ShapeDtypeStruct((B, S, D), jnp.float32),
        scratch_shapes=[
            pltpu.VMEM((POOL_TAIL + SEQ_TILE, POOL_WIDTH), jnp.float32),
            pltpu.VMEM((CONV_TAIL + SEQ_TILE, LRU_WIDTH), jnp.float32),
            pltpu.VMEM((nq, SEQ_TILE, LANES), jnp.float32),
            pltpu.VMEM((nq, SEQ_TILE, LANES), jnp.float32),
            pltpu.VMEM((nq, scan_rows, LANES), jnp.float32),
            pltpu.VMEM((1, LRU_WIDTH), jnp.float32),
            pltpu.VMEM((2, SEQ_TILE, D), jnp.float32),
            pltpu.VMEM((2, SEQ_TILE, D), jnp.bfloat16),
            pltpu.VMEM((D, IN_WIDTH), bf16),
            pltpu.VMEM((D, D), bf16),
            pltpu.VMEM((D, D_FF), bf16),
            pltpu.VMEM((D, D_FF), bf16),
            pltpu.VMEM((D_FF, D), bf16),
            pltpu.VMEM((2, LRU_WIDTH // 2, LRU_WIDTH + LRU_WIDTH // 2), bf16),
            pltpu.VMEM((WEIGHT_STAGE_SLOTS, WEIGHT_STAGE_ROWS, D_FF), jnp.float32),
            pltpu.SemaphoreType.DMA((WEIGHT_STAGE_SLOTS,)),
        ],
        compiler_params=pltpu.CompilerParams(
            dimension_semantics=("arbitrary",),
            vmem_limit_bytes=VMEM_LIMIT_BYTES),
        name="hymba_layer",
    )(x, ln1_g, w_in[0], pool_w, pool_scale, conv_w, conv_b, w_a, b_a, w_i, b_i, lam, gn_pool_g, gn_lru_g,
      w_out[0], ln2_g, w_ffn_gate[0], w_ffn_up[0], w_ffn_down[0], lnf_g.reshape(1, D))
    return out
```

```python
import functools
import math

import jax
import jax.numpy as jnp
from jax import lax
from jax.experimental import pallas as pl
from jax.experimental.pallas import tpu as pltpu

D_MODEL = 1024
POOL_WINDOWS = (2, 4, 8, 16)
POOL_GROUP = 128
POOL_WIDTH = 512
LRU_WIDTH = 512
LRU_HEADS = 8
LRU_HEAD_DIM = 64
CONV_WIDTH = 4
LRU_C = 8.0
D_FF = 2816
IN_WIDTH = POOL_WIDTH + 2 * LRU_WIDTH
EPS = 1e-6

LANES = 128
SUBLANES = 8
SEQ_TILE = 512
POOL_TAIL = 16
CONV_TAIL = 8
SUB_LEN = SEQ_TILE // SUBLANES
SUB_PITCH = SUB_LEN + SUBLANES
FF_CHUNK = 256
N_FF_CHUNKS = D_FF // FF_CHUNK
FF_DOWN_GROUP = 6
HBM_WEIGHT_OPERANDS = (2, 14, 16, 17, 18)
WEIGHT_STAGE_ROWS = 128
PREP_CHUNKS_PER_FFN_SLOT = 4
WEIGHT_STAGE_SLOTS = 3
VMEM_LIMIT_BYTES = 58 * 1024 * 1024


def _ffn_sequence():
    ops = []
    for c in range(N_FF_CHUNKS):
        ops.append(("u", c))
        if c % FF_DOWN_GROUP == 0 and c > 0:
            ops.append(("d", c - FF_DOWN_GROUP))
    last = (N_FF_CHUNKS - 1) // FF_DOWN_GROUP * FF_DOWN_GROUP
    return ops + [("d", last)]


def _rms(x, g):
    ms = jnp.mean(x * x, axis=-1, keepdims=True)
    return x * lax.rsqrt(ms + EPS) * g


def _sigmoid(x):
    return 0.5 * (jnp.tanh(0.5 * x) + 1.0)


def _gelu_tanh_times(x, h):
    c = math.sqrt(2.0 / math.pi)
    t = jnp.tanh(x * ((x * x) * (0.044715 * c) + c))
    return (h * x) * (0.5 * t + 0.5)


def _dot(a, b):
    return jnp.dot(a, b, preferred_element_type=jnp.float32)


def _load_weight_bf16(src_hbm, dst, stage, sems, scale=None):
    rows, cols = src_hbm.shape
    n = rows // WEIGHT_STAGE_ROWS

    def copy(r, slot):
        src = src_hbm.at[pl.ds(r * WEIGHT_STAGE_ROWS, WEIGHT_STAGE_ROWS), :]
        return pltpu.make_async_copy(src, stage.at[slot, :, pl.ds(0, cols)], sems.at[slot])

    ahead = WEIGHT_STAGE_SLOTS - 1
    for r in range(min(ahead, n)):
        copy(r, r % WEIGHT_STAGE_SLOTS).start()

    def body(r, carry):
        slot = lax.rem(r, WEIGHT_STAGE_SLOTS)

        @pl.when(r + ahead < n)
        def _():
            copy(r + ahead, lax.rem(r + ahead, WEIGHT_STAGE_SLOTS)).start()

        copy(r, slot).wait()
        row0 = pl.multiple_of(r * WEIGHT_STAGE_ROWS, WEIGHT_STAGE_ROWS)
        w = stage[slot, :, 0:cols]
        dst[pl.ds(row0, WEIGHT_STAGE_ROWS), :] = (w if scale is None else w * scale).astype(jnp.bfloat16)
        return carry

    lax.fori_loop(0, n, body, 0)


def _cast_weights_stepwise(weights, stage, sems):
    chunks = [(src, dst, scale, r * WEIGHT_STAGE_ROWS)
              for src, dst, scale in weights for r in range(src.shape[0] // WEIGHT_STAGE_ROWS)]

    def copy(k):
        src, _, _, row0 = chunks[k]
        slot = k % WEIGHT_STAGE_SLOTS
        return pltpu.make_async_copy(src.at[pl.ds(row0, WEIGHT_STAGE_ROWS), :],
                                     stage.at[slot, :, pl.ds(0, src.shape[1])], sems.at[slot])

    ahead = WEIGHT_STAGE_SLOTS - 1
    for k in range(min(ahead, len(chunks))):
        copy(k).start()
    yield
    for k, (src, dst, scale, row0) in enumerate(chunks):
        if k + ahead < len(chunks):
            copy(k + ahead).start()
        copy(k).wait()
        w = stage[k % WEIGHT_STAGE_SLOTS, :, 0:src.shape[1]]
        dst[row0:row0 + WEIGHT_STAGE_ROWS, :] = (w if scale is None else w * scale).astype(jnp.bfloat16)
        yield


def _build_small_weights(pool_w_ref, w_a_ref, w_i_ref, small_w_ref):
    half = LRU_WIDTH // 2
    heads = half // LRU_HEAD_DIM
    bf16 = jnp.bfloat16
    c_idx = lax.broadcasted_iota(jnp.int32, (LRU_HEAD_DIM, half), 0)
    j_idx = lax.broadcasted_iota(jnp.int32, (LRU_HEAD_DIM, half), 1)
    spread = (lax.rem(j_idx, LRU_HEAD_DIM) == c_idx).astype(bf16)
    col_head = j_idx // LRU_HEAD_DIM
    for q in range(2):
        for part, w_ref in enumerate((w_a_ref, w_i_ref)):
            for h in range(heads):
                rep = _dot(w_ref[0, heads * q + h].astype(bf16), spread)
                blk = jnp.where(col_head == h, rep, 0.0).astype(bf16)
                small_w_ref[q, h * LRU_HEAD_DIM:(h + 1) * LRU_HEAD_DIM, part * half:(part + 1) * half] = blk
        small_w_ref[q, :, 2 * half:3 * half] = jnp.zeros((half, half), bf16)
        for g in range(2):
            rows = slice(g * POOL_GROUP, (g + 1) * POOL_GROUP)
            cols = slice(2 * half + g * POOL_GROUP, 2 * half + (g + 1) * POOL_GROUP)
            small_w_ref[q, rows, cols] = pool_w_ref[0, 2 * q + g].astype(bf16)


def _layer_kernel(x_ref, ln1_ref, w_in_hbm, pool_w_ref, pool_scale_ref, conv_w_ref, conv_b_ref,
                  w_a_ref, b_a_ref, w_i_ref, b_i_ref, lam_ref, gn_pool_ref, gn_lru_ref, w_out_hbm,
                  ln2_ref, wg_hbm, wu_hbm, wd_hbm, lnf_ref, o_ref,
                  pbuf, lbuf, abuf, bbuf, hbuf, hcarry, h1buf, n2buf,
                  w_in_ref, w_out_ref, wg_ref, wu_ref, wd_ref, small_w_ref, stage, stage_sems,
                  *, chunks_per_seq, n_chunks):
    i = pl.program_id(0)
    s = lax.rem(i, chunks_per_seq)
    T = SEQ_TILE
    half = LRU_WIDTH // 2

    slot = lax.rem(i, 2)
    prev = 1 - slot

    def mixer():
        @pl.when(s == 0)
        def _():
            pbuf[0:POOL_TAIL, :] = jnp.zeros((POOL_TAIL, POOL_WIDTH), jnp.float32)
            lbuf[0:CONV_TAIL, :] = jnp.zeros((CONV_TAIL, LRU_WIDTH), jnp.float32)
            hcarry[...] = jnp.zeros_like(hcarry)

        yield 1
        x = x_ref[...]
        nb = _rms(x, ln1_ref[...]).astype(jnp.bfloat16)

        pbuf[POOL_TAIL:POOL_TAIL + T, :] = _dot(nb, w_in_ref[:, 0:POOL_WIDTH])
        lbuf[CONV_TAIL:CONV_TAIL + T, :] = _dot(nb, w_in_ref[:, POOL_WIDTH:POOL_WIDTH + LRU_WIDTH])

        yield 1
        n_groups = len(POOL_WINDOWS)
        level = pbuf[...]
        sums = []
        for g, w in enumerate(POOL_WINDOWS):
            assert w == 2 << g and w - 1 <= POOL_TAIL
            rest = level[:, (0 if g == 0 else POOL_GROUP):]
            rest = rest + pltpu.roll(rest, w // 2, axis=0)
            sums.append(rest[POOL_TAIL:, 0:POOL_GROUP])
            level = rest
        head_pos = (s * T + 1 + lax.broadcasted_iota(jnp.int32, (POOL_TAIL, POOL_GROUP), 0)).astype(jnp.float32)
        d_parts = []
        for g, w in enumerate(POOL_WINDOWS):
            ug = pbuf[POOL_TAIL:POOL_TAIL + T, g * POOL_GROUP:(g + 1) * POOL_GROUP]
            head = sums[g][0:POOL_TAIL] / jnp.minimum(head_pos, float(w))
            mean = jnp.concatenate([head, sums[g][POOL_TAIL:] * (1.0 / w)], axis=0)
            d_parts.append((mean - ug).astype(jnp.bfloat16))

        u_gate = _dot(nb, w_in_ref[:, POOL_WIDTH + LRU_WIDTH:])
        y_parts = []
        for p in range(n_groups // 2):
            d_pair = jnp.concatenate(d_parts[2 * p:2 * p + 2], axis=-1)
            y_parts.append(_dot(d_pair, small_w_ref[p, :, 2 * half:3 * half]))

        yield 1
        assert CONV_WIDTH == 4
        u_ext = lbuf[...]
        u_prev = pltpu.roll(u_ext, 1, axis=0)
        tap = lambda k: conv_w_ref[0, k:k + 1, :]
        older = tap(1) * u_ext + tap(0) * u_prev
        xc = (conv_b_ref[...] + tap(3) * u_ext + tap(2) * u_prev + pltpu.roll(older, 2, axis=0))[CONV_TAIL:]
        xcb = xc.astype(jnp.bfloat16)
        z = [_dot(xcb[:, q * half:(q + 1) * half], small_w_ref[q, :, 0:2 * half])
             for q in range(2)]

        yield 2
        y_pool = jnp.concatenate(y_parts, axis=-1) * pool_scale_ref[...]
        mix_pool = _rms(y_pool, gn_pool_ref[...]).astype(jnp.bfloat16)

        za = jnp.concatenate([z[0][:, :half], z[1][:, :half]], axis=-1) + b_a_ref[...]
        zi = jnp.concatenate([z[0][:, half:], z[1][:, half:]], axis=-1) + b_i_ref[...]
        gate_i = _sigmoid(zi)
        neg_lam = -lam_ref[...]
        softplus = jnp.maximum(neg_lam, 0.0) + jnp.log1p(jnp.exp(-jnp.abs(neg_lam)))
        k_row = (0.5 * LRU_C) * softplus
        neg_log_a = jnp.tanh(0.5 * za) * k_row + k_row
        a = jnp.exp(-neg_log_a)
        one_minus_a2 = jnp.tanh(neg_log_a) * (a * a + 1.0)
        mult = jnp.where(one_minus_a2 > 0.0, one_minus_a2 * lax.rsqrt(one_minus_a2), 0.0)
        bterm = mult * (gate_i * xc)

        nq = LRU_WIDTH // LANES
        tiles_per_sub = SUB_LEN // SUBLANES
        for q in range(nq):
            lanes = slice(q * LANES, (q + 1) * LANES)
            for t in range(T // SUBLANES):
                j, m0 = divmod(t, tiles_per_sub)
                dst = pl.ds(m0 * SUBLANES * SUBLANES + j, SUBLANES, stride=SUBLANES)
                rows = slice(t * SUBLANES, (t + 1) * SUBLANES)
                abuf[q, dst, :] = a[rows, lanes]
                bbuf[q, dst, :] = bterm[rows, lanes]

        def tile(m):
            return slice(m * SUBLANES, (m + 1) * SUBLANES)

        yield 2
        prod = [jnp.ones((SUBLANES, LANES), jnp.float32) for _ in range(nq)]
        loc = [jnp.zeros((SUBLANES, LANES), jnp.float32) for _ in range(nq)]
        for m in range(SUB_LEN):
            for q in range(nq):
                am = abuf[q, tile(m), :]
                loc[q] = am * loc[q] + bbuf[q, tile(m), :]
                prod[q] = am * prod[q]
        row = lax.broadcasted_iota(jnp.int32, (SUBLANES, LANES), 0)
        h_in = []
        for q in range(nq):
            carry = hcarry[:, q * LANES:(q + 1) * LANES]
            start = jnp.zeros((SUBLANES, LANES), jnp.float32)
            for j in range(SUBLANES):
                start = jnp.where(row == j, carry, start)
                carry = prod[q][j:j + 1, :] * carry + loc[q][j:j + 1, :]
            h_in.append(start)
            hcarry[:, q * LANES:(q + 1) * LANES] = carry
        yield 2
        for m in range(SUB_LEN):
            for q in range(nq):
                h_in[q] = abuf[q, tile(m), :] * h_in[q] + bbuf[q, tile(m), :]
                hbuf[q, pl.ds(m, SUBLANES, stride=SUB_PITCH), :] = h_in[q]
        h_lru = jnp.concatenate(
            [jnp.concatenate([hbuf[q, j * SUB_PITCH:j * SUB_PITCH + SUB_LEN, :] for j in range(SUBLANES)], axis=0)
             for q in range(nq)], axis=-1)

        yield 2
        y_lru = _gelu_tanh_times(u_gate, h_lru)
        mix_lru = _rms(y_lru, gn_lru_ref[...]).astype(jnp.bfloat16)

        mix = jnp.concatenate([mix_pool, mix_lru], axis=-1)
        h1_new = x + _dot(mix, w_out_ref[...])
        yield 2
        h1buf[slot] = h1_new
        n2buf[slot] = _rms(h1_new, ln2_ref[...]).astype(jnp.bfloat16)

        pbuf[0:POOL_TAIL, :] = pbuf[T:T + POOL_TAIL, :]
        lbuf[0:CONV_TAIL, :] = lbuf[T:T + CONV_TAIL, :]

    def make_ffn():
        acts = {}
        ffn_parts = []
        pending = _ffn_sequence()

        def ffn(n):
            for _ in range(n):
                op, c = pending.pop(0)
                if op == "u":
                    cols = slice(c * FF_CHUNK, (c + 1) * FF_CHUNK)
                    n2 = n2buf[prev]
                    half_g = _dot(n2, wg_ref[:, cols])
                    up = _dot(n2, wu_ref[:, cols])
                    acts[c] = (half_g * (jnp.tanh(half_g) + 1.0) * up).astype(jnp.bfloat16)
                else:
                    group = range(c, min(c + FF_DOWN_GROUP, N_FF_CHUNKS))
                    act = jnp.concatenate([acts.pop(k) for k in group], axis=-1)
                    ffn_parts.append(_dot(act, wd_ref[c * FF_CHUNK:(group[-1] + 1) * FF_CHUNK, :]))

        def finish():
            ffn(len(pending))
            ffn_out = ffn_parts[0]
            for part in ffn_parts[1:]:
                ffn_out = ffn_out + part
            o_ref[...] = _rms(h1buf[prev] + ffn_out, lnf_ref[...])

        return ffn, finish

    @pl.when(i == 0)
    def _():
        _load_weight_bf16(w_in_hbm, w_in_ref, stage, stage_sems)
        _build_small_weights(pool_w_ref, w_a_ref, w_i_ref, small_w_ref)
        later = _cast_weights_stepwise(((w_out_hbm, w_out_ref, None), (wg_hbm, wg_ref, 0.5),
                                        (wu_hbm, wu_ref, None), (wd_hbm, wd_ref, None)), stage, stage_sems)
        next(later)
        for k, n in enumerate(mixer()):
            for _ in range(PREP_CHUNKS_PER_FFN_SLOT * n if k else 0):
                next(later, None)
        for _ in later:
            pass

    @pl.when(jnp.logical_and(i > 0, i < n_chunks))
    def _():
        ffn, finish = make_ffn()
        for n in mixer():
            ffn(n)
        finish()

    @pl.when(i == n_chunks)
    def _():
        _, finish = make_ffn()
        finish()


def kernel(x, ln1_g, w_in, pool_w, pool_scale, conv_w, conv_b, w_a, b_a, w_i, b_i, lam, gn_pool_g,
           gn_lru_g, w_out, ln2_g, w_ffn_gate, w_ffn_up, w_ffn_down, lnf_g):
    B, S, D = x.shape
    assert D == D_MODEL and S % SEQ_TILE == 0 and ln1_g.shape[0] == 1
    bf16 = jnp.bfloat16
    vmem = pl.BlockSpec(memory_space=pltpu.VMEM)
    hbm = pl.BlockSpec(memory_space=pl.ANY)
    nq = LRU_WIDTH // LANES
    scan_rows = SUBLANES * SUB_PITCH
    cps = S // SEQ_TILE
    n_chunks = B * cps

    def in_index(i):
        c = jnp.minimum(i, n_chunks - 1)
        return (c // cps, c % cps, 0)

    def out_index(i):
        c = jnp.maximum(i - 1, 0)
        return (c // cps, c % cps, 0)

    out = pl.pallas_call(
        functools.partial(_layer_kernel, chunks_per_seq=cps, n_chunks=n_chunks),
        grid=(n_chunks + 1,),
        in_specs=[pl.BlockSpec((None, SEQ_TILE, D), in_index)] + [
            hbm if k in HBM_WEIGHT_OPERANDS else vmem for k in range(1, 20)],
        out_specs=pl.BlockSpec((None, SEQ_TILE, D), out_index),
        out_shape=jax.ShapeDtypeStruct((B, S, D), jnp.float32),
        scratch_shapes=[
            pltpu.VMEM((POOL_TAIL + SEQ_TILE, POOL_WIDTH), jnp.float32),
            pltpu.VMEM((CONV_TAIL + SEQ_TILE, LRU_WIDTH), jnp.float32),
            pltpu.VMEM((nq, SEQ_TILE, LANES), jnp.float32),
            pltpu.VMEM((nq, SEQ_TILE, LANES), jnp.float32),
            pltpu.VMEM((nq, scan_rows, LANES), jnp.float32),
            pltpu.VMEM((1, LRU_WIDTH), jnp.float32),
            pltpu.VMEM((2, SEQ_TILE, D), jnp.float32),
            pltpu.VMEM((2, SEQ_TILE, D), jnp.bfloat16),
            pltpu.VMEM((D, IN_WIDTH), bf16),
            pltpu.VMEM((D, D), bf16),
            pltpu.VMEM((D, D_FF), bf16),
            pltpu.VMEM((D, D_FF), bf16),
            pltpu.VMEM((D_FF, D), bf16),
            pltpu.VMEM((2, LRU_WIDTH // 2, LRU_WIDTH + LRU_WIDTH // 2), bf16),
            pltpu.VMEM((WEIGHT_STAGE_SLOTS, WEIGHT_STAGE_ROWS, D_FF), jnp.float32),
            pltpu.SemaphoreType.DMA((WEIGHT_STAGE_SLOTS,)),
        ],
        compiler_params=pltpu.CompilerParams(
            dimension_semantics=("arbitrary",),
            vmem_limit_bytes=VMEM_LIMIT_BYTES),
        name="hymba_layer",
    )(x, ln1_g, w_in[0], pool_w, pool_scale, conv_w, conv_b, w_a, b_a, w_i, b_i, lam, gn_pool_g, gn_lru_g,
      w_out[0], ln2_g, w_ffn_gate[0], w_ffn_up[0], w_ffn_down[0], lnf_g.reshape(1, D))
    return out
```

```python
import functools
import math

import jax
import jax.numpy as jnp
from jax import lax
from jax.experimental import pallas as pl
from jax.experimental.pallas import tpu as pltpu

D_MODEL = 1024
POOL_WINDOWS = (2, 4, 8, 16)
POOL_GROUP = 128
POOL_WIDTH = 512
LRU_WIDTH = 512
LRU_HEADS = 8
LRU_HEAD_DIM = 64
CONV_WIDTH = 4
LRU_C = 8.0
D_FF = 2816
IN_WIDTH = POOL_WIDTH + 2 * LRU_WIDTH
EPS = 1e-6

LANES = 128
SUBLANES = 8
SEQ_TILE = 512
POOL_TAIL = 16
CONV_TAIL = 8
SUB_LEN = SEQ_TILE // SUBLANES
SUB_PITCH = SUB_LEN + SUBLANES
FF_CHUNK = 256
N_FF_CHUNKS = D_FF // FF_CHUNK
FF_DOWN_GROUP = 6
FFN_PIECES_AT = (1, 1, 1, 2, 2, 2, 2, 2)
HBM_WEIGHT_OPERANDS = (2, 14, 16, 17, 18)
WEIGHT_STAGE_ROWS = 256
WEIGHT_STAGE_SLOTS = 3
VMEM_LIMIT_BYTES = 58 * 1024 * 1024


def _ffn_sequence():
    ops = []
    for c in range(N_FF_CHUNKS):
        ops.append(("u", c))
        if c % FF_DOWN_GROUP == 0 and c > 0:
            ops.append(("d", c - FF_DOWN_GROUP))
    last = (N_FF_CHUNKS - 1) // FF_DOWN_GROUP * FF_DOWN_GROUP
    return ops + [("d", last)]


def _rms(x, g):
    ms = jnp.mean(x * x, axis=-1, keepdims=True)
    return x * lax.rsqrt(ms + EPS) * g


def _sigmoid(x):
    return 0.5 * (jnp.tanh(0.5 * x) + 1.0)


def _gelu_tanh_times(x, h):
    c = math.sqrt(2.0 / math.pi)
    t = jnp.tanh(x * ((x * x) * (0.044715 * c) + c))
    return (h * x) * (0.5 * t + 0.5)


def _dot(a, b):
    return jnp.dot(a, b, preferred_element_type=jnp.float32)


def _load_weight_bf16(src_hbm, dst, stage, sems, scale=None):
    rows, cols = src_hbm.shape
    n = rows // WEIGHT_STAGE_ROWS

    def copy(r, slot):
        src = src_hbm.at[pl.ds(r * WEIGHT_STAGE_ROWS, WEIGHT_STAGE_ROWS), :]
        return pltpu.make_async_copy(src, stage.at[slot, :, pl.ds(0, cols)], sems.at[slot])

    ahead = WEIGHT_STAGE_SLOTS - 1
    for r in range(min(ahead, n)):
        copy(r, r % WEIGHT_STAGE_SLOTS).start()

    def body(r, carry):
        slot = lax.rem(r, WEIGHT_STAGE_SLOTS)

        @pl.when(r + ahead < n)
        def _():
            copy(r + ahead, lax.rem(r + ahead, WEIGHT_STAGE_SLOTS)).start()

        copy(r, slot).wait()
        row0 = pl.multiple_of(r * WEIGHT_STAGE_ROWS, WEIGHT_STAGE_ROWS)
        w = stage[slot, :, 0:cols]
        dst[pl.ds(row0, WEIGHT_STAGE_ROWS), :] = (w if scale is None else w * scale).astype(jnp.bfloat16)
        return carry

    lax.fori_loop(0, n, body, 0)


def _build_small_weights(pool_w_ref, w_a_ref, w_i_ref, small_w_ref):
    half = LRU_WIDTH // 2
    heads = half // LRU_HEAD_DIM
    bf16 = jnp.bfloat16
    c_idx = lax.broadcasted_iota(jnp.int32, (LRU_HEAD_DIM, half), 0)
    j_idx = lax.broadcasted_iota(jnp.int32, (LRU_HEAD_DIM, half), 1)
    spread = (lax.rem(j_idx, LRU_HEAD_DIM) == c_idx).astype(bf16)
    col_head = j_idx // LRU_HEAD_DIM
    for q in range(2):
        for part, w_ref in enumerate((w_a_ref, w_i_ref)):
            for h in range(heads):
                rep = _dot(w_ref[0, heads * q + h].astype(bf16), spread)
                blk = jnp.where(col_head == h, rep, 0.0).astype(bf16)
                small_w_ref[q, h * LRU_HEAD_DIM:(h + 1) * LRU_HEAD_DIM, part * half:(part + 1) * half] = blk
        small_w_ref[q, :, 2 * half:3 * half] = jnp.zeros((half, half), bf16)
        for g in range(2):
            rows = slice(g * POOL_GROUP, (g + 1) * POOL_GROUP)
            cols = slice(2 * half + g * POOL_GROUP, 2 * half + (g + 1) * POOL_GROUP)
            small_w_ref[q, rows, cols] = pool_w_ref[0, 2 * q + g].astype(bf16)


def _layer_kernel(x_ref, ln1_ref, w_in_hbm, pool_w_ref, pool_scale_ref, conv_w_ref, conv_b_ref,
                  w_a_ref, b_a_ref, w_i_ref, b_i_ref, lam_ref, gn_pool_ref, gn_lru_ref, w_out_hbm,
                  ln2_ref, wg_hbm, wu_hbm, wd_hbm, lnf_ref, o_ref,
                  pbuf, lbuf, abuf, bbuf, hbuf, hcarry, h1buf, n2buf,
                  w_in_ref, w_out_ref, wg_ref, wu_ref, wd_ref, small_w_ref, stage, stage_sems,
                  *, chunks_per_seq, n_chunks):
    i = pl.program_id(0)
    s = lax.rem(i, chunks_per_seq)
    T = SEQ_TILE
    half = LRU_WIDTH // 2

    @pl.when(i == 0)
    def _():
        for src, dst, scale in ((w_in_hbm, w_in_ref, None), (w_out_hbm, w_out_ref, None),
                                (wg_hbm, wg_ref, 0.5), (wu_hbm, wu_ref, None), (wd_hbm, wd_ref, None)):
            _load_weight_bf16(src, dst, stage, stage_sems, scale)
        _build_small_weights(pool_w_ref, w_a_ref, w_i_ref, small_w_ref)

    slot = lax.rem(i, 2)
    prev = 1 - slot

    def mixer():
        @pl.when(s == 0)
        def _():
            pbuf[0:POOL_TAIL, :] = jnp.zeros((POOL_TAIL, POOL_WIDTH), jnp.float32)
            lbuf[0:CONV_TAIL, :] = jnp.zeros((CONV_TAIL, LRU_WIDTH), jnp.float32)
            hcarry[...] = jnp.zeros_like(hcarry)

        yield FFN_PIECES_AT[0]
        x = x_ref[...]
        nb = _rms(x, ln1_ref[...]).astype(jnp.bfloat16)

        lbuf[CONV_TAIL:CONV_TAIL + T, :] = _dot(nb, w_in_ref[:, POOL_WIDTH:POOL_WIDTH + LRU_WIDTH])
        u_gate = _dot(nb, w_in_ref[:, POOL_WIDTH + LRU_WIDTH:])

        yield FFN_PIECES_AT[2]
        assert CONV_WIDTH == 4
        u_ext = lbuf[...]
        u_prev = pltpu.roll(u_ext, 1, axis=0)
        tap = lambda k: conv_w_ref[0, k:k + 1, :]
        older = tap(1) * u_ext + tap(0) * u_prev
        xc = (conv_b_ref[...] + tap(3) * u_ext + tap(2) * u_prev + pltpu.roll(older, 2, axis=0))[CONV_TAIL:]
        xcb = xc.astype(jnp.bfloat16)
        z = [_dot(xcb[:, q * half:(q + 1) * half], small_w_ref[q, :, 0:2 * half])
             for q in range(2)]

        yield FFN_PIECES_AT[3]
        za = jnp.concatenate([z[0][:, :half], z[1][:, :half]], axis=-1) + b_a_ref[...]
        zi = jnp.concatenate([z[0][:, half:], z[1][:, half:]], axis=-1) + b_i_ref[...]
        gate_i = _sigmoid(zi)
        neg_lam = -lam_ref[...]
        softplus = jnp.maximum(neg_lam, 0.0) + jnp.log1p(jnp.exp(-jnp.abs(neg_lam)))
        k_row = (0.5 * LRU_C) * softplus
        neg_log_a = jnp.tanh(0.5 * za) * k_row + k_row
        a = jnp.exp(-neg_log_a)
        one_minus_a2 = jnp.tanh(neg_log_a) * (a * a + 1.0)
        mult = jnp.where(one_minus_a2 > 0.0, one_minus_a2 * lax.rsqrt(one_minus_a2), 0.0)
        bterm = mult * (gate_i * xc)

        nq = LRU_WIDTH // LANES
        tiles_per_sub = SUB_LEN // SUBLANES
        for q in range(nq):
            lanes = slice(q * LANES, (q + 1) * LANES)
            for t in range(T // SUBLANES):
                j, m0 = divmod(t, tiles_per_sub)
                dst = pl.ds(m0 * SUBLANES * SUBLANES + j, SUBLANES, stride=SUBLANES)
                rows = slice(t * SUBLANES, (t + 1) * SUBLANES)
                abuf[q, dst, :] = a[rows, lanes]
                bbuf[q, dst, :] = bterm[rows, lanes]

        def tile(m):
            return slice(m * SUBLANES, (m + 1) * SUBLANES)

        yield FFN_PIECES_AT[4]
        prod = [jnp.ones((SUBLANES, LANES), jnp.float32) for _ in range(nq)]
        loc = [jnp.zeros((SUBLANES, LANES), jnp.float32) for _ in range(nq)]
        for m in range(SUB_LEN):
            for q in range(nq):
                am = abuf[q, tile(m), :]
                loc[q] = am * loc[q] + bbuf[q, tile(m), :]
                prod[q] = am * prod[q]
        row = lax.broadcasted_iota(jnp.int32, (SUBLANES, LANES), 0)
        h_in = []
        for q in range(nq):
            carry = hcarry[:, q * LANES:(q + 1) * LANES]
            start = jnp.zeros((SUBLANES, LANES), jnp.float32)
            for j in range(SUBLANES):
                start = jnp.where(row == j, carry, start)
                carry = prod[q][j:j + 1, :] * carry + loc[q][j:j + 1, :]
            h_in.append(start)
            hcarry[:, q * LANES:(q + 1) * LANES] = carry
        yield FFN_PIECES_AT[5]
        for m in range(SUB_LEN):
            for q in range(nq):
                h_in[q] = abuf[q, tile(m), :] * h_in[q] + bbuf[q, tile(m), :]
                hbuf[q, pl.ds(m, SUBLANES, stride=SUB_PITCH), :] = h_in[q]
        h_lru = jnp.concatenate(
            [jnp.concatenate([hbuf[q, j * SUB_PITCH:j * SUB_PITCH + SUB_LEN, :] for j in range(SUBLANES)], axis=0)
             for q in range(nq)], axis=-1)

        pbuf[POOL_TAIL:POOL_TAIL + T, :] = _dot(nb, w_in_ref[:, 0:POOL_WIDTH])
        yield FFN_PIECES_AT[1]
        n_groups = len(POOL_WINDOWS)
        level = pbuf[...]
        sums = []
        for g, w in enumerate(POOL_WINDOWS):
            assert w == 2 << g and w - 1 <= POOL_TAIL
            rest = level[:, (0 if g == 0 else POOL_GROUP):]
            rest = rest + pltpu.roll(rest, w // 2, axis=0)
            sums.append(rest[POOL_TAIL:, 0:POOL_GROUP])
            level = rest
        head_pos = (s * T + 1 + lax.broadcasted_iota(jnp.int32, (POOL_TAIL, POOL_GROUP), 0)).astype(jnp.float32)
        d_parts = []
        for g, w in enumerate(POOL_WINDOWS):
            ug = pbuf[POOL_TAIL:POOL_TAIL + T, g * POOL_GROUP:(g + 1) * POOL_GROUP]
            head = sums[g][0:POOL_TAIL] / jnp.minimum(head_pos, float(w))
            mean = jnp.concatenate([head, sums[g][POOL_TAIL:] * (1.0 / w)], axis=0)
            d_parts.append((mean - ug).astype(jnp.bfloat16))
        y_parts = []
        for p in range(n_groups // 2):
            d_pair = jnp.concatenate(d_parts[2 * p:2 * p + 2], axis=-1)
            y_parts.append(_dot(d_pair, small_w_ref[p, :, 2 * half:3 * half]))
        y_pool = jnp.concatenate(y_parts, axis=-1) * pool_scale_ref[...]
        mix_pool = _rms(y_pool, gn_pool_ref[...]).astype(jnp.bfloat16)

        yield FFN_PIECES_AT[6]
        y_lru = _gelu_tanh_times(u_gate, h_lru)
        mix_lru = _rms(y_lru, gn_lru_ref[...]).astype(jnp.bfloat16)

        mix = jnp.concatenate([mix_pool, mix_lru], axis=-1)
        h1_new = x + _dot(mix, w_out_ref[...])
        yield FFN_PIECES_AT[7]
        h1buf[slot] = h1_new
        n2buf[slot] = _rms(h1_new, ln2_ref[...]).astype(jnp.bfloat16)

        pbuf[0:POOL_TAIL, :] = pbuf[T:T + POOL_TAIL, :]
        lbuf[0:CONV_TAIL, :] = lbuf[T:T + CONV_TAIL, :]

    def make_ffn():
        acts = {}
        ffn_parts = []
        pending = _ffn_sequence()

        def ffn(n):
            for _ in range(n):
                op, c = pending.pop(0)
                if op == "u":
                    cols = slice(c * FF_CHUNK, (c + 1) * FF_CHUNK)
                    n2 = n2buf[prev]
                    half_g = _dot(n2, wg_ref[:, cols])
                    up = _dot(n2, wu_ref[:, cols])
                    acts[c] = (half_g * (jnp.tanh(half_g) + 1.0) * up).astype(jnp.bfloat16)
                else:
                    group = range(c, min(c + FF_DOWN_GROUP, N_FF_CHUNKS))
                    act = jnp.concatenate([acts.pop(k) for k in group], axis=-1)
                    ffn_parts.append(_dot(act, wd_ref[c * FF_CHUNK:(group[-1] + 1) * FF_CHUNK, :]))

        def finish():
            ffn(len(pending))
            ffn_out = ffn_parts[0]
            for part in ffn_parts[1:]:
                ffn_out = ffn_out + part
            o_ref[...] = _rms(h1buf[prev] + ffn_out, lnf_ref[...])

        return ffn, finish

    @pl.when(i == 0)
    def _():
        for _ in mixer():
            pass

    @pl.when(jnp.logical_and(i > 0, i < n_chunks))
    def _():
        ffn, finish = make_ffn()
        for n in mixer():
            ffn(n)
        finish()

    @pl.when(i == n_chunks)
    def _():
        _, finish = make_ffn()
        finish()


def kernel(x, ln1_g, w_in, pool_w, pool_scale, conv_w, conv_b, w_a, b_a, w_i, b_i, lam, gn_pool_g,
           gn_lru_g, w_out, ln2_g, w_ffn_gate, w_ffn_up, w_ffn_down, lnf_g):
    B, S, D = x.shape
    assert D == D_MODEL and S % SEQ_TILE == 0 and ln1_g.shape[0] == 1
    bf16 = jnp.bfloat16
    vmem = pl.BlockSpec(memory_space=pltpu.VMEM)
    hbm = pl.BlockSpec(memory_space=pl.ANY)
    nq = LRU_WIDTH // LANES
    scan_rows = SUBLANES * SUB_PITCH
    cps = S // SEQ_TILE
    n_chunks = B * cps

    def in_index(i):
        c = jnp.minimum(i, n_chunks - 1)
        return (c // cps, c % cps, 0)

    def out_index(i):
        c = jnp.maximum(i - 1, 0)
        return (c // cps, c % cps, 0)

    out = pl.pallas_call(
        functools.partial(_layer_kernel, chunks_per_seq=cps, n_chunks=n_chunks),
        grid=(n_chunks + 1,),
        in_specs=[pl.BlockSpec((None, SEQ_TILE, D), in_index)] + [
            hbm if k in HBM_WEIGHT_OPERANDS else vmem for k in range(1, 20)],
        out_specs=pl.BlockSpec((None, SEQ_TILE, D), out_index),
        out_shape=jax.ShapeDtypeStruct((B, S, D), jnp.float32),
        scratch_shapes=[
            pltpu.VMEM((POOL_TAIL + SEQ_TILE, POOL_WIDTH), jnp.float32),
            pltpu.VMEM((CONV_TAIL + SEQ_TILE, LRU_WIDTH), jnp.float32),
            pltpu.VMEM((nq, SEQ_TILE, LANES), jnp.float32),
            pltpu.VMEM((nq, SEQ_TILE, LANES), jnp.float32),
            pltpu.VMEM((nq, scan_rows, LANES), jnp.float32),
            pltpu.VMEM((1, LRU_WIDTH), jnp.float32),
            pltpu.VMEM((2, SEQ_TILE, D), jnp.float32),
            pltpu.VMEM((2, SEQ_TILE, D), jnp.bfloat16),
            pltpu.VMEM((D, IN_WIDTH), bf16),
            pltpu.VMEM((D, D), bf16),
            pltpu.VMEM((D, D_FF), bf16),
            pltpu.VMEM((D, D_FF), bf16),
            pltpu.VMEM((D_FF, D), bf16),
            pltpu.VMEM((2, LRU_WIDTH // 2, LRU_WIDTH + LRU_WIDTH // 2), bf16),
            pltpu.VMEM((WEIGHT_STAGE_SLOTS, WEIGHT_STAGE_ROWS, D_FF), jnp.float32),
            pltpu.SemaphoreType.DMA((WEIGHT_STAGE_SLOTS,)),
        ],
        compiler_params=pltpu.CompilerParams(
            dimension_semantics=("arbitrary",),
            vmem_limit_bytes=VMEM_LIMIT_BYTES),
        name="hymba_layer",
    )(x, ln1_g, w_in[0], pool_w, pool_scale, conv_w, conv_b, w_a, b_a, w_i, b_i, lam, gn_pool_g, gn_lru_g,
      w_out[0], ln2_g, w_ffn_gate[0], w_ffn_up[0], w_ffn_down[0], lnf_g.reshape(1, D))
    return out
```

```python
import functools
import math

import jax
import jax.numpy as jnp
from jax import lax
from jax.experimental import pallas as pl
from jax.experimental.pallas import tpu as pltpu

D_MODEL = 1024
POOL_WINDOWS = (2, 4, 8, 16)
POOL_GROUP = 128
POOL_WIDTH = 512
LRU_WIDTH = 512
LRU_HEADS = 8
LRU_HEAD_DIM = 64
CONV_WIDTH = 4
LRU_C = 8.0
D_FF = 2816
IN_WIDTH = POOL_WIDTH + 2 * LRU_WIDTH
EPS = 1e-6

LANES = 128
SUBLANES = 8
SEQ_TILE = 512
POOL_TAIL = 16
CONV_TAIL = 8
SUB_LEN = SEQ_TILE // SUBLANES
SUB_PITCH = SUB_LEN + SUBLANES
FF_CHUNK = 256
N_FF_CHUNKS = D_FF // FF_CHUNK
FF_DOWN_GROUP = 6
GATE_BLOCKS = 4
FFN_PIECES_AT = (1, 1, 1, (1, 1, 1, 1), 1, 1, 2, 2)
HBM_WEIGHT_OPERANDS = (2, 14, 16, 17, 18)
WEIGHT_STAGE_ROWS = 256
WEIGHT_STAGE_SLOTS = 3
VMEM_LIMIT_BYTES = 58 * 1024 * 1024


def _ffn_sequence():
    ops = []
    for c in range(N_FF_CHUNKS):
        ops.append(("u", c))
        if c % FF_DOWN_GROUP == 0 and c > 0:
            ops.append(("d", c - FF_DOWN_GROUP))
    last = (N_FF_CHUNKS - 1) // FF_DOWN_GROUP * FF_DOWN_GROUP
    return ops + [("d", last)]


def _rms(x, g):
    ms = jnp.mean(x * x, axis=-1, keepdims=True)
    return x * lax.rsqrt(ms + EPS) * g


def _sigmoid(x):
    return 0.5 * (jnp.tanh(0.5 * x) + 1.0)


def _gelu_tanh_times(x, h):
    c = math.sqrt(2.0 / math.pi)
    t = jnp.tanh(x * ((x * x) * (0.044715 * c) + c))
    return (h * x) * (0.5 * t + 0.5)


def _dot(a, b):
    return jnp.dot(a, b, preferred_element_type=jnp.float32)


def _load_weight_bf16(src_hbm, dst, stage, sems, scale=None):
    rows, cols = src_hbm.shape
    n = rows // WEIGHT_STAGE_ROWS

    def copy(r, slot):
        src = src_hbm.at[pl.ds(r * WEIGHT_STAGE_ROWS, WEIGHT_STAGE_ROWS), :]
        return pltpu.make_async_copy(src, stage.at[slot, :, pl.ds(0, cols)], sems.at[slot])

    ahead = WEIGHT_STAGE_SLOTS - 1
    for r in range(min(ahead, n)):
        copy(r, r % WEIGHT_STAGE_SLOTS).start()

    def body(r, carry):
        slot = lax.rem(r, WEIGHT_STAGE_SLOTS)

        @pl.when(r + ahead < n)
        def _():
            copy(r + ahead, lax.rem(r + ahead, WEIGHT_STAGE_SLOTS)).start()

        copy(r, slot).wait()
        row0 = pl.multiple_of(r * WEIGHT_STAGE_ROWS, WEIGHT_STAGE_ROWS)
        w = stage[slot, :, 0:cols]
        dst[pl.ds(row0, WEIGHT_STAGE_ROWS), :] = (w if scale is None else w * scale).astype(jnp.bfloat16)
        return carry

    lax.fori_loop(0, n, body, 0)


def _build_small_weights(pool_w_ref, w_a_ref, w_i_ref, small_w_ref):
    half = LRU_WIDTH // 2
    heads = half // LRU_HEAD_DIM
    bf16 = jnp.bfloat16
    c_idx = lax.broadcasted_iota(jnp.int32, (LRU_HEAD_DIM, half), 0)
    j_idx = lax.broadcasted_iota(jnp.int32, (LRU_HEAD_DIM, half), 1)
    spread = (lax.rem(j_idx, LRU_HEAD_DIM) == c_idx).astype(bf16)
    col_head = j_idx // LRU_HEAD_DIM
    for q in range(2):
        for part, w_ref in enumerate((w_a_ref, w_i_ref)):
            for h in range(heads):
                rep = _dot(w_ref[0, heads * q + h].astype(bf16), spread)
                blk = jnp.where(col_head == h, rep, 0.0).astype(bf16)
                small_w_ref[q, h * LRU_HEAD_DIM:(h + 1) * LRU_HEAD_DIM, part * half:(part + 1) * half] = blk
        small_w_ref[q, :, 2 * half:3 * half] = jnp.zeros((half, half), bf16)
        for g in range(2):
            rows = slice(g * POOL_GROUP, (g + 1) * POOL_GROUP)
            cols = slice(2 * half + g * POOL_GROUP, 2 * half + (g + 1) * POOL_GROUP)
            small_w_ref[q, rows, cols] = pool_w_ref[0, 2 * q + g].astype(bf16)


def _layer_kernel(x_ref, ln1_ref, w_in_hbm, pool_w_ref, pool_scale_ref, conv_w_ref, conv_b_ref,
                  w_a_ref, b_a_ref, w_i_ref, b_i_ref, lam_ref, gn_pool_ref, gn_lru_ref, w_out_hbm,
                  ln2_ref, wg_hbm, wu_hbm, wd_hbm, lnf_ref, o_ref,
                  pbuf, lbuf, abuf, bbuf, hbuf, hcarry, h1buf, n2buf,
                  w_in_ref, w_out_ref, wg_ref, wu_ref, wd_ref, small_w_ref, stage, stage_sems,
                  *, chunks_per_seq, n_chunks):
    i = pl.program_id(0)
    s = lax.rem(i, chunks_per_seq)
    T = SEQ_TILE
    half = LRU_WIDTH // 2

    @pl.when(i == 0)
    def _():
        for src, dst, scale in ((w_in_hbm, w_in_ref, None), (w_out_hbm, w_out_ref, None),
                                (wg_hbm, wg_ref, 0.5), (wu_hbm, wu_ref, None), (wd_hbm, wd_ref, None)):
            _load_weight_bf16(src, dst, stage, stage_sems, scale)
        _build_small_weights(pool_w_ref, w_a_ref, w_i_ref, small_w_ref)

    slot = lax.rem(i, 2)
    prev = 1 - slot

    def mixer():
        @pl.when(s == 0)
        def _():
            pbuf[0:POOL_TAIL, :] = jnp.zeros((POOL_TAIL, POOL_WIDTH), jnp.float32)
            lbuf[0:CONV_TAIL, :] = jnp.zeros((CONV_TAIL, LRU_WIDTH), jnp.float32)
            hcarry[...] = jnp.zeros_like(hcarry)

        yield FFN_PIECES_AT[0]
        x = x_ref[...]
        nb = _rms(x, ln1_ref[...]).astype(jnp.bfloat16)

        lbuf[CONV_TAIL:CONV_TAIL + T, :] = _dot(nb, w_in_ref[:, POOL_WIDTH:POOL_WIDTH + LRU_WIDTH])
        u_gate = _dot(nb, w_in_ref[:, POOL_WIDTH + LRU_WIDTH:])

        yield FFN_PIECES_AT[2]
        assert CONV_WIDTH == 4
        u_ext = lbuf[...]
        u_prev = pltpu.roll(u_ext, 1, axis=0)
        tap = lambda k: conv_w_ref[0, k:k + 1, :]
        older = tap(1) * u_ext + tap(0) * u_prev
        xc = (conv_b_ref[...] + tap(3) * u_ext + tap(2) * u_prev + pltpu.roll(older, 2, axis=0))[CONV_TAIL:]
        xcb = xc.astype(jnp.bfloat16)
        neg_lam = -lam_ref[...]
        softplus = jnp.maximum(neg_lam, 0.0) + jnp.log1p(jnp.exp(-jnp.abs(neg_lam)))
        k_row = (0.5 * LRU_C) * softplus

        nq = LRU_WIDTH // LANES
        tiles_per_sub = SUB_LEN // SUBLANES
        blk_rows = T // GATE_BLOCKS
        for blk in range(GATE_BLOCKS):
            rows_blk = slice(blk * blk_rows, (blk + 1) * blk_rows)
            z = [_dot(xcb[rows_blk, q * half:(q + 1) * half], small_w_ref[q, :, 0:2 * half])
                 for q in range(2)]
            yield FFN_PIECES_AT[3][blk]
            za = jnp.concatenate([z[0][:, :half], z[1][:, :half]], axis=-1) + b_a_ref[...]
            zi = jnp.concatenate([z[0][:, half:], z[1][:, half:]], axis=-1) + b_i_ref[...]
            neg_log_a = jnp.tanh(0.5 * za) * k_row + k_row
            a = jnp.exp(-neg_log_a)
            one_minus_a2 = jnp.tanh(neg_log_a) * (a * a + 1.0)
            mult = jnp.where(one_minus_a2 > 0.0, one_minus_a2 * lax.rsqrt(one_minus_a2), 0.0)
            bterm = mult * (_sigmoid(zi) * xc[rows_blk, :])
            for q in range(nq):
                lanes = slice(q * LANES, (q + 1) * LANES)
                for t in range(blk_rows // SUBLANES):
                    j, m0 = divmod(blk * (blk_rows // SUBLANES) + t, tiles_per_sub)
                    dst = pl.ds(m0 * SUBLANES * SUBLANES + j, SUBLANES, stride=SUBLANES)
                    rows = slice(t * SUBLANES, (t + 1) * SUBLANES)
                    abuf[q, dst, :] = a[rows, lanes]
                    bbuf[q, dst, :] = bterm[rows, lanes]

        def tile(m):
            return slice(m * SUBLANES, (m + 1) * SUBLANES)

        yield FFN_PIECES_AT[4]
        prod = [jnp.ones((SUBLANES, LANES), jnp.float32) for _ in range(nq)]
        loc = [jnp.zeros((SUBLANES, LANES), jnp.float32) for _ in range(nq)]
        for m in range(SUB_LEN):
            for q in range(nq):
                am = abuf[q, tile(m), :]
                loc[q] = am * loc[q] + bbuf[q, tile(m), :]
                prod[q] = am * prod[q]
        row = lax.broadcasted_iota(jnp.int32, (SUBLANES, LANES), 0)
        h_in = []
        for q in range(nq):
            carry = hcarry[:, q * LANES:(q + 1) * LANES]
            start = jnp.zeros((SUBLANES, LANES), jnp.float32)
            for j in range(SUBLANES):
                start = jnp.where(row == j, carry, start)
                carry = prod[q][j:j + 1, :] * carry + loc[q][j:j + 1, :]
            h_in.append(start)
            hcarry[:, q * LANES:(q + 1) * LANES] = carry
        yield FFN_PIECES_AT[5]
        for m in range(SUB_LEN):
            for q in range(nq):
                h_in[q] = abuf[q, tile(m), :] * h_in[q] + bbuf[q, tile(m), :]
                hbuf[q, pl.ds(m, SUBLANES, stride=SUB_PITCH), :] = h_in[q]
        h_lru = jnp.concatenate(
            [jnp.concatenate([hbuf[q, j * SUB_PITCH:j * SUB_PITCH + SUB_LEN, :] for j in range(SUBLANES)], axis=0)
             for q in range(nq)], axis=-1)

        pbuf[POOL_TAIL:POOL_TAIL + T, :] = _dot(nb, w_in_ref[:, 0:POOL_WIDTH])
        yield FFN_PIECES_AT[1]
        n_groups = len(POOL_WINDOWS)
        level = pbuf[...]
        sums = []
        for g, w in enumerate(POOL_WINDOWS):
            assert w == 2 << g and w - 1 <= POOL_TAIL
            rest = level[:, (0 if g == 0 else POOL_GROUP):]
            rest = rest + pltpu.roll(rest, w // 2, axis=0)
            sums.append(rest[POOL_TAIL:, 0:POOL_GROUP])
            level = rest
        head_pos = (s * T + 1 + lax.broadcasted_iota(jnp.int32, (POOL_TAIL, POOL_GROUP), 0)).astype(jnp.float32)
        d_parts = []
        for g, w in enumerate(POOL_WINDOWS):
            ug = pbuf[POOL_TAIL:POOL_TAIL + T, g * POOL_GROUP:(g + 1) * POOL_GROUP]
            head = sums[g][0:POOL_TAIL] / jnp.minimum(head_pos, float(w))
            mean = jnp.concatenate([head, sums[g][POOL_TAIL:] * (1.0 / w)], axis=0)
            d_parts.append((mean - ug).astype(jnp.bfloat16))
        y_parts = []
        for p in range(n_groups // 2):
            d_pair = jnp.concatenate(d_parts[2 * p:2 * p + 2], axis=-1)
            y_parts.append(_dot(d_pair, small_w_ref[p, :, 2 * half:3 * half]))
        y_pool = jnp.concatenate(y_parts, axis=-1) * pool_scale_ref[...]
        mix_pool = _rms(y_pool, gn_pool_ref[...]).astype(jnp.bfloat16)

        yield FFN_PIECES_AT[6]
        y_lru = _gelu_tanh_times(u_gate, h_lru)
        mix_lru = _rms(y_lru, gn_lru_ref[...]).astype(jnp.bfloat16)

        mix = jnp.concatenate([mix_pool, mix_lru], axis=-1)
        h1_new = x + _dot(mix, w_out_ref[...])
        yield FFN_PIECES_AT[7]
        h1buf[slot] = h1_new
        n2buf[slot] = _rms(h1_new, ln2_ref[...]).astype(jnp.bfloat16)

        pbuf[0:POOL_TAIL, :] = pbuf[T:T + POOL_TAIL, :]
        lbuf[0:CONV_TAIL, :] = lbuf[T:T + CONV_TAIL, :]

    def make_ffn():
        acts = {}
        ffn_parts = []
        pending = _ffn_sequence()

        def ffn(n):
            for _ in range(n):
                op, c = pending.pop(0)
                if op == "u":
                    cols = slice(c * FF_CHUNK, (c + 1) * FF_CHUNK)
                    n2 = n2buf[prev]
                    half_g = _dot(n2, wg_ref[:, cols])
                    up = _dot(n2, wu_ref[:, cols])
                    acts[c] = (half_g * (jnp.tanh(half_g) + 1.0) * up).astype(jnp.bfloat16)
                else:
                    group = range(c, min(c + FF_DOWN_GROUP, N_FF_CHUNKS))
                    act = jnp.concatenate([acts.pop(k) for k in group], axis=-1)
                    ffn_parts.append(_dot(act, wd_ref[c * FF_CHUNK:(group[-1] + 1) * FF_CHUNK, :]))

        def finish():
            ffn(len(pending))
            ffn_out = ffn_parts[0]
            for part in ffn_parts[1:]:
                ffn_out = ffn_out + part
            o_ref[...] = _rms(h1buf[prev] + ffn_out, lnf_ref[...])

        return ffn, finish

    @pl.when(i == 0)
    def _():
        for _ in mixer():
            pass

    @pl.when(jnp.logical_and(i > 0, i < n_chunks))
    def _():
        ffn, finish = make_ffn()
        for n in mixer():
            ffn(n)
        finish()

    @pl.when(i == n_chunks)
    def _():
        _, finish = make_ffn()
        finish()


def kernel(x, ln1_g, w_in, pool_w, pool_scale, conv_w, conv_b, w_a, b_a, w_i, b_i, lam, gn_pool_g,
           gn_lru_g, w_out, ln2_g, w_ffn_gate, w_ffn_up, w_ffn_down, lnf_g):
    B, S, D = x.shape
    assert D == D_MODEL and S % SEQ_TILE == 0 and ln1_g.shape[0] == 1
    bf16 = jnp.bfloat16
    vmem = pl.BlockSpec(memory_space=pltpu.VMEM)
    hbm = pl.BlockSpec(memory_space=pl.ANY)
    nq = LRU_WIDTH // LANES
    scan_rows = SUBLANES * SUB_PITCH
    cps = S // SEQ_TILE
    n_chunks = B * cps

    def in_index(i):
        c = jnp.minimum(i, n_chunks - 1)
        return (c // cps, c % cps, 0)

    def out_index(i):
        c = jnp.maximum(i - 1, 0)
        return (c // cps, c % cps, 0)

    out = pl.pallas_call(
        functools.partial(_layer_kernel, chunks_per_seq=cps, n_chunks=n_chunks),
        grid=(n_chunks + 1,),
        in_specs=[pl.BlockSpec((None, SEQ_TILE, D), in_index)] + [
            hbm if k in HBM_WEIGHT_OPERANDS else vmem for k in range(1, 20)],
        out_specs=pl.BlockSpec((None, SEQ_TILE, D), out_index),
        out_shape=jax.ShapeDtypeStruct((B, S, D), jnp.float32),
        scratch_shapes=[
            pltpu.VMEM((POOL_TAIL + SEQ_TILE, POOL_WIDTH), jnp.float32),
            pltpu.VMEM((CONV_TAIL + SEQ_TILE, LRU_WIDTH), jnp.float32),
            pltpu.VMEM((nq, SEQ_TILE, LANES), jnp.float32),
            pltpu.VMEM((nq, SEQ_TILE, LANES), jnp.float32),
            pltpu.VMEM((nq, scan_rows, LANES), jnp.float32),
            pltpu.VMEM((1, LRU_WIDTH), jnp.float32),
            pltpu.VMEM((2, SEQ_TILE, D), jnp.float32),
            pltpu.VMEM((2, SEQ_TILE, D), jnp.bfloat16),
            pltpu.VMEM((D, IN_WIDTH), bf16),
            pltpu.VMEM((D, D), bf16),
            pltpu.VMEM((D, D_FF), bf16),
            pltpu.VMEM((D, D_FF), bf16),
            pltpu.VMEM((D_FF, D), bf16),
            pltpu.VMEM((2, LRU_WIDTH // 2, LRU_WIDTH + LRU_WIDTH // 2), bf16),
            pltpu.VMEM((WEIGHT_STAGE_SLOTS, WEIGHT_STAGE_ROWS, D_FF), jnp.float32),
            pltpu.SemaphoreType.DMA((WEIGHT_STAGE_SLOTS,)),
        ],
        compiler_params=pltpu.CompilerParams(
            dimension_semantics=("arbitrary",),
            vmem_limit_bytes=VMEM_LIMIT_BYTES),
        name="hymba_layer",
    )(x, ln1_g, w_in[0], pool_w, pool_scale, conv_w, conv_b, w_a, b_a, w_i, b_i, lam, gn_pool_g, gn_lru_g,
      w_out[0], ln2_g, w_ffn_gate[0], w_ffn_up[0], w_ffn_down[0], lnf_g.reshape(1, D))
    return out
```

```python
import functools
import math

import jax
import jax.numpy as jnp
from jax import lax
from jax.experimental import pallas as pl
from jax.experimental.pallas import tpu as pltpu

D_MODEL = 1024
POOL_WINDOWS = (2, 4, 8, 16)
POOL_GROUP = 128
POOL_WIDTH = 512
LRU_WIDTH = 512
LRU_HEADS = 8
LRU_HEAD_DIM = 64
CONV_WIDTH = 4
LRU_C = 8.0
D_FF = 2816
IN_WIDTH = POOL_WIDTH + 2 * LRU_WIDTH
EPS = 1e-6

LANES = 128
SUBLANES = 8
SEQ_TILE = 512
POOL_TAIL = 16
CONV_TAIL = 8
SUB_LEN = SEQ_TILE // SUBLANES
SUB_PITCH = SUB_LEN + SUBLANES
FF_CHUNK = 512
N_FF_CHUNKS = -(-D_FF // FF_CHUNK)
FF_DOWN_GROUP = 3
GATE_BLOCKS = 1
FFN_PIECES_AT = (1, 1, 1, (1,), 1, 1, 1, 1)
HBM_WEIGHT_OPERANDS = (2, 14, 16, 17, 18)
WEIGHT_STAGE_ROWS = 256
WEIGHT_STAGE_SLOTS = 3
VMEM_LIMIT_BYTES = 58 * 1024 * 1024


def _ffn_sequence():
    ops = []
    for c in range(N_FF_CHUNKS):
        ops.append(("u", c))
        if c % FF_DOWN_GROUP == 0 and c > 0:
            ops.append(("d", c - FF_DOWN_GROUP))
    last = (N_FF_CHUNKS - 1) // FF_DOWN_GROUP * FF_DOWN_GROUP
    return ops + [("d", last)]


def _rms(x, g):
    ms = jnp.mean(x * x, axis=-1, keepdims=True)
    return x * lax.rsqrt(ms + EPS) * g


def _sigmoid(x):
    return 0.5 * (jnp.tanh(0.5 * x) + 1.0)


def _gelu_tanh_times(x, h):
    c = math.sqrt(2.0 / math.pi)
    t = jnp.tanh(x * ((x * x) * (0.044715 * c) + c))
    return (h * x) * (0.5 * t + 0.5)


def _dot(a, b):
    return jnp.dot(a, b, preferred_element_type=jnp.float32)


def _load_weight_bf16(src_hbm, dst, stage, sems, scale=None):
    rows, cols = src_hbm.shape
    n = rows // WEIGHT_STAGE_ROWS

    def copy(r, slot):
        src = src_hbm.at[pl.ds(r * WEIGHT_STAGE_ROWS, WEIGHT_STAGE_ROWS), :]
        return pltpu.make_async_copy(src, stage.at[slot, :, pl.ds(0, cols)], sems.at[slot])

    ahead = WEIGHT_STAGE_SLOTS - 1
    for r in range(min(ahead, n)):
        copy(r, r % WEIGHT_STAGE_SLOTS).start()

    def body(r, carry):
        slot = lax.rem(r, WEIGHT_STAGE_SLOTS)

        @pl.when(r + ahead < n)
        def _():
            copy(r + ahead, lax.rem(r + ahead, WEIGHT_STAGE_SLOTS)).start()

        copy(r, slot).wait()
        row0 = pl.multiple_of(r * WEIGHT_STAGE_ROWS, WEIGHT_STAGE_ROWS)
        w = stage[slot, :, 0:cols]
        dst[pl.ds(row0, WEIGHT_STAGE_ROWS), :] = (w if scale is None else w * scale).astype(jnp.bfloat16)
        return carry

    lax.fori_loop(0, n, body, 0)


def _build_small_weights(pool_w_ref, w_a_ref, w_i_ref, small_w_ref):
    half = LRU_WIDTH // 2
    heads = half // LRU_HEAD_DIM
    bf16 = jnp.bfloat16
    c_idx = lax.broadcasted_iota(jnp.int32, (LRU_HEAD_DIM, half), 0)
    j_idx = lax.broadcasted_iota(jnp.int32, (LRU_HEAD_DIM, half), 1)
    spread = (lax.rem(j_idx, LRU_HEAD_DIM) == c_idx).astype(bf16)
    col_head = j_idx // LRU_HEAD_DIM
    for q in range(2):
        for part, w_ref in enumerate((w_a_ref, w_i_ref)):
            for h in range(heads):
                rep = _dot(w_ref[0, heads * q + h].astype(bf16), spread)
                blk = jnp.where(col_head == h, rep, 0.0).astype(bf16)
                small_w_ref[q, h * LRU_HEAD_DIM:(h + 1) * LRU_HEAD_DIM, part * half:(part + 1) * half] = blk
        small_w_ref[q, :, 2 * half:3 * half] = jnp.zeros((half, half), bf16)
        for g in range(2):
            rows = slice(g * POOL_GROUP, (g + 1) * POOL_GROUP)
            cols = slice(2 * half + g * POOL_GROUP, 2 * half + (g + 1) * POOL_GROUP)
            small_w_ref[q, rows, cols] = pool_w_ref[0, 2 * q + g].astype(bf16)


def _layer_kernel(x_ref, ln1_ref, w_in_hbm, pool_w_ref, pool_scale_ref, conv_w_ref, conv_b_ref,
                  w_a_ref, b_a_ref, w_i_ref, b_i_ref, lam_ref, gn_pool_ref, gn_lru_ref, w_out_hbm,
                  ln2_ref, wg_hbm, wu_hbm, wd_hbm, lnf_ref, o_ref,
                  pbuf, lbuf, abuf, bbuf, hbuf, hcarry, h1buf, n2buf,
                  w_in_ref, w_out_ref, wg_ref, wu_ref, wd_ref, small_w_ref, stage, stage_sems,
                  *, chunks_per_seq, n_chunks):
    i = pl.program_id(0)
    s = lax.rem(i, chunks_per_seq)
    T = SEQ_TILE
    half = LRU_WIDTH // 2

    @pl.when(i == 0)
    def _():
        for src, dst, scale in ((w_in_hbm, w_in_ref, None), (w_out_hbm, w_out_ref, None),
                                (wg_hbm, wg_ref, 0.5), (wu_hbm, wu_ref, None), (wd_hbm, wd_ref, None)):
            _load_weight_bf16(src, dst, stage, stage_sems, scale)
        _build_small_weights(pool_w_ref, w_a_ref, w_i_ref, small_w_ref)

    slot = lax.rem(i, 2)
    prev = 1 - slot

    def mixer():
        @pl.when(s == 0)
        def _():
            pbuf[0:POOL_TAIL, :] = jnp.zeros((POOL_TAIL, POOL_WIDTH), jnp.float32)
            lbuf[0:CONV_TAIL, :] = jnp.zeros((CONV_TAIL, LRU_WIDTH), jnp.float32)
            hcarry[...] = jnp.zeros_like(hcarry)

        yield FFN_PIECES_AT[0]
        x = x_ref[...]
        nb = _rms(x, ln1_ref[...]).astype(jnp.bfloat16)

        lbuf[CONV_TAIL:CONV_TAIL + T, :] = _dot(nb, w_in_ref[:, POOL_WIDTH:POOL_WIDTH + LRU_WIDTH])
        u_gate = _dot(nb, w_in_ref[:, POOL_WIDTH + LRU_WIDTH:])

        yield FFN_PIECES_AT[2]
        assert CONV_WIDTH == 4
        u_ext = lbuf[...]
        u_prev = pltpu.roll(u_ext, 1, axis=0)
        tap = lambda k: conv_w_ref[0, k:k + 1, :]
        older = tap(1) * u_ext + tap(0) * u_prev
        xc = (conv_b_ref[...] + tap(3) * u_ext + tap(2) * u_prev + pltpu.roll(older, 2, axis=0))[CONV_TAIL:]
        xcb = xc.astype(jnp.bfloat16)
        neg_lam = -lam_ref[...]
        softplus = jnp.maximum(neg_lam, 0.0) + jnp.log1p(jnp.exp(-jnp.abs(neg_lam)))
        k_row = (0.5 * LRU_C) * softplus

        nq = LRU_WIDTH // LANES
        tiles_per_sub = SUB_LEN // SUBLANES
        blk_rows = T // GATE_BLOCKS
        for blk in range(GATE_BLOCKS):
            rows_blk = slice(blk * blk_rows, (blk + 1) * blk_rows)
            z = [_dot(xcb[rows_blk, q * half:(q + 1) * half], small_w_ref[q, :, 0:2 * half])
                 for q in range(2)]
            yield FFN_PIECES_AT[3][blk]
            za = jnp.concatenate([z[0][:, :half], z[1][:, :half]], axis=-1) + b_a_ref[...]
            zi = jnp.concatenate([z[0][:, half:], z[1][:, half:]], axis=-1) + b_i_ref[...]
            neg_log_a = jnp.tanh(0.5 * za) * k_row + k_row
            a = jnp.exp(-neg_log_a)
            one_minus_a2 = jnp.tanh(neg_log_a) * (a * a + 1.0)
            mult = jnp.where(one_minus_a2 > 0.0, one_minus_a2 * lax.rsqrt(one_minus_a2), 0.0)
            bterm = mult * (_sigmoid(zi) * xc[rows_blk, :])
            for q in range(nq):
                lanes = slice(q * LANES, (q + 1) * LANES)
                for t in range(blk_rows // SUBLANES):
                    j, m0 = divmod(blk * (blk_rows // SUBLANES) + t, tiles_per_sub)
                    dst = pl.ds(m0 * SUBLANES * SUBLANES + j, SUBLANES, stride=SUBLANES)
                    rows = slice(t * SUBLANES, (t + 1) * SUBLANES)
                    abuf[q, dst, :] = a[rows, lanes]
                    bbuf[q, dst, :] = bterm[rows, lanes]

        def tile(m):
            return slice(m * SUBLANES, (m + 1) * SUBLANES)

        yield FFN_PIECES_AT[4]
        prod = [jnp.ones((SUBLANES, LANES), jnp.float32) for _ in range(nq)]
        loc = [jnp.zeros((SUBLANES, LANES), jnp.float32) for _ in range(nq)]
        for m in range(SUB_LEN):
            for q in range(nq):
                am = abuf[q, tile(m), :]
                loc[q] = am * loc[q] + bbuf[q, tile(m), :]
                prod[q] = am * prod[q]
        row = lax.broadcasted_iota(jnp.int32, (SUBLANES, LANES), 0)
        h_in = []
        for q in range(nq):
            carry = hcarry[:, q * LANES:(q + 1) * LANES]
            start = jnp.zeros((SUBLANES, LANES), jnp.float32)
            for j in range(SUBLANES):
                start = jnp.where(row == j, carry, start)
                carry = prod[q][j:j + 1, :] * carry + loc[q][j:j + 1, :]
            h_in.append(start)
            hcarry[:, q * LANES:(q + 1) * LANES] = carry
        yield FFN_PIECES_AT[5]
        for m in range(SUB_LEN):
            for q in range(nq):
                h_in[q] = abuf[q, tile(m), :] * h_in[q] + bbuf[q, tile(m), :]
                hbuf[q, pl.ds(m, SUBLANES, stride=SUB_PITCH), :] = h_in[q]
        h_lru = jnp.concatenate(
            [jnp.concatenate([hbuf[q, j * SUB_PITCH:j * SUB_PITCH + SUB_LEN, :] for j in range(SUBLANES)], axis=0)
             for q in range(nq)], axis=-1)

        pbuf[POOL_TAIL:POOL_TAIL + T, :] = _dot(nb, w_in_ref[:, 0:POOL_WIDTH])
        yield FFN_PIECES_AT[1]
        n_groups = len(POOL_WINDOWS)
        level = pbuf[...]
        sums = []
        for g, w in enumerate(POOL_WINDOWS):
            assert w == 2 << g and w - 1 <= POOL_TAIL
            rest = level[:, (0 if g == 0 else POOL_GROUP):]
            rest = rest + pltpu.roll(rest, w // 2, axis=0)
            sums.append(rest[POOL_TAIL:, 0:POOL_GROUP])
            level = rest
        head_pos = (s * T + 1 + lax.broadcasted_iota(jnp.int32, (POOL_TAIL, POOL_GROUP), 0)).astype(jnp.float32)
        d_parts = []
        for g, w in enumerate(POOL_WINDOWS):
            ug = pbuf[POOL_TAIL:POOL_TAIL + T, g * POOL_GROUP:(g + 1) * POOL_GROUP]
            head = sums[g][0:POOL_TAIL] / jnp.minimum(head_pos, float(w))
            mean = jnp.concatenate([head, sums[g][POOL_TAIL:] * (1.0 / w)], axis=0)
            d_parts.append((mean - ug).astype(jnp.bfloat16))
        y_parts = []
        for p in range(n_groups // 2):
            d_pair = jnp.concatenate(d_parts[2 * p:2 * p + 2], axis=-1)
            y_parts.append(_dot(d_pair, small_w_ref[p, :, 2 * half:3 * half]))
        y_pool = jnp.concatenate(y_parts, axis=-1) * pool_scale_ref[...]
        mix_pool = _rms(y_pool, gn_pool_ref[...]).astype(jnp.bfloat16)

        yield FFN_PIECES_AT[6]
        y_lru = _gelu_tanh_times(u_gate, h_lru)
        mix_lru = _rms(y_lru, gn_lru_ref[...]).astype(jnp.bfloat16)

        mix = jnp.concatenate([mix_pool, mix_lru], axis=-1)
        h1_new = x + _dot(mix, w_out_ref[...])
        yield FFN_PIECES_AT[7]
        h1buf[slot] = h1_new
        n2buf[slot] = _rms(h1_new, ln2_ref[...]).astype(jnp.bfloat16)

        pbuf[0:POOL_TAIL, :] = pbuf[T:T + POOL_TAIL, :]
        lbuf[0:CONV_TAIL, :] = lbuf[T:T + CONV_TAIL, :]

    def make_ffn():
        acts = {}
        ffn_parts = []
        pending = _ffn_sequence()

        def ffn(n):
            for _ in range(n):
                op, c = pending.pop(0)
                if op == "u":
                    cols = slice(c * FF_CHUNK, min((c + 1) * FF_CHUNK, D_FF))
                    n2 = n2buf[prev]
                    half_g = _dot(n2, wg_ref[:, cols])
                    up = _dot(n2, wu_ref[:, cols])
                    acts[c] = (half_g * (jnp.tanh(half_g) + 1.0) * up).astype(jnp.bfloat16)
                else:
                    group = range(c, min(c + FF_DOWN_GROUP, N_FF_CHUNKS))
                    act = jnp.concatenate([acts.pop(k) for k in group], axis=-1)
                    ffn_parts.append(_dot(act, wd_ref[c * FF_CHUNK:min((group[-1] + 1) * FF_CHUNK, D_FF), :]))

        def finish():
            ffn(len(pending))
            ffn_out = ffn_parts[0]
            for part in ffn_parts[1:]:
                ffn_out = ffn_out + part
            o_ref[...] = _rms(h1buf[prev] + ffn_out, lnf_ref[...])

        return ffn, finish

    @pl.when(i == 0)
    def _():
        for _ in mixer():
            pass

    @pl.when(jnp.logical_and(i > 0, i < n_chunks))
    def _():
        ffn, finish = make_ffn()
        for n in mixer():
            ffn(n)
        finish()

    @pl.when(i == n_chunks)
    def _():
        _, finish = make_ffn()
        finish()


def kernel(x, ln1_g, w_in, pool_w, pool_scale, conv_w, conv_b, w_a, b_a, w_i, b_i, lam, gn_pool_g,
           gn_lru_g, w_out, ln2_g, w_ffn_gate, w_ffn_up, w_ffn_down, lnf_g):
    B, S, D = x.shape
    assert D == D_MODEL and S % SEQ_TILE == 0 and ln1_g.shape[0] == 1
    bf16 = jnp.bfloat16
    vmem = pl.BlockSpec(memory_space=pltpu.VMEM)
    hbm = pl.BlockSpec(memory_space=pl.ANY)
    nq = LRU_WIDTH // LANES
    scan_rows = SUBLANES * SUB_PITCH
    cps = S // SEQ_TILE
    n_chunks = B * cps

    def in_index(i):
        c = jnp.minimum(i, n_chunks - 1)
        return (c // cps, c % cps, 0)

    def out_index(i):
        c = jnp.maximum(i - 1, 0)
        return (c // cps, c % cps, 0)

    out = pl.pallas_call(
        functools.partial(_layer_kernel, chunks_per_seq=cps, n_chunks=n_chunks),
        grid=(n_chunks + 1,),
        in_specs=[pl.BlockSpec((None, SEQ_TILE, D), in_index)] + [
            hbm if k in HBM_WEIGHT_OPERANDS else vmem for k in range(1, 20)],
        out_specs=pl.BlockSpec((None, SEQ_TILE, D), out_index),
        out_shape=jax.ShapeDtypeStruct((B, S, D), jnp.float32),
        scratch_shapes=[
            pltpu.VMEM((POOL_TAIL + SEQ_TILE, POOL_WIDTH), jnp.float32),
            pltpu.VMEM((CONV_TAIL + SEQ_TILE, LRU_WIDTH), jnp.float32),
            pltpu.VMEM((nq, SEQ_TILE, LANES), jnp.float32),
            pltpu.VMEM((nq, SEQ_TILE, LANES), jnp.float32),
            pltpu.VMEM((nq, scan_rows, LANES), jnp.float32),
            pltpu.VMEM((1, LRU_WIDTH), jnp.float32),
            pltpu.VMEM((2, SEQ_TILE, D), jnp.float32),
            pltpu.VMEM((2, SEQ_TILE, D), jnp.bfloat16),
            pltpu.VMEM((D, IN_WIDTH), bf16),
            pltpu.VMEM((D, D), bf16),
            pltpu.VMEM((D, D_FF), bf16),
            pltpu.VMEM((D, D_FF), bf16),
            pltpu.VMEM((D_FF, D), bf16),
            pltpu.VMEM((2, LRU_WIDTH // 2, LRU_WIDTH + LRU_WIDTH // 2), bf16),
            pltpu.VMEM((WEIGHT_STAGE_SLOTS, WEIGHT_STAGE_ROWS, D_FF), jnp.float32),
            pltpu.SemaphoreType.DMA((WEIGHT_STAGE_SLOTS,)),
        ],
        compiler_params=pltpu.CompilerParams(
            dimension_semantics=("arbitrary",),
            vmem_limit_bytes=VMEM_LIMIT_BYTES),
        name="hymba_layer",
    )(x, ln1_g, w_in[0], pool_w, pool_scale, conv_w, conv_b, w_a, b_a, w_i, b_i, lam, gn_pool_g, gn_lru_g,
      w_out[0], ln2_g, w_ffn_gate[0], w_ffn_up[0], w_ffn_down[0], lnf_g.reshape(1, D))
    return out
```

```python
import functools
import math

import jax
import jax.numpy as jnp
from jax import lax
from jax.experimental import pallas as pl
from jax.experimental.pallas import tpu as pltpu

D_MODEL = 1024
POOL_WINDOWS = (2, 4, 8, 16)
POOL_GROUP = 128
POOL_WIDTH = 512
LRU_WIDTH = 512
LRU_HEADS = 8
LRU_HEAD_DIM = 64
CONV_WIDTH = 4
LRU_C = 8.0
D_FF = 2816
IN_WIDTH = POOL_WIDTH + 2 * LRU_WIDTH
EPS = 1e-6

LANES = 128
SUBLANES = 8
SEQ_TILE = 512
POOL_TAIL = 16
CONV_TAIL = 8
SUB_LEN = SEQ_TILE // SUBLANES
SUB_PITCH = SUB_LEN + SUBLANES
FF_CHUNK = 1024
N_FF_CHUNKS = -(-D_FF // FF_CHUNK)
FF_DOWN_GROUP = 3
GATE_BLOCKS = 1
FFN_PIECES_AT = (1, 0, 1, (1,), 0, 0, 0, 1)
HBM_WEIGHT_OPERANDS = (2, 14, 16, 17, 18)
WEIGHT_STAGE_ROWS = 256
WEIGHT_STAGE_SLOTS = 3
VMEM_LIMIT_BYTES = 58 * 1024 * 1024


def _ffn_sequence():
    ops = []
    for c in range(N_FF_CHUNKS):
        ops.append(("u", c))
        if c % FF_DOWN_GROUP == 0 and c > 0:
            ops.append(("d", c - FF_DOWN_GROUP))
    last = (N_FF_CHUNKS - 1) // FF_DOWN_GROUP * FF_DOWN_GROUP
    return ops + [("d", last)]


def _rms(x, g):
    ms = jnp.mean(x * x, axis=-1, keepdims=True)
    return x * lax.rsqrt(ms + EPS) * g


def _sigmoid(x):
    return 0.5 * (jnp.tanh(0.5 * x) + 1.0)


def _gelu_tanh_times(x, h):
    c = math.sqrt(2.0 / math.pi)
    t = jnp.tanh(x * ((x * x) * (0.044715 * c) + c))
    return (h * x) * (0.5 * t + 0.5)


def _dot(a, b):
    return jnp.dot(a, b, preferred_element_type=jnp.float32)


def _load_weight_bf16(src_hbm, dst, stage, sems, scale=None):
    rows, cols = src_hbm.shape
    n = rows // WEIGHT_STAGE_ROWS

    def copy(r, slot):
        src = src_hbm.at[pl.ds(r * WEIGHT_STAGE_ROWS, WEIGHT_STAGE_ROWS), :]
        return pltpu.make_async_copy(src, stage.at[slot, :, pl.ds(0, cols)], sems.at[slot])

    ahead = WEIGHT_STAGE_SLOTS - 1
    for r in range(min(ahead, n)):
        copy(r, r % WEIGHT_STAGE_SLOTS).start()

    def body(r, carry):
        slot = lax.rem(r, WEIGHT_STAGE_SLOTS)

        @pl.when(r + ahead < n)
        def _():
            copy(r + ahead, lax.rem(r + ahead, WEIGHT_STAGE_SLOTS)).start()

        copy(r, slot).wait()
        row0 = pl.multiple_of(r * WEIGHT_STAGE_ROWS, WEIGHT_STAGE_ROWS)
        w = stage[slot, :, 0:cols]
        dst[pl.ds(row0, WEIGHT_STAGE_ROWS), :] = (w if scale is None else w * scale).astype(jnp.bfloat16)
        return carry

    lax.fori_loop(0, n, body, 0)


def _build_small_weights(pool_w_ref, w_a_ref, w_i_ref, small_w_ref):
    half = LRU_WIDTH // 2
    heads = half // LRU_HEAD_DIM
    bf16 = jnp.bfloat16
    c_idx = lax.broadcasted_iota(jnp.int32, (LRU_HEAD_DIM, half), 0)
    j_idx = lax.broadcasted_iota(jnp.int32, (LRU_HEAD_DIM, half), 1)
    spread = (lax.rem(j_idx, LRU_HEAD_DIM) == c_idx).astype(bf16)
    col_head = j_idx // LRU_HEAD_DIM
    for q in range(2):
        for part, w_ref in enumerate((w_a_ref, w_i_ref)):
            for h in range(heads):
                rep = _dot(w_ref[0, heads * q + h].astype(bf16), spread)
                blk = jnp.where(col_head == h, rep, 0.0).astype(bf16)
                small_w_ref[q, h * LRU_HEAD_DIM:(h + 1) * LRU_HEAD_DIM, part * half:(part + 1) * half] = blk
        small_w_ref[q, :, 2 * half:3 * half] = jnp.zeros((half, half), bf16)
        for g in range(2):
            rows = slice(g * POOL_GROUP, (g + 1) * POOL_GROUP)
            cols = slice(2 * half + g * POOL_GROUP, 2 * half + (g + 1) * POOL_GROUP)
            small_w_ref[q, rows, cols] = pool_w_ref[0, 2 * q + g].astype(bf16)


def _layer_kernel(x_ref, ln1_ref, w_in_hbm, pool_w_ref, pool_scale_ref, conv_w_ref, conv_b_ref,
                  w_a_ref, b_a_ref, w_i_ref, b_i_ref, lam_ref, gn_pool_ref, gn_lru_ref, w_out_hbm,
                  ln2_ref, wg_hbm, wu_hbm, wd_hbm, lnf_ref, o_ref,
                  pbuf, lbuf, abuf, bbuf, hbuf, hcarry, h1buf, n2buf,
                  w_in_ref, w_out_ref, wg_ref, wu_ref, wd_ref, small_w_ref, stage, stage_sems,
                  *, chunks_per_seq, n_chunks):
    i = pl.program_id(0)
    s = lax.rem(i, chunks_per_seq)
    T = SEQ_TILE
    half = LRU_WIDTH // 2

    @pl.when(i == 0)
    def _():
        for src, dst, scale in ((w_in_hbm, w_in_ref, None), (w_out_hbm, w_out_ref, None),
                                (wg_hbm, wg_ref, 0.5), (wu_hbm, wu_ref, None), (wd_hbm, wd_ref, None)):
            _load_weight_bf16(src, dst, stage, stage_sems, scale)
        _build_small_weights(pool_w_ref, w_a_ref, w_i_ref, small_w_ref)

    slot = lax.rem(i, 2)
    prev = 1 - slot

    def mixer():
        @pl.when(s == 0)
        def _():
            pbuf[0:POOL_TAIL, :] = jnp.zeros((POOL_TAIL, POOL_WIDTH), jnp.float32)
            lbuf[0:CONV_TAIL, :] = jnp.zeros((CONV_TAIL, LRU_WIDTH), jnp.float32)
            hcarry[...] = jnp.zeros_like(hcarry)

        yield FFN_PIECES_AT[0]
        x = x_ref[...]
        nb = _rms(x, ln1_ref[...]).astype(jnp.bfloat16)

        lbuf[CONV_TAIL:CONV_TAIL + T, :] = _dot(nb, w_in_ref[:, POOL_WIDTH:POOL_WIDTH + LRU_WIDTH])
        u_gate = _dot(nb, w_in_ref[:, POOL_WIDTH + LRU_WIDTH:])

        yield FFN_PIECES_AT[2]
        assert CONV_WIDTH == 4
        u_ext = lbuf[...]
        u_prev = pltpu.roll(u_ext, 1, axis=0)
        tap = lambda k: conv_w_ref[0, k:k + 1, :]
        older = tap(1) * u_ext + tap(0) * u_prev
        xc = (conv_b_ref[...] + tap(3) * u_ext + tap(2) * u_prev + pltpu.roll(older, 2, axis=0))[CONV_TAIL:]
        xcb = xc.astype(jnp.bfloat16)
        neg_lam = -lam_ref[...]
        softplus = jnp.maximum(neg_lam, 0.0) + jnp.log1p(jnp.exp(-jnp.abs(neg_lam)))
        k_row = (0.5 * LRU_C) * softplus

        nq = LRU_WIDTH // LANES
        tiles_per_sub = SUB_LEN // SUBLANES
        blk_rows = T // GATE_BLOCKS
        for blk in range(GATE_BLOCKS):
            rows_blk = slice(blk * blk_rows, (blk + 1) * blk_rows)
            z = [_dot(xcb[rows_blk, q * half:(q + 1) * half], small_w_ref[q, :, 0:2 * half])
                 for q in range(2)]
            yield FFN_PIECES_AT[3][blk]
            za = jnp.concatenate([z[0][:, :half], z[1][:, :half]], axis=-1) + b_a_ref[...]
            zi = jnp.concatenate([z[0][:, half:], z[1][:, half:]], axis=-1) + b_i_ref[...]
            neg_log_a = jnp.tanh(0.5 * za) * k_row + k_row
            a = jnp.exp(-neg_log_a)
            one_minus_a2 = jnp.tanh(neg_log_a) * (a * a + 1.0)
            mult = jnp.where(one_minus_a2 > 0.0, one_minus_a2 * lax.rsqrt(one_minus_a2), 0.0)
            bterm = mult * (_sigmoid(zi) * xc[rows_blk, :])
            for q in range(nq):
                lanes = slice(q * LANES, (q + 1) * LANES)
                for t in range(blk_rows // SUBLANES):
                    j, m0 = divmod(blk * (blk_rows // SUBLANES) + t, tiles_per_sub)
                    dst = pl.ds(m0 * SUBLANES * SUBLANES + j, SUBLANES, stride=SUBLANES)
                    rows = slice(t * SUBLANES, (t + 1) * SUBLANES)
                    abuf[q, dst, :] = a[rows, lanes]
                    bbuf[q, dst, :] = bterm[rows, lanes]

        def tile(m):
            return slice(m * SUBLANES, (m + 1) * SUBLANES)

        yield FFN_PIECES_AT[4]
        prod = [jnp.ones((SUBLANES, LANES), jnp.float32) for _ in range(nq)]
        loc = [jnp.zeros((SUBLANES, LANES), jnp.float32) for _ in range(nq)]
        for m in range(SUB_LEN):
            for q in range(nq):
                am = abuf[q, tile(m), :]
                loc[q] = am * loc[q] + bbuf[q, tile(m), :]
                prod[q] = am * prod[q]
        row = lax.broadcasted_iota(jnp.int32, (SUBLANES, LANES), 0)
        h_in = []
        for q in range(nq):
            carry = hcarry[:, q * LANES:(q + 1) * LANES]
            start = jnp.zeros((SUBLANES, LANES), jnp.float32)
            for j in range(SUBLANES):
                start = jnp.where(row == j, carry, start)
                carry = prod[q][j:j + 1, :] * carry + loc[q][j:j + 1, :]
            h_in.append(start)
            hcarry[:, q * LANES:(q + 1) * LANES] = carry
        yield FFN_PIECES_AT[5]
        for m in range(SUB_LEN):
            for q in range(nq):
                h_in[q] = abuf[q, tile(m), :] * h_in[q] + bbuf[q, tile(m), :]
                hbuf[q, pl.ds(m, SUBLANES, stride=SUB_PITCH), :] = h_in[q]
        h_lru = jnp.concatenate(
            [jnp.concatenate([hbuf[q, j * SUB_PITCH:j * SUB_PITCH + SUB_LEN, :] for j in range(SUBLANES)], axis=0)
             for q in range(nq)], axis=-1)

        pbuf[POOL_TAIL:POOL_TAIL + T, :] = _dot(nb, w_in_ref[:, 0:POOL_WIDTH])
        yield FFN_PIECES_AT[1]
        n_groups = len(POOL_WINDOWS)
        level = pbuf[...]
        sums = []
        for g, w in enumerate(POOL_WINDOWS):
            assert w == 2 << g and w - 1 <= POOL_TAIL
            rest = level[:, (0 if g == 0 else POOL_GROUP):]
            rest = rest + pltpu.roll(rest, w // 2, axis=0)
            sums.append(rest[POOL_TAIL:, 0:POOL_GROUP])
            level = rest
        head_pos = (s * T + 1 + lax.broadcasted_iota(jnp.int32, (POOL_TAIL, POOL_GROUP), 0)).astype(jnp.float32)
        d_parts = []
        for g, w in enumerate(POOL_WINDOWS):
            ug = pbuf[POOL_TAIL:POOL_TAIL + T, g * POOL_GROUP:(g + 1) * POOL_GROUP]
            head = sums[g][0:POOL_TAIL] / jnp.minimum(head_pos, float(w))
            mean = jnp.concatenate([head, sums[g][POOL_TAIL:] * (1.0 / w)], axis=0)
            d_parts.append((mean - ug).astype(jnp.bfloat16))
        y_parts = []
        for p in range(n_groups // 2):
            d_pair = jnp.concatenate(d_parts[2 * p:2 * p + 2], axis=-1)
            y_parts.append(_dot(d_pair, small_w_ref[p, :, 2 * half:3 * half]))
        y_pool = jnp.concatenate(y_parts, axis=-1) * pool_scale_ref[...]
        mix_pool = _rms(y_pool, gn_pool_ref[...]).astype(jnp.bfloat16)

        yield FFN_PIECES_AT[6]
        y_lru = _gelu_tanh_times(u_gate, h_lru)
        mix_lru = _rms(y_lru, gn_lru_ref[...]).astype(jnp.bfloat16)

        mix = jnp.concatenate([mix_pool, mix_lru], axis=-1)
        h1_new = x + _dot(mix, w_out_ref[...])
        yield FFN_PIECES_AT[7]
        h1buf[slot] = h1_new
        n2buf[slot] = _rms(h1_new, ln2_ref[...]).astype(jnp.bfloat16)

        pbuf[0:POOL_TAIL, :] = pbuf[T:T + POOL_TAIL, :]
        lbuf[0:CONV_TAIL, :] = lbuf[T:T + CONV_TAIL, :]

    def make_ffn():
        acts = {}
        ffn_parts = []
        pending = _ffn_sequence()

        def ffn(n):
            for _ in range(n):
                op, c = pending.pop(0)
                if op == "u":
                    cols = slice(c * FF_CHUNK, min((c + 1) * FF_CHUNK, D_FF))
                    n2 = n2buf[prev]
                    half_g = _dot(n2, wg_ref[:, cols])
                    up = _dot(n2, wu_ref[:, cols])
                    acts[c] = (half_g * (jnp.tanh(half_g) + 1.0) * up).astype(jnp.bfloat16)
                else:
                    group = range(c, min(c + FF_DOWN_GROUP, N_FF_CHUNKS))
                    act = jnp.concatenate([acts.pop(k) for k in group], axis=-1)
                    ffn_parts.append(_dot(act, wd_ref[c * FF_CHUNK:min((group[-1] + 1) * FF_CHUNK, D_FF), :]))

        def finish():
            ffn(len(pending))
            ffn_out = ffn_parts[0]
            for part in ffn_parts[1:]:
                ffn_out = ffn_out + part
            o_ref[...] = _rms(h1buf[prev] + ffn_out, lnf_ref[...])

        return ffn, finish

    @pl.when(i == 0)
    def _():
        for _ in mixer():
            pass

    @pl.when(jnp.logical_and(i > 0, i < n_chunks))
    def _():
        ffn, finish = make_ffn()
        for n in mixer():
            ffn(n)
        finish()

    @pl.when(i == n_chunks)
    def _():
        _, finish = make_ffn()
        finish()


def kernel(x, ln1_g, w_in, pool_w, pool_scale, conv_w, conv_b, w_a, b_a, w_i, b_i, lam, gn_pool_g,
           gn_lru_g, w_out, ln2_g, w_ffn_gate, w_ffn_up, w_ffn_down, lnf_g):
    B, S, D = x.shape
    assert D == D_MODEL and S % SEQ_TILE == 0 and ln1_g.shape[0] == 1
    bf16 = jnp.bfloat16
    vmem = pl.BlockSpec(memory_space=pltpu.VMEM)
    hbm = pl.BlockSpec(memory_space=pl.ANY)
    nq = LRU_WIDTH // LANES
    scan_rows = SUBLANES * SUB_PITCH
    cps = S // SEQ_TILE
    n_chunks = B * cps

    def in_index(i):
        c = jnp.minimum(i, n_chunks - 1)
        return (c // cps, c % cps, 0)

    def out_index(i):
        c = jnp.maximum(i - 1, 0)
        return (c // cps, c % cps, 0)

    out = pl.pallas_call(
        functools.partial(_layer_kernel, chunks_per_seq=cps, n_chunks=n_chunks),
        grid=(n_chunks + 1,),
        in_specs=[pl.BlockSpec((None, SEQ_TILE, D), in_index)] + [
            hbm if k in HBM_WEIGHT_OPERANDS else vmem for k in range(1, 20)],
        out_specs=pl.BlockSpec((None, SEQ_TILE, D), out_index),
        out_shape=jax.ShapeDtypeStruct((B, S, D), jnp.float32),
        scratch_shapes=[
            pltpu.VMEM((POOL_TAIL + SEQ_TILE, POOL_WIDTH), jnp.float32),
            pltpu.VMEM((CONV_TAIL + SEQ_TILE, LRU_WIDTH), jnp.float32),
            pltpu.VMEM((nq, SEQ_TILE, LANES), jnp.float32),
            pltpu.VMEM((nq, SEQ_TILE, LANES), jnp.float32),
            pltpu.VMEM((nq, scan_rows, LANES), jnp.float32),
            pltpu.VMEM((1, LRU_WIDTH), jnp.float32),
            pltpu.VMEM((2, SEQ_TILE, D), jnp.float32),
            pltpu.VMEM((2, SEQ_TILE, D), jnp.bfloat16),
            pltpu.VMEM((D, IN_WIDTH), bf16),
            pltpu.VMEM((D, D), bf16),
            pltpu.VMEM((D, D_FF), bf16),
            pltpu.VMEM((D, D_FF), bf16),
            pltpu.VMEM((D_FF, D), bf16),
            pltpu.VMEM((2, LRU_WIDTH // 2, LRU_WIDTH + LRU_WIDTH // 2), bf16),
            pltpu.VMEM((WEIGHT_STAGE_SLOTS, WEIGHT_STAGE_ROWS, D_FF), jnp.float32),
            pltpu.SemaphoreType.DMA((WEIGHT_STAGE_SLOTS,)),
        ],
        compiler_params=pltpu.CompilerParams(
            dimension_semantics=("arbitrary",),
            vmem_limit_bytes=VMEM_LIMIT_BYTES),
        name="hymba_layer",
    )(x, ln1_g, w_in[0], pool_w, pool_scale, conv_w, conv_b, w_a, b_a, w_i, b_i, lam, gn_pool_g, gn_lru_g,
      w_out[0], ln2_g, w_ffn_gate[0], w_ffn_up[0], w_ffn_down[0], lnf_g.reshape(1, D))
    return out
```

```python
import functools
import math

import jax
import jax.numpy as jnp
from jax import lax
from jax.experimental import pallas as pl
from jax.experimental.pallas import tpu as pltpu

D_MODEL = 1024
POOL_WINDOWS = (2, 4, 8, 16)
POOL_GROUP = 128
POOL_WIDTH = 512
LRU_WIDTH = 512
LRU_HEADS = 8
LRU_HEAD_DIM = 64
CONV_WIDTH = 4
LRU_C = 8.0
D_FF = 2816
IN_WIDTH = POOL_WIDTH + 2 * LRU_WIDTH
EPS = 1e-6

LANES = 128
SUBLANES = 8
SEQ_TILE = 512
POOL_TAIL = 16
CONV_TAIL = 8
SUB_LEN = SEQ_TILE // SUBLANES
SUB_PITCH = SUB_LEN + SUBLANES
FF_CHUNK = 512
N_FF_CHUNKS = -(-D_FF // FF_CHUNK)
FF_DOWN_GROUP = 3
GATE_BLOCKS = 1
FFN_PIECES_AT = (1, 1, 1, (1,), 1, 1, 1, 1)
HBM_WEIGHT_OPERANDS = (2, 14, 16, 17, 18)
WEIGHT_STAGE_ROWS = 256
WEIGHT_STAGE_SLOTS = 3
VMEM_LIMIT_BYTES = 58 * 1024 * 1024


def _ffn_sequence():
    ops = []
    for c in range(N_FF_CHUNKS):
        ops.append(("u", c))
        if c % FF_DOWN_GROUP == 0 and c > 0:
            ops.append(("d", c - FF_DOWN_GROUP))
    last = (N_FF_CHUNKS - 1) // FF_DOWN_GROUP * FF_DOWN_GROUP
    return ops + [("d", last)]


def _ff_chunk_cols():
    return tuple((c * FF_CHUNK, min((c + 1) * FF_CHUNK, D_FF)) for c in range(N_FF_CHUNKS))


def _rms(x, g):
    ms = jnp.mean(x * x, axis=-1, keepdims=True)
    return x * lax.rsqrt(ms + EPS) * g


def _sigmoid(x):
    return 0.5 * (jnp.tanh(0.5 * x) + 1.0)


def _gelu_tanh_times(x, h):
    c = math.sqrt(2.0 / math.pi)
    t = jnp.tanh(x * ((x * x) * (0.044715 * c) + c))
    return (h * x) * (0.5 * t + 0.5)


def _dot(a, b):
    return jnp.dot(a, b, preferred_element_type=jnp.float32)


def _load_weight_bf16(src_hbm, dst, stage, sems, scale=None, col_map=None):
    rows, cols = src_hbm.shape
    col_map = col_map or ((0, cols, 0),)
    n = rows // WEIGHT_STAGE_ROWS

    def copy(r, slot):
        src = src_hbm.at[pl.ds(r * WEIGHT_STAGE_ROWS, WEIGHT_STAGE_ROWS), :]
        return pltpu.make_async_copy(src, stage.at[slot, :, pl.ds(0, cols)], sems.at[slot])

    ahead = WEIGHT_STAGE_SLOTS - 1
    for r in range(min(ahead, n)):
        copy(r, r % WEIGHT_STAGE_SLOTS).start()

    def body(r, carry):
        slot = lax.rem(r, WEIGHT_STAGE_SLOTS)

        @pl.when(r + ahead < n)
        def _():
            copy(r + ahead, lax.rem(r + ahead, WEIGHT_STAGE_SLOTS)).start()

        copy(r, slot).wait()
        row0 = pl.multiple_of(r * WEIGHT_STAGE_ROWS, WEIGHT_STAGE_ROWS)
        for c0, c1, d0 in col_map:
            w = stage[slot, :, c0:c1]
            dst[pl.ds(row0, WEIGHT_STAGE_ROWS), d0:d0 + c1 - c0] = (
                w if scale is None else w * scale).astype(jnp.bfloat16)
        return carry

    lax.fori_loop(0, n, body, 0)


def _build_small_weights(pool_w_ref, w_a_ref, w_i_ref, small_w_ref):
    half = LRU_WIDTH // 2
    heads = half // LRU_HEAD_DIM
    bf16 = jnp.bfloat16
    c_idx = lax.broadcasted_iota(jnp.int32, (LRU_HEAD_DIM, half), 0)
    j_idx = lax.broadcasted_iota(jnp.int32, (LRU_HEAD_DIM, half), 1)
    spread = (lax.rem(j_idx, LRU_HEAD_DIM) == c_idx).astype(bf16)
    col_head = j_idx // LRU_HEAD_DIM
    for q in range(2):
        for part, w_ref in enumerate((w_a_ref, w_i_ref)):
            for h in range(heads):
                rep = _dot(w_ref[0, heads * q + h].astype(bf16), spread)
                blk = jnp.where(col_head == h, rep, 0.0).astype(bf16)
                small_w_ref[q, h * LRU_HEAD_DIM:(h + 1) * LRU_HEAD_DIM, part * half:(part + 1) * half] = blk
        small_w_ref[q, :, 2 * half:3 * half] = jnp.zeros((half, half), bf16)
        for g in range(2):
            rows = slice(g * POOL_GROUP, (g + 1) * POOL_GROUP)
            cols = slice(2 * half + g * POOL_GROUP, 2 * half + (g + 1) * POOL_GROUP)
            small_w_ref[q, rows, cols] = pool_w_ref[0, 2 * q + g].astype(bf16)


def _layer_kernel(x_ref, ln1_ref, w_in_hbm, pool_w_ref, pool_scale_ref, conv_w_ref, conv_b_ref,
                  w_a_ref, b_a_ref, w_i_ref, b_i_ref, lam_ref, gn_pool_ref, gn_lru_ref, w_out_hbm,
                  ln2_ref, wg_hbm, wu_hbm, wd_hbm, lnf_ref, o_ref,
                  pbuf, lbuf, abuf, bbuf, hbuf, hcarry, h1buf, n2buf,
                  w_in_ref, w_out_ref, wgu_ref, wd_ref, small_w_ref, stage, stage_sems,
                  *, chunks_per_seq, n_chunks):
    i = pl.program_id(0)
    s = lax.rem(i, chunks_per_seq)
    T = SEQ_TILE
    half = LRU_WIDTH // 2

    @pl.when(i == 0)
    def _():
        g_map = tuple((c0, c1, 2 * c0) for c0, c1 in _ff_chunk_cols())
        u_map = tuple((c0, c1, c0 + c1) for c0, c1 in _ff_chunk_cols())
        for src, dst, scale, col_map in ((w_in_hbm, w_in_ref, None, None), (w_out_hbm, w_out_ref, None, None),
                                         (wg_hbm, wgu_ref, 0.5, g_map), (wu_hbm, wgu_ref, None, u_map),
                                         (wd_hbm, wd_ref, None, None)):
            _load_weight_bf16(src, dst, stage, stage_sems, scale, col_map)
        _build_small_weights(pool_w_ref, w_a_ref, w_i_ref, small_w_ref)

    slot = lax.rem(i, 2)
    prev = 1 - slot

    def mixer():
        @pl.when(s == 0)
        def _():
            pbuf[0:POOL_TAIL, :] = jnp.zeros((POOL_TAIL, POOL_WIDTH), jnp.float32)
            lbuf[0:CONV_TAIL, :] = jnp.zeros((CONV_TAIL, LRU_WIDTH), jnp.float32)
            hcarry[...] = jnp.zeros_like(hcarry)

        yield FFN_PIECES_AT[0]
        x = x_ref[...]
        nb = _rms(x, ln1_ref[...]).astype(jnp.bfloat16)

        lbuf[CONV_TAIL:CONV_TAIL + T, :] = _dot(nb, w_in_ref[:, POOL_WIDTH:POOL_WIDTH + LRU_WIDTH])
        u_gate = _dot(nb, w_in_ref[:, POOL_WIDTH + LRU_WIDTH:])

        yield FFN_PIECES_AT[2]
        assert CONV_WIDTH == 4
        u_ext = lbuf[...]
        u_prev = pltpu.roll(u_ext, 1, axis=0)
        tap = lambda k: conv_w_ref[0, k:k + 1, :]
        older = tap(1) * u_ext + tap(0) * u_prev
        xc = (conv_b_ref[...] + tap(3) * u_ext + tap(2) * u_prev + pltpu.roll(older, 2, axis=0))[CONV_TAIL:]
        xcb = xc.astype(jnp.bfloat16)
        neg_lam = -lam_ref[...]
        softplus = jnp.maximum(neg_lam, 0.0) + jnp.log1p(jnp.exp(-jnp.abs(neg_lam)))
        k_row = (0.5 * LRU_C) * softplus

        nq = LRU_WIDTH // LANES
        tiles_per_sub = SUB_LEN // SUBLANES
        blk_rows = T // GATE_BLOCKS
        for blk in range(GATE_BLOCKS):
            rows_blk = slice(blk * blk_rows, (blk + 1) * blk_rows)
            z = [_dot(xcb[rows_blk, q * half:(q + 1) * half], small_w_ref[q, :, 0:2 * half])
                 for q in range(2)]
            yield FFN_PIECES_AT[3][blk]
            za = jnp.concatenate([z[0][:, :half], z[1][:, :half]], axis=-1) + b_a_ref[...]
            zi = jnp.concatenate([z[0][:, half:], z[1][:, half:]], axis=-1) + b_i_ref[...]
            neg_log_a = jnp.tanh(0.5 * za) * k_row + k_row
            a = jnp.exp(-neg_log_a)
            one_minus_a2 = jnp.tanh(neg_log_a) * (a * a + 1.0)
            mult = jnp.where(one_minus_a2 > 0.0, one_minus_a2 * lax.rsqrt(one_minus_a2), 0.0)
            bterm = mult * (_sigmoid(zi) * xc[rows_blk, :])
            for q in range(nq):
                lanes = slice(q * LANES, (q + 1) * LANES)
                for t in range(blk_rows // SUBLANES):
                    j, m0 = divmod(blk * (blk_rows // SUBLANES) + t, tiles_per_sub)
                    dst = pl.ds(m0 * SUBLANES * SUBLANES + j, SUBLANES, stride=SUBLANES)
                    rows = slice(t * SUBLANES, (t + 1) * SUBLANES)
                    abuf[q, dst, :] = a[rows, lanes]
                    bbuf[q, dst, :] = bterm[rows, lanes]

        def tile(m):
            return slice(m * SUBLANES, (m + 1) * SUBLANES)

        yield FFN_PIECES_AT[4]
        prod = [jnp.ones((SUBLANES, LANES), jnp.float32) for _ in range(nq)]
        loc = [jnp.zeros((SUBLANES, LANES), jnp.float32) for _ in range(nq)]
        for m in range(SUB_LEN):
            for q in range(nq):
                am = abuf[q, tile(m), :]
                loc[q] = am * loc[q] + bbuf[q, tile(m), :]
                prod[q] = am * prod[q]
        row = lax.broadcasted_iota(jnp.int32, (SUBLANES, LANES), 0)
        h_in = []
        for q in range(nq):
            carry = hcarry[:, q * LANES:(q + 1) * LANES]
            start = jnp.zeros((SUBLANES, LANES), jnp.float32)
            for j in range(SUBLANES):
                start = jnp.where(row == j, carry, start)
                carry = prod[q][j:j + 1, :] * carry + loc[q][j:j + 1, :]
            h_in.append(start)
            hcarry[:, q * LANES:(q + 1) * LANES] = carry
        yield FFN_PIECES_AT[5]
        for m in range(SUB_LEN):
            for q in range(nq):
                h_in[q] = abuf[q, tile(m), :] * h_in[q] + bbuf[q, tile(m), :]
                hbuf[q, pl.ds(m, SUBLANES, stride=SUB_PITCH), :] = h_in[q]
        h_lru = jnp.concatenate(
            [jnp.concatenate([hbuf[q, j * SUB_PITCH:j * SUB_PITCH + SUB_LEN, :] for j in range(SUBLANES)], axis=0)
             for q in range(nq)], axis=-1)

        pbuf[POOL_TAIL:POOL_TAIL + T, :] = _dot(nb, w_in_ref[:, 0:POOL_WIDTH])
        yield FFN_PIECES_AT[1]
        n_groups = len(POOL_WINDOWS)
        level = pbuf[...]
        sums = []
        for g, w in enumerate(POOL_WINDOWS):
            assert w == 2 << g and w - 1 <= POOL_TAIL
            rest = level[:, (0 if g == 0 else POOL_GROUP):]
            rest = rest + pltpu.roll(rest, w // 2, axis=0)
            sums.append(rest[POOL_TAIL:, 0:POOL_GROUP])
            level = rest
        head_pos = (s * T + 1 + lax.broadcasted_iota(jnp.int32, (POOL_TAIL, POOL_GROUP), 0)).astype(jnp.float32)
        d_parts = []
        for g, w in enumerate(POOL_WINDOWS):
            ug = pbuf[POOL_TAIL:POOL_TAIL + T, g * POOL_GROUP:(g + 1) * POOL_GROUP]
            head = sums[g][0:POOL_TAIL] / jnp.minimum(head_pos, float(w))
            mean = jnp.concatenate([head, sums[g][POOL_TAIL:] * (1.0 / w)], axis=0)
            d_parts.append((mean - ug).astype(jnp.bfloat16))
        y_parts = []
        for p in range(n_groups // 2):
            d_pair = jnp.concatenate(d_parts[2 * p:2 * p + 2], axis=-1)
            y_parts.append(_dot(d_pair, small_w_ref[p, :, 2 * half:3 * half]))
        y_pool = jnp.concatenate(y_parts, axis=-1) * pool_scale_ref[...]
        mix_pool = _rms(y_pool, gn_pool_ref[...]).astype(jnp.bfloat16)

        yield FFN_PIECES_AT[6]
        y_lru = _gelu_tanh_times(u_gate, h_lru)
        mix_lru = _rms(y_lru, gn_lru_ref[...]).astype(jnp.bfloat16)

        mix = jnp.concatenate([mix_pool, mix_lru], axis=-1)
        h1_new = x + _dot(mix, w_out_ref[...])
        yield FFN_PIECES_AT[7]
        h1buf[slot] = h1_new
        n2buf[slot] = _rms(h1_new, ln2_ref[...]).astype(jnp.bfloat16)

        pbuf[0:POOL_TAIL, :] = pbuf[T:T + POOL_TAIL, :]
        lbuf[0:CONV_TAIL, :] = lbuf[T:T + CONV_TAIL, :]

    def make_ffn():
        acts = {}
        ffn_parts = []
        pending = _ffn_sequence()

        def ffn(n):
            for _ in range(n):
                op, c = pending.pop(0)
                if op == "u":
                    c0, c1 = _ff_chunk_cols()[c]
                    n2 = n2buf[prev]
                    gu = _dot(n2, wgu_ref[:, 2 * c0:2 * c1])
                    half_g, up = gu[:, :c1 - c0], gu[:, c1 - c0:]
                    acts[c] = (half_g * (jnp.tanh(half_g) + 1.0) * up).astype(jnp.bfloat16)
                else:
                    group = range(c, min(c + FF_DOWN_GROUP, N_FF_CHUNKS))
                    act = jnp.concatenate([acts.pop(k) for k in group], axis=-1)
                    ffn_parts.append(_dot(act, wd_ref[c * FF_CHUNK:min((group[-1] + 1) * FF_CHUNK, D_FF), :]))

        def finish():
            ffn(len(pending))
            ffn_out = ffn_parts[0]
            for part in ffn_parts[1:]:
                ffn_out = ffn_out + part
            o_ref[...] = _rms(h1buf[prev] + ffn_out, lnf_ref[...])

        return ffn, finish

    @pl.when(i == 0)
    def _():
        for _ in mixer():
            pass

    @pl.when(jnp.logical_and(i > 0, i < n_chunks))
    def _():
        ffn, finish = make_ffn()
        for n in mixer():
            ffn(n)
        finish()

    @pl.when(i == n_chunks)
    def _():
        _, finish = make_ffn()
        finish()


def kernel(x, ln1_g, w_in, pool_w, pool_scale, conv_w, conv_b, w_a, b_a, w_i, b_i, lam, gn_pool_g,
           gn_lru_g, w_out, ln2_g, w_ffn_gate, w_ffn_up, w_ffn_down, lnf_g):
    B, S, D = x.shape
    assert D == D_MODEL and S % SEQ_TILE == 0 and ln1_g.shape[0] == 1
    bf16 = jnp.bfloat16
    vmem = pl.BlockSpec(memory_space=pltpu.VMEM)
    hbm = pl.BlockSpec(memory_space=pl.ANY)
    nq = LRU_WIDTH // LANES
    scan_rows = SUBLANES * SUB_PITCH
    cps = S // SEQ_TILE
    n_chunks = B * cps

    def in_index(i):
        c = jnp.minimum(i, n_chunks - 1)
        return (c // cps, c % cps, 0)

    def out_index(i):
        c = jnp.maximum(i - 1, 0)
        return (c // cps, c % cps, 0)

    out = pl.pallas_call(
        functools.partial(_layer_kernel, chunks_per_seq=cps, n_chunks=n_chunks),
        grid=(n_chunks + 1,),
        in_specs=[pl.BlockSpec((None, SEQ_TILE, D), in_index)] + [
            hbm if k in HBM_WEIGHT_OPERANDS else vmem for k in range(1, 20)],
        out_specs=pl.BlockSpec((None, SEQ_TILE, D), out_index),
        out_shape=jax.ShapeDtypeStruct((B, S, D), jnp.float32),
        scratch_shapes=[
            pltpu.VMEM((POOL_TAIL + SEQ_TILE, POOL_WIDTH), jnp.float32),
            pltpu.VMEM((CONV_TAIL + SEQ_TILE, LRU_WIDTH), jnp.float32),
            pltpu.VMEM((nq, SEQ_TILE, LANES), jnp.float32),
            pltpu.VMEM((nq, SEQ_TILE, LANES), jnp.float32),
            pltpu.VMEM((nq, scan_rows, LANES), jnp.float32),
            pltpu.VMEM((1, LRU_WIDTH), jnp.float32),
            pltpu.VMEM((2, SEQ_TILE, D), jnp.float32),
            pltpu.VMEM((2, SEQ_TILE, D), jnp.bfloat16),
            pltpu.VMEM((D, IN_WIDTH), bf16),
            pltpu.VMEM((D, D), bf16),
            pltpu.VMEM((D, 2 * D_FF), bf16),
            pltpu.VMEM((D_FF, D), bf16),
            pltpu.VMEM((2, LRU_WIDTH // 2, LRU_WIDTH + LRU_WIDTH // 2), bf16),
            pltpu.VMEM((WEIGHT_STAGE_SLOTS, WEIGHT_STAGE_ROWS, D_FF), jnp.float32),
            pltpu.SemaphoreType.DMA((WEIGHT_STAGE_SLOTS,)),
        ],
        compiler_params=pltpu.CompilerParams(
            dimension_semantics=("arbitrary",),
            vmem_limit_bytes=VMEM_LIMIT_BYTES),
        name="hymba_layer",
    )(x, ln1_g, w_in[0], pool_w, pool_scale, conv_w, conv_b, w_a, b_a, w_i, b_i, lam, gn_pool_g, gn_lru_g,
      w_out[0], ln2_g, w_ffn_gate[0], w_ffn_up[0], w_ffn_down[0], lnf_g.reshape(1, D))
    return out
```

```python
import functools
import math

import jax
import jax.numpy as jnp
from jax import lax
from jax.experimental import pallas as pl
from jax.experimental.pallas import tpu as pltpu

D_MODEL = 1024
POOL_WINDOWS = (2, 4, 8, 16)
POOL_GROUP = 128
POOL_WIDTH = 512
LRU_WIDTH = 512
LRU_HEADS = 8
LRU_HEAD_DIM = 64
CONV_WIDTH = 4
LRU_C = 8.0
D_FF = 2816
IN_WIDTH = POOL_WIDTH + 2 * LRU_WIDTH
EPS = 1e-6

LANES = 128
SUBLANES = 8
SEQ_TILE = 512
POOL_TAIL = 16
CONV_TAIL = 8
SUB_LEN = SEQ_TILE // SUBLANES
SUB_PITCH = SUB_LEN + SUBLANES
FF_CHUNK = 512
N_FF_CHUNKS = -(-D_FF // FF_CHUNK)
FF_DOWN_GROUP = 3
GATE_BLOCKS = 1
FFN_PIECES_AT = (1, 1, 1, (1,), 1, 1, 1, 1)
HBM_WEIGHT_OPERANDS = (2, 14, 16, 17, 18)
WEIGHT_STAGE_ROWS = 256
WEIGHT_STAGE_SLOTS = 3
VMEM_LIMIT_BYTES = 58 * 1024 * 1024


def _ffn_sequence():
    ops = []
    for c in range(N_FF_CHUNKS):
        ops.append(("u", c))
        if c % FF_DOWN_GROUP == 0 and c > 0:
            ops.append(("d", c - FF_DOWN_GROUP))
    last = (N_FF_CHUNKS - 1) // FF_DOWN_GROUP * FF_DOWN_GROUP
    return ops + [("d", last)]


def _ff_chunk_cols():
    return tuple((c * FF_CHUNK, min((c + 1) * FF_CHUNK, D_FF)) for c in range(N_FF_CHUNKS))


def _rms(x, g):
    ms = jnp.mean(x * x, axis=-1, keepdims=True)
    return x * lax.rsqrt(ms + EPS) * g


def _sigmoid(x):
    return 0.5 * (jnp.tanh(0.5 * x) + 1.0)


def _gelu_tanh_times(x, h):
    c = math.sqrt(2.0 / math.pi)
    t = jnp.tanh(x * ((x * x) * (0.044715 * c) + c))
    return (h * x) * (0.5 * t + 0.5)


def _dot(a, b):
    return jnp.dot(a, b, preferred_element_type=jnp.float32)


def _load_weight_bf16(src_hbm, dst, stage, sems, scale=None, col_map=None):
    rows, cols = src_hbm.shape
    col_map = col_map or ((0, cols, 0),)
    n = rows // WEIGHT_STAGE_ROWS

    def copy(r, slot):
        src = src_hbm.at[pl.ds(r * WEIGHT_STAGE_ROWS, WEIGHT_STAGE_ROWS), :]
        return pltpu.make_async_copy(src, stage.at[slot, :, pl.ds(0, cols)], sems.at[slot])

    ahead = WEIGHT_STAGE_SLOTS - 1
    for r in range(min(ahead, n)):
        copy(r, r % WEIGHT_STAGE_SLOTS).start()

    def body(r, carry):
        slot = lax.rem(r, WEIGHT_STAGE_SLOTS)

        @pl.when(r + ahead < n)
        def _():
            copy(r + ahead, lax.rem(r + ahead, WEIGHT_STAGE_SLOTS)).start()

        copy(r, slot).wait()
        row0 = pl.multiple_of(r * WEIGHT_STAGE_ROWS, WEIGHT_STAGE_ROWS)
        for c0, c1, d0 in col_map:
            w = stage[slot, :, c0:c1]
            dst[pl.ds(row0, WEIGHT_STAGE_ROWS), d0:d0 + c1 - c0] = (
                w if scale is None else w * scale).astype(jnp.bfloat16)
        return carry

    lax.fori_loop(0, n, body, 0)


def _build_small_weights(pool_w_ref, w_a_ref, w_i_ref, small_w_ref):
    half = LRU_WIDTH // 2
    heads = half // LRU_HEAD_DIM
    bf16 = jnp.bfloat16
    c_idx = lax.broadcasted_iota(jnp.int32, (LRU_HEAD_DIM, half), 0)
    j_idx = lax.broadcasted_iota(jnp.int32, (LRU_HEAD_DIM, half), 1)
    spread = (lax.rem(j_idx, LRU_HEAD_DIM) == c_idx).astype(bf16)
    col_head = j_idx // LRU_HEAD_DIM
    for q in range(2):
        for part, w_ref in enumerate((w_a_ref, w_i_ref)):
            for h in range(heads):
                rep = _dot(w_ref[0, heads * q + h].astype(bf16), spread)
                blk = jnp.where(col_head == h, rep, 0.0).astype(bf16)
                small_w_ref[q, h * LRU_HEAD_DIM:(h + 1) * LRU_HEAD_DIM, part * half:(part + 1) * half] = blk
        small_w_ref[q, :, 2 * half:3 * half] = jnp.zeros((half, half), bf16)
        for g in range(2):
            rows = slice(g * POOL_GROUP, (g + 1) * POOL_GROUP)
            cols = slice(2 * half + g * POOL_GROUP, 2 * half + (g + 1) * POOL_GROUP)
            small_w_ref[q, rows, cols] = pool_w_ref[0, 2 * q + g].astype(bf16)


def _layer_kernel(x_ref, ln1_ref, w_in_hbm, pool_w_ref, pool_scale_ref, conv_w_ref, conv_b_ref,
                  w_a_ref, b_a_ref, w_i_ref, b_i_ref, lam_ref, gn_pool_ref, gn_lru_ref, w_out_hbm,
                  ln2_ref, wg_hbm, wu_hbm, wd_hbm, lnf_ref, o_ref,
                  pbuf, lbuf, abuf, bbuf, hbuf, hcarry, h1buf, n2buf,
                  w_in_ref, w_out_ref, wgu_ref, wd_ref, small_w_ref, stage, stage_sems,
                  *, chunks_per_seq, n_chunks):
    i = pl.program_id(0)
    s = lax.rem(i, chunks_per_seq)
    T = SEQ_TILE
    half = LRU_WIDTH // 2

    @pl.when(i == 0)
    def _():
        g_map = tuple((c0, c1, 2 * c0) for c0, c1 in _ff_chunk_cols())
        u_map = tuple((c0, c1, c0 + c1) for c0, c1 in _ff_chunk_cols())
        for src, dst, scale, col_map in ((w_in_hbm, w_in_ref, None, None), (w_out_hbm, w_out_ref, None, None),
                                         (wg_hbm, wgu_ref, 0.5, g_map), (wu_hbm, wgu_ref, None, u_map),
                                         (wd_hbm, wd_ref, None, None)):
            _load_weight_bf16(src, dst, stage, stage_sems, scale, col_map)
        _build_small_weights(pool_w_ref, w_a_ref, w_i_ref, small_w_ref)

    slot = lax.rem(i, 2)
    prev = 1 - slot

    def mixer():
        @pl.when(s == 0)
        def _():
            pbuf[0:POOL_TAIL, :] = jnp.zeros((POOL_TAIL, POOL_WIDTH), jnp.float32)
            lbuf[0:CONV_TAIL, :] = jnp.zeros((CONV_TAIL, LRU_WIDTH), jnp.float32)
            hcarry[...] = jnp.zeros_like(hcarry)

        yield FFN_PIECES_AT[0]
        x = x_ref[...]
        nb = _rms(x, ln1_ref[...]).astype(jnp.bfloat16)

        u_lru_gate = _dot(nb, w_in_ref[:, POOL_WIDTH:])
        lbuf[CONV_TAIL:CONV_TAIL + T, :] = u_lru_gate[:, :LRU_WIDTH]
        u_gate = u_lru_gate[:, LRU_WIDTH:]

        yield FFN_PIECES_AT[2]
        assert CONV_WIDTH == 4
        u_ext = lbuf[...]
        u_prev = pltpu.roll(u_ext, 1, axis=0)
        tap = lambda k: conv_w_ref[0, k:k + 1, :]
        older = tap(1) * u_ext + tap(0) * u_prev
        xc = (conv_b_ref[...] + tap(3) * u_ext + tap(2) * u_prev + pltpu.roll(older, 2, axis=0))[CONV_TAIL:]
        xcb = xc.astype(jnp.bfloat16)
        neg_lam = -lam_ref[...]
        softplus = jnp.maximum(neg_lam, 0.0) + jnp.log1p(jnp.exp(-jnp.abs(neg_lam)))
        k_row = (0.5 * LRU_C) * softplus

        nq = LRU_WIDTH // LANES
        tiles_per_sub = SUB_LEN // SUBLANES
        blk_rows = T // GATE_BLOCKS
        for blk in range(GATE_BLOCKS):
            rows_blk = slice(blk * blk_rows, (blk + 1) * blk_rows)
            z = [_dot(xcb[rows_blk, q * half:(q + 1) * half], small_w_ref[q, :, 0:2 * half])
                 for q in range(2)]
            yield FFN_PIECES_AT[3][blk]
            za = jnp.concatenate([z[0][:, :half], z[1][:, :half]], axis=-1) + b_a_ref[...]
            zi = jnp.concatenate([z[0][:, half:], z[1][:, half:]], axis=-1) + b_i_ref[...]
            neg_log_a = jnp.tanh(0.5 * za) * k_row + k_row
            a = jnp.exp(-neg_log_a)
            one_minus_a2 = jnp.tanh(neg_log_a) * (a * a + 1.0)
            mult = jnp.where(one_minus_a2 > 0.0, one_minus_a2 * lax.rsqrt(one_minus_a2), 0.0)
            bterm = mult * (_sigmoid(zi) * xc[rows_blk, :])
            for q in range(nq):
                lanes = slice(q * LANES, (q + 1) * LANES)
                for t in range(blk_rows // SUBLANES):
                    j, m0 = divmod(blk * (blk_rows // SUBLANES) + t, tiles_per_sub)
                    dst = pl.ds(m0 * SUBLANES * SUBLANES + j, SUBLANES, stride=SUBLANES)
                    rows = slice(t * SUBLANES, (t + 1) * SUBLANES)
                    abuf[q, dst, :] = a[rows, lanes]
                    bbuf[q, dst, :] = bterm[rows, lanes]

        def tile(m):
            return slice(m * SUBLANES, (m + 1) * SUBLANES)

        yield FFN_PIECES_AT[4]
        prod = [jnp.ones((SUBLANES, LANES), jnp.float32) for _ in range(nq)]
        loc = [jnp.zeros((SUBLANES, LANES), jnp.float32) for _ in range(nq)]
        for m in range(SUB_LEN):
            for q in range(nq):
                am = abuf[q, tile(m), :]
                loc[q] = am * loc[q] + bbuf[q, tile(m), :]
                prod[q] = am * prod[q]
        row = lax.broadcasted_iota(jnp.int32, (SUBLANES, LANES), 0)
        h_in = []
        for q in range(nq):
            carry = hcarry[:, q * LANES:(q + 1) * LANES]
            start = jnp.zeros((SUBLANES, LANES), jnp.float32)
            for j in range(SUBLANES):
                start = jnp.where(row == j, carry, start)
                carry = prod[q][j:j + 1, :] * carry + loc[q][j:j + 1, :]
            h_in.append(start)
            hcarry[:, q * LANES:(q + 1) * LANES] = carry
        yield FFN_PIECES_AT[5]
        for m in range(SUB_LEN):
            for q in range(nq):
                h_in[q] = abuf[q, tile(m), :] * h_in[q] + bbuf[q, tile(m), :]
                hbuf[q, pl.ds(m, SUBLANES, stride=SUB_PITCH), :] = h_in[q]
        h_lru = jnp.concatenate(
            [jnp.concatenate([hbuf[q, j * SUB_PITCH:j * SUB_PITCH + SUB_LEN, :] for j in range(SUBLANES)], axis=0)
             for q in range(nq)], axis=-1)

        pbuf[POOL_TAIL:POOL_TAIL + T, :] = _dot(nb, w_in_ref[:, 0:POOL_WIDTH])
        yield FFN_PIECES_AT[1]
        n_groups = len(POOL_WINDOWS)
        level = pbuf[...]
        sums = []
        for g, w in enumerate(POOL_WINDOWS):
            assert w == 2 << g and w - 1 <= POOL_TAIL
            rest = level[:, (0 if g == 0 else POOL_GROUP):]
            rest = rest + pltpu.roll(rest, w // 2, axis=0)
            sums.append(rest[POOL_TAIL:, 0:POOL_GROUP])
            level = rest
        head_pos = (s * T + 1 + lax.broadcasted_iota(jnp.int32, (POOL_TAIL, POOL_GROUP), 0)).astype(jnp.float32)
        d_parts = []
        for g, w in enumerate(POOL_WINDOWS):
            ug = pbuf[POOL_TAIL:POOL_TAIL + T, g * POOL_GROUP:(g + 1) * POOL_GROUP]
            head = sums[g][0:POOL_TAIL] / jnp.minimum(head_pos, float(w))
            mean = jnp.concatenate([head, sums[g][POOL_TAIL:] * (1.0 / w)], axis=0)
            d_parts.append((mean - ug).astype(jnp.bfloat16))
        y_parts = []
        for p in range(n_groups // 2):
            d_pair = jnp.concatenate(d_parts[2 * p:2 * p + 2], axis=-1)
            y_parts.append(_dot(d_pair, small_w_ref[p, :, 2 * half:3 * half]))
        y_pool = jnp.concatenate(y_parts, axis=-1) * pool_scale_ref[...]
        mix_pool = _rms(y_pool, gn_pool_ref[...]).astype(jnp.bfloat16)

        yield FFN_PIECES_AT[6]
        y_lru = _gelu_tanh_times(u_gate, h_lru)
        mix_lru = _rms(y_lru, gn_lru_ref[...]).astype(jnp.bfloat16)

        mix = jnp.concatenate([mix_pool, mix_lru], axis=-1)
        h1_new = x + _dot(mix, w_out_ref[...])
        yield FFN_PIECES_AT[7]
        h1buf[slot] = h1_new
        n2buf[slot] = _rms(h1_new, ln2_ref[...]).astype(jnp.bfloat16)

        pbuf[0:POOL_TAIL, :] = pbuf[T:T + POOL_TAIL, :]
        lbuf[0:CONV_TAIL, :] = lbuf[T:T + CONV_TAIL, :]

    def make_ffn():
        acts = {}
        ffn_parts = []
        pending = _ffn_sequence()

        def ffn(n):
            for _ in range(n):
                op, c = pending.pop(0)
                if op == "u":
                    c0, c1 = _ff_chunk_cols()[c]
                    n2 = n2buf[prev]
                    gu = _dot(n2, wgu_ref[:, 2 * c0:2 * c1])
                    half_g, up = gu[:, :c1 - c0], gu[:, c1 - c0:]
                    acts[c] = (half_g * (jnp.tanh(half_g) + 1.0) * up).astype(jnp.bfloat16)
                else:
                    group = range(c, min(c + FF_DOWN_GROUP, N_FF_CHUNKS))
                    act = jnp.concatenate([acts.pop(k) for k in group], axis=-1)
                    ffn_parts.append(_dot(act, wd_ref[c * FF_CHUNK:min((group[-1] + 1) * FF_CHUNK, D_FF), :]))

        def finish():
            ffn(len(pending))
            ffn_out = ffn_parts[0]
            for part in ffn_parts[1:]:
                ffn_out = ffn_out + part
            o_ref[...] = _rms(h1buf[prev] + ffn_out, lnf_ref[...])

        return ffn, finish

    @pl.when(i == 0)
    def _():
        for _ in mixer():
            pass

    @pl.when(jnp.logical_and(i > 0, i < n_chunks))
    def _():
        ffn, finish = make_ffn()
        for n in mixer():
            ffn(n)
        finish()

    @pl.when(i == n_chunks)
    def _():
        _, finish = make_ffn()
        finish()


def kernel(x, ln1_g, w_in, pool_w, pool_scale, conv_w, conv_b, w_a, b_a, w_i, b_i, lam, gn_pool_g,
           gn_lru_g, w_out, ln2_g, w_ffn_gate, w_ffn_up, w_ffn_down, lnf_g):
    B, S, D = x.shape
    assert D == D_MODEL and S % SEQ_TILE == 0 and ln1_g.shape[0] == 1
    bf16 = jnp.bfloat16
    vmem = pl.BlockSpec(memory_space=pltpu.VMEM)
    hbm = pl.BlockSpec(memory_space=pl.ANY)
    nq = LRU_WIDTH // LANES
    scan_rows = SUBLANES * SUB_PITCH
    cps = S // SEQ_TILE
    n_chunks = B * cps

    def in_index(i):
        c = jnp.minimum(i, n_chunks - 1)
        return (c // cps, c % cps, 0)

    def out_index(i):
        c = jnp.maximum(i - 1, 0)
        return (c // cps, c % cps, 0)

    out = pl.pallas_call(
        functools.partial(_layer_kernel, chunks_per_seq=cps, n_chunks=n_chunks),
        grid=(n_chunks + 1,),
        in_specs=[pl.BlockSpec((None, SEQ_TILE, D), in_index)] + [
            hbm if k in HBM_WEIGHT_OPERANDS else vmem for k in range(1, 20)],
        out_specs=pl.BlockSpec((None, SEQ_TILE, D), out_index),
        out_shape=jax.ShapeDtypeStruct((B, S, D), jnp.float32),
        scratch_shapes=[
            pltpu.VMEM((POOL_TAIL + SEQ_TILE, POOL_WIDTH), jnp.float32),
            pltpu.VMEM((CONV_TAIL + SEQ_TILE, LRU_WIDTH), jnp.float32),
            pltpu.VMEM((nq, SEQ_TILE, LANES), jnp.float32),
            pltpu.VMEM((nq, SEQ_TILE, LANES), jnp.float32),
            pltpu.VMEM((nq, scan_rows, LANES), jnp.float32),
            pltpu.VMEM((1, LRU_WIDTH), jnp.float32),
            pltpu.VMEM((2, SEQ_TILE, D), jnp.float32),
            pltpu.VMEM((2, SEQ_TILE, D), jnp.bfloat16),
            pltpu.VMEM((D, IN_WIDTH), bf16),
            pltpu.VMEM((D, D), bf16),
            pltpu.VMEM((D, 2 * D_FF), bf16),
            pltpu.VMEM((D_FF, D), bf16),
            pltpu.VMEM((2, LRU_WIDTH // 2, LRU_WIDTH + LRU_WIDTH // 2), bf16),
            pltpu.VMEM((WEIGHT_STAGE_SLOTS, WEIGHT_STAGE_ROWS, D_FF), jnp.float32),
            pltpu.SemaphoreType.DMA((WEIGHT_STAGE_SLOTS,)),
        ],
        compiler_params=pltpu.CompilerParams(
            dimension_semantics=("arbitrary",),
            vmem_limit_bytes=VMEM_LIMIT_BYTES),
        name="hymba_layer",
    )(x, ln1_g, w_in[0], pool_w, pool_scale, conv_w, conv_b, w_a, b_a, w_i, b_i, lam, gn_pool_g, gn_lru_g,
      w_out[0], ln2_g, w_ffn_gate[0], w_ffn_up[0], w_ffn_down[0], lnf_g.reshape(1, D))
    return out
```

```python
import functools
import math

import jax
import jax.numpy as jnp
from jax import lax
from jax.experimental import pallas as pl
from jax.experimental.pallas import tpu as pltpu

D_MODEL = 1024
POOL_WINDOWS = (2, 4, 8, 16)
POOL_GROUP = 128
POOL_WIDTH = 512
LRU_WIDTH = 512
LRU_HEADS = 8
LRU_HEAD_DIM = 64
CONV_WIDTH = 4
LRU_C = 8.0
D_FF = 2816
IN_WIDTH = POOL_WIDTH + 2 * LRU_WIDTH
EPS = 1e-6

LANES = 128
SUBLANES = 8
SEQ_TILE = 512
POOL_TAIL = 16
CONV_TAIL = 8
SUB_LEN = SEQ_TILE // SUBLANES
SUB_PITCH = SUB_LEN + SUBLANES
FF_CHUNK = 512
N_FF_CHUNKS = -(-D_FF // FF_CHUNK)
FF_DOWN_GROUP = 3
GATE_BLOCKS = 1
FFN_PIECES_AT = (1, 1, 1, (1,), 1, 1, 1, 1)
HBM_WEIGHT_OPERANDS = (2, 14, 16, 17, 18)
WEIGHT_STAGE_ROWS = 256
WEIGHT_STAGE_SLOTS = 3
VMEM_LIMIT_BYTES = 58 * 1024 * 1024


def _ffn_sequence():
    ops = []
    for c in range(N_FF_CHUNKS):
        ops.append(("u", c))
        if c % FF_DOWN_GROUP == 0 and c > 0:
            ops.append(("d", c - FF_DOWN_GROUP))
    last = (N_FF_CHUNKS - 1) // FF_DOWN_GROUP * FF_DOWN_GROUP
    return ops + [("d", last)]


def _ff_chunk_cols():
    return tuple((c * FF_CHUNK, min((c + 1) * FF_CHUNK, D_FF)) for c in range(N_FF_CHUNKS))


def _rms(x, g):
    ms = jnp.mean(x * x, axis=-1, keepdims=True)
    return x * lax.rsqrt(ms + EPS) * g


def _sigmoid(x):
    return 0.5 * (jnp.tanh(0.5 * x) + 1.0)


def _gelu_tanh_times(x, h):
    c = math.sqrt(2.0 / math.pi)
    t = jnp.tanh(x * ((x * x) * (0.044715 * c) + c))
    return (h * x) * (0.5 * t + 0.5)


def _dot(a, b):
    return jnp.dot(a, b, preferred_element_type=jnp.float32)


def _load_weight_bf16(src_hbm, dst, stage, sems, scale=None, col_map=None):
    rows, cols = src_hbm.shape
    col_map = col_map or ((0, cols, 0),)
    n = rows // WEIGHT_STAGE_ROWS

    def copy(r, slot):
        src = src_hbm.at[pl.ds(r * WEIGHT_STAGE_ROWS, WEIGHT_STAGE_ROWS), :]
        return pltpu.make_async_copy(src, stage.at[slot, :, pl.ds(0, cols)], sems.at[slot])

    ahead = WEIGHT_STAGE_SLOTS - 1
    for r in range(min(ahead, n)):
        copy(r, r % WEIGHT_STAGE_SLOTS).start()

    def body(r, carry):
        slot = lax.rem(r, WEIGHT_STAGE_SLOTS)

        @pl.when(r + ahead < n)
        def _():
            copy(r + ahead, lax.rem(r + ahead, WEIGHT_STAGE_SLOTS)).start()

        copy(r, slot).wait()
        row0 = pl.multiple_of(r * WEIGHT_STAGE_ROWS, WEIGHT_STAGE_ROWS)
        for c0, c1, d0 in col_map:
            w = stage[slot, :, c0:c1]
            dst[pl.ds(row0, WEIGHT_STAGE_ROWS), d0:d0 + c1 - c0] = (
                w if scale is None else w * scale).astype(jnp.bfloat16)
        return carry

    lax.fori_loop(0, n, body, 0)


def _build_small_weights(pool_w_ref, w_a_ref, w_i_ref, small_w_ref):
    half = LRU_WIDTH // 2
    heads = half // LRU_HEAD_DIM
    bf16 = jnp.bfloat16
    c_idx = lax.broadcasted_iota(jnp.int32, (LRU_HEAD_DIM, half), 0)
    j_idx = lax.broadcasted_iota(jnp.int32, (LRU_HEAD_DIM, half), 1)
    spread = (lax.rem(j_idx, LRU_HEAD_DIM) == c_idx).astype(bf16)
    col_head = j_idx // LRU_HEAD_DIM
    for q in range(2):
        for part, w_ref in enumerate((w_a_ref, w_i_ref)):
            for h in range(heads):
                rep = _dot(w_ref[0, heads * q + h].astype(bf16), spread)
                blk = jnp.where(col_head == h, rep, 0.0).astype(bf16)
                small_w_ref[q, h * LRU_HEAD_DIM:(h + 1) * LRU_HEAD_DIM, part * half:(part + 1) * half] = blk
        small_w_ref[q, :, 2 * half:3 * half] = jnp.zeros((half, half), bf16)
        for g in range(2):
            rows = slice(g * POOL_GROUP, (g + 1) * POOL_GROUP)
            cols = slice(2 * half + g * POOL_GROUP, 2 * half + (g + 1) * POOL_GROUP)
            small_w_ref[q, rows, cols] = pool_w_ref[0, 2 * q + g].astype(bf16)


def _layer_kernel(x_ref, ln1_ref, w_in_hbm, pool_w_ref, pool_scale_ref, conv_w_ref, conv_b_ref,
                  w_a_ref, b_a_ref, w_i_ref, b_i_ref, lam_ref, gn_pool_ref, gn_lru_ref, w_out_hbm,
                  ln2_ref, wg_hbm, wu_hbm, wd_hbm, lnf_ref, o_ref,
                  pbuf, lbuf, abuf, bbuf, hbuf, hcarry, h1buf, n2buf,
                  w_in_ref, w_out_ref, wgu_ref, wd_ref, small_w_ref, stage, stage_sems,
                  *, chunks_per_seq, n_chunks):
    i = pl.program_id(0)
    s = lax.rem(i, chunks_per_seq)
    T = SEQ_TILE
    half = LRU_WIDTH // 2

    @pl.when(i == 0)
    def _():
        g_map = tuple((c0, c1, 2 * c0) for c0, c1 in _ff_chunk_cols())
        u_map = tuple((c0, c1, c0 + c1) for c0, c1 in _ff_chunk_cols())
        for src, dst, scale, col_map in ((w_in_hbm, w_in_ref, None, None), (w_out_hbm, w_out_ref, None, None),
                                         (wg_hbm, wgu_ref, 0.5, g_map), (wu_hbm, wgu_ref, None, u_map),
                                         (wd_hbm, wd_ref, None, None)):
            _load_weight_bf16(src, dst, stage, stage_sems, scale, col_map)
        _build_small_weights(pool_w_ref, w_a_ref, w_i_ref, small_w_ref)

    slot = lax.rem(i, 2)
    prev = 1 - slot

    def mixer():
        @pl.when(s == 0)
        def _():
            pbuf[0:POOL_TAIL, :] = jnp.zeros((POOL_TAIL, POOL_WIDTH), jnp.float32)
            lbuf[0:CONV_TAIL, :] = jnp.zeros((CONV_TAIL, LRU_WIDTH), jnp.float32)
            hcarry[...] = jnp.zeros_like(hcarry)

        yield FFN_PIECES_AT[0]
        x = x_ref[...]
        nb = _rms(x, ln1_ref[...]).astype(jnp.bfloat16)

        lbuf[CONV_TAIL:CONV_TAIL + T, :] = _dot(nb, w_in_ref[:, POOL_WIDTH:POOL_WIDTH + LRU_WIDTH])
        u_gate = _dot(nb, w_in_ref[:, POOL_WIDTH + LRU_WIDTH:])

        yield FFN_PIECES_AT[2]
        assert CONV_WIDTH == 4
        u_ext = lbuf[...]
        u_prev = pltpu.roll(u_ext, 1, axis=0)
        tap = lambda k: conv_w_ref[0, k:k + 1, :]
        older = tap(1) * u_ext + tap(0) * u_prev
        xc = (conv_b_ref[...] + tap(3) * u_ext + tap(2) * u_prev + pltpu.roll(older, 2, axis=0))[CONV_TAIL:]
        xcb = xc.astype(jnp.bfloat16)
        neg_lam = -lam_ref[...]
        softplus = jnp.maximum(neg_lam, 0.0) + jnp.log1p(jnp.exp(-jnp.abs(neg_lam)))
        k_row = (0.5 * LRU_C) * softplus

        nq = LRU_WIDTH // LANES
        tiles_per_sub = SUB_LEN // SUBLANES
        blk_rows = T // GATE_BLOCKS
        for blk in range(GATE_BLOCKS):
            rows_blk = slice(blk * blk_rows, (blk + 1) * blk_rows)
            z = [_dot(xcb[rows_blk, q * half:(q + 1) * half], small_w_ref[q, :, 0:2 * half])
                 for q in range(2)]
            yield FFN_PIECES_AT[3][blk]
            za = jnp.concatenate([z[0][:, :half], z[1][:, :half]], axis=-1) + b_a_ref[...]
            zi = jnp.concatenate([z[0][:, half:], z[1][:, half:]], axis=-1) + b_i_ref[...]
            neg_log_a = jnp.tanh(0.5 * za) * k_row + k_row
            a = jnp.exp(-neg_log_a)
            one_minus_a2 = jnp.tanh(neg_log_a) * (a * a + 1.0)
            mult = jnp.where(one_minus_a2 > 0.0, one_minus_a2 * lax.rsqrt(one_minus_a2), 0.0)
            bterm = mult * (_sigmoid(zi) * xc[rows_blk, :])
            for q in range(nq):
                lanes = slice(q * LANES, (q + 1) * LANES)
                for t in range(blk_rows // SUBLANES):
                    j, m0 = divmod(blk * (blk_rows // SUBLANES) + t, tiles_per_sub)
                    dst = pl.ds(m0 * SUBLANES * SUBLANES + j, SUBLANES, stride=SUBLANES)
                    rows = slice(t * SUBLANES, (t + 1) * SUBLANES)
                    abuf[q, dst, :] = a[rows, lanes]
                    bbuf[q, dst, :] = bterm[rows, lanes]

        def tile(m):
            return slice(m * SUBLANES, (m + 1) * SUBLANES)

        yield FFN_PIECES_AT[4]
        prod = [jnp.ones((SUBLANES, LANES), jnp.float32) for _ in range(nq)]
        loc = [jnp.zeros((SUBLANES, LANES), jnp.float32) for _ in range(nq)]
        for m in range(SUB_LEN):
            for q in range(nq):
                am = abuf[q, tile(m), :]
                loc[q] = am * loc[q] + bbuf[q, tile(m), :]
                prod[q] = am * prod[q]
        row = lax.broadcasted_iota(jnp.int32, (SUBLANES, LANES), 0)
        h_in = []
        for q in range(nq):
            carry = hcarry[:, q * LANES:(q + 1) * LANES]
            start = jnp.zeros((SUBLANES, LANES), jnp.float32)
            for j in range(SUBLANES):
                start = jnp.where(row == j, carry, start)
                carry = prod[q][j:j + 1, :] * carry + loc[q][j:j + 1, :]
            h_in.append(start)
            hcarry[:, q * LANES:(q + 1) * LANES] = carry
        yield FFN_PIECES_AT[5]
        for m in range(SUB_LEN):
            for q in range(nq):
                h_in[q] = abuf[q, tile(m), :] * h_in[q] + bbuf[q, tile(m), :]
                hbuf[q, pl.ds(m, SUBLANES, stride=SUB_PITCH), :] = h_in[q]
        h_lru = jnp.concatenate(
            [jnp.concatenate([hbuf[q, j * SUB_PITCH:j * SUB_PITCH + SUB_LEN, :] for j in range(SUBLANES)], axis=0)
             for q in range(nq)], axis=-1)

        pbuf[POOL_TAIL:POOL_TAIL + T, :] = _dot(nb, w_in_ref[:, 0:POOL_WIDTH])
        yield FFN_PIECES_AT[1]
        n_groups = len(POOL_WINDOWS)
        level = pbuf[...]
        sums = []
        for g, w in enumerate(POOL_WINDOWS):
            assert w == 2 << g and w - 1 <= POOL_TAIL
            rest = level[:, (0 if g == 0 else POOL_GROUP):]
            rest = rest + pltpu.roll(rest, w // 2, axis=0)
            sums.append(rest[POOL_TAIL:, 0:POOL_GROUP])
            level = rest
        head_pos = (s * T + 1 + lax.broadcasted_iota(jnp.int32, (POOL_TAIL, POOL_GROUP), 0)).astype(jnp.float32)
        d_parts = []
        for g, w in enumerate(POOL_WINDOWS):
            ug = pbuf[POOL_TAIL:POOL_TAIL + T, g * POOL_GROUP:(g + 1) * POOL_GROUP]
            head = sums[g][0:POOL_TAIL] / jnp.minimum(head_pos, float(w))
            mean = jnp.concatenate([head, sums[g][POOL_TAIL:] * (1.0 / w)], axis=0)
            d_parts.append((mean - ug).astype(jnp.bfloat16))
        y_parts = []
        for p in range(n_groups // 2):
            d_pair = jnp.concatenate(d_parts[2 * p:2 * p + 2], axis=-1)
            y_parts.append(_dot(d_pair, small_w_ref[p, :, 2 * half:3 * half]))
        y_pool = jnp.concatenate(y_parts, axis=-1) * pool_scale_ref[...]
        mix_pool = _rms(y_pool, gn_pool_ref[...]).astype(jnp.bfloat16)

        yield FFN_PIECES_AT[6]
        y_lru = _gelu_tanh_times(u_gate, h_lru)
        mix_lru = _rms(y_lru, gn_lru_ref[...]).astype(jnp.bfloat16)

        mix = jnp.concatenate([mix_pool, mix_lru], axis=-1)
        h1_new = x + _dot(mix, w_out_ref[...])
        yield FFN_PIECES_AT[7]
        h1buf[slot] = h1_new
        n2buf[slot] = _rms(h1_new, ln2_ref[...]).astype(jnp.bfloat16)

        pbuf[0:POOL_TAIL, :] = pbuf[T:T + POOL_TAIL, :]
        lbuf[0:CONV_TAIL, :] = lbuf[T:T + CONV_TAIL, :]

    def make_ffn():
        acts = {}
        ffn_parts = []
        pending = _ffn_sequence()

        def ffn(n):
            for _ in range(n):
                op, c = pending.pop(0)
                if op == "u":
                    c0, c1 = _ff_chunk_cols()[c]
                    n2 = n2buf[prev]
                    gu = _dot(n2, wgu_ref[:, 2 * c0:2 * c1])
                    half_g, up = gu[:, :c1 - c0], gu[:, c1 - c0:]
                    acts[c] = (half_g * (jnp.tanh(half_g) + 1.0) * up).astype(jnp.bfloat16)
                else:
                    group = range(c, min(c + FF_DOWN_GROUP, N_FF_CHUNKS))
                    act = jnp.concatenate([acts.pop(k) for k in group], axis=-1)
                    ffn_parts.append(_dot(act, wd_ref[c * FF_CHUNK:min((group[-1] + 1) * FF_CHUNK, D_FF), :]))

        def finish():
            ffn(len(pending))
            h2 = h1buf[prev]
            for part in ffn_parts:
                h2 = h2 + part
            o_ref[...] = _rms(h2, lnf_ref[...])

        return ffn, finish

    @pl.when(i == 0)
    def _():
        for _ in mixer():
            pass

    @pl.when(jnp.logical_and(i > 0, i < n_chunks))
    def _():
        ffn, finish = make_ffn()
        for n in mixer():
            ffn(n)
        finish()

    @pl.when(i == n_chunks)
    def _():
        _, finish = make_ffn()
        finish()


def kernel(x, ln1_g, w_in, pool_w, pool_scale, conv_w, conv_b, w_a, b_a, w_i, b_i, lam, gn_pool_g,
           gn_lru_g, w_out, ln2_g, w_ffn_gate, w_ffn_up, w_ffn_down, lnf_g):
    B, S, D = x.shape
    assert D == D_MODEL and S % SEQ_TILE == 0 and ln1_g.shape[0] == 1
    bf16 = jnp.bfloat16
    vmem = pl.BlockSpec(memory_space=pltpu.VMEM)
    hbm = pl.BlockSpec(memory_space=pl.ANY)
    nq = LRU_WIDTH // LANES
    scan_rows = SUBLANES * SUB_PITCH
    cps = S // SEQ_TILE
    n_chunks = B * cps

    def in_index(i):
        c = jnp.minimum(i, n_chunks - 1)
        return (c // cps, c % cps, 0)

    def out_index(i):
        c = jnp.maximum(i - 1, 0)
        return (c // cps, c % cps, 0)

    out = pl.pallas_call(
        functools.partial(_layer_kernel, chunks_per_seq=cps, n_chunks=n_chunks),
        grid=(n_chunks + 1,),
        in_specs=[pl.BlockSpec((None, SEQ_TILE, D), in_index)] + [
            hbm if k in HBM_WEIGHT_OPERANDS else vmem for k in range(1, 20)],
        out_specs=pl.BlockSpec((None, SEQ_TILE, D), out_index),
        out_shape=jax.ShapeDtypeStruct((B, S, D), jnp.float32),
        scratch_shapes=[
            pltpu.VMEM((POOL_TAIL + SEQ_TILE, POOL_WIDTH), jnp.float32),
            pltpu.VMEM((CONV_TAIL + SEQ_TILE, LRU_WIDTH), jnp.float32),
            pltpu.VMEM((nq, SEQ_TILE, LANES), jnp.float32),
            pltpu.VMEM((nq, SEQ_TILE, LANES), jnp.float32),
            pltpu.VMEM((nq, scan_rows, LANES), jnp.float32),
            pltpu.VMEM((1, LRU_WIDTH), jnp.float32),
            pltpu.VMEM((2, SEQ_TILE, D), jnp.float32),
            pltpu.VMEM((2, SEQ_TILE, D), jnp.bfloat16),
            pltpu.VMEM((D, IN_WIDTH), bf16),
            pltpu.VMEM((D, D), bf16),
            pltpu.VMEM((D, 2 * D_FF), bf16),
            pltpu.VMEM((D_FF, D), bf16),
            pltpu.VMEM((2, LRU_WIDTH // 2, LRU_WIDTH + LRU_WIDTH // 2), bf16),
            pltpu.VMEM((WEIGHT_STAGE_SLOTS, WEIGHT_STAGE_ROWS, D_FF), jnp.float32),
            pltpu.SemaphoreType.DMA((WEIGHT_STAGE_SLOTS,)),
        ],
        compiler_params=pltpu.CompilerParams(
            dimension_semantics=("arbitrary",),
            vmem_limit_bytes=VMEM_LIMIT_BYTES),
        name="hymba_layer",
    )(x, ln1_g, w_in[0], pool_w, pool_scale, conv_w, conv_b, w_a, b_a, w_i, b_i, lam, gn_pool_g, gn_lru_g,
      w_out[0], ln2_g, w_ffn_gate[0], w_ffn_up[0], w_ffn_down[0], lnf_g.reshape(1, D))
    return out
```

```python
import functools
import math

import jax
import jax.numpy as jnp
from jax import lax
from jax.experimental import pallas as pl
from jax.experimental.pallas import tpu as pltpu

D_MODEL = 1024
POOL_WINDOWS = (2, 4, 8, 16)
POOL_GROUP = 128
POOL_WIDTH = 512
LRU_WIDTH = 512
LRU_HEADS = 8
LRU_HEAD_DIM = 64
CONV_WIDTH = 4
LRU_C = 8.0
D_FF = 2816
IN_WIDTH = POOL_WIDTH + 2 * LRU_WIDTH
EPS = 1e-6

LANES = 128
SUBLANES = 8
SEQ_TILE = 512
POOL_TAIL = 16
CONV_TAIL = 8
SUB_LEN = SEQ_TILE // SUBLANES
SUB_PITCH = SUB_LEN + SUBLANES
FF_CHUNK = 512
N_FF_CHUNKS = -(-D_FF // FF_CHUNK)
FF_DOWN_GROUP = 3
GATE_BLOCKS = 1
FFN_PIECES_AT = (1, 1, 1, (1,), 1, 1, 1, 1)
HBM_WEIGHT_OPERANDS = (2, 14, 16, 17, 18)
WEIGHT_STAGE_ROWS = 256
WEIGHT_STAGE_SLOTS = 3
VMEM_LIMIT_BYTES = 58 * 1024 * 1024


def _ffn_sequence():
    ops = []
    for c in range(N_FF_CHUNKS):
        ops.append(("u", c))
        if c % FF_DOWN_GROUP == 0 and c > 0:
            ops.append(("d", c - FF_DOWN_GROUP))
    last = (N_FF_CHUNKS - 1) // FF_DOWN_GROUP * FF_DOWN_GROUP
    return ops + [("d", last)]


def _ff_chunk_cols():
    return tuple((c * FF_CHUNK, min((c + 1) * FF_CHUNK, D_FF)) for c in range(N_FF_CHUNKS))


def _rms(x, g):
    ms = jnp.mean(x * x, axis=-1, keepdims=True)
    return x * lax.rsqrt(ms + EPS) * g


def _sigmoid(x):
    return 0.5 * (jnp.tanh(0.5 * x) + 1.0)


def _gelu_tanh_times(x, h):
    c = math.sqrt(2.0 / math.pi)
    t = jnp.tanh(x * ((x * x) * (0.044715 * c) + c))
    return (h * x) * (0.5 * t + 0.5)


def _dot(a, b):
    return jnp.dot(a, b, preferred_element_type=jnp.float32)


def _load_weight_bf16(src_hbm, dst, stage, sems, scale=None, col_map=None):
    rows, cols = src_hbm.shape
    col_map = col_map or ((0, cols, 0),)
    n = rows // WEIGHT_STAGE_ROWS

    def copy(r, slot):
        src = src_hbm.at[pl.ds(r * WEIGHT_STAGE_ROWS, WEIGHT_STAGE_ROWS), :]
        return pltpu.make_async_copy(src, stage.at[slot, :, pl.ds(0, cols)], sems.at[slot])

    ahead = WEIGHT_STAGE_SLOTS - 1
    for r in range(min(ahead, n)):
        copy(r, r % WEIGHT_STAGE_SLOTS).start()

    def body(r, carry):
        slot = lax.rem(r, WEIGHT_STAGE_SLOTS)

        @pl.when(r + ahead < n)
        def _():
            copy(r + ahead, lax.rem(r + ahead, WEIGHT_STAGE_SLOTS)).start()

        copy(r, slot).wait()
        row0 = pl.multiple_of(r * WEIGHT_STAGE_ROWS, WEIGHT_STAGE_ROWS)
        for c0, c1, d0 in col_map:
            w = stage[slot, :, c0:c1]
            dst[pl.ds(row0, WEIGHT_STAGE_ROWS), d0:d0 + c1 - c0] = (
                w if scale is None else w * scale).astype(jnp.bfloat16)
        return carry

    lax.fori_loop(0, n, body, 0)


def _build_small_weights(pool_w_ref, w_a_ref, w_i_ref, small_w_ref):
    half = LRU_WIDTH // 2
    heads = half // LRU_HEAD_DIM
    bf16 = jnp.bfloat16
    c_idx = lax.broadcasted_iota(jnp.int32, (LRU_HEAD_DIM, half), 0)
    j_idx = lax.broadcasted_iota(jnp.int32, (LRU_HEAD_DIM, half), 1)
    spread = (lax.rem(j_idx, LRU_HEAD_DIM) == c_idx).astype(bf16)
    col_head = j_idx // LRU_HEAD_DIM
    for q in range(2):
        for part, w_ref in enumerate((w_a_ref, w_i_ref)):
            for h in range(heads):
                rep = _dot(w_ref[0, heads * q + h].astype(bf16), spread)
                blk = jnp.where(col_head == h, rep, 0.0).astype(bf16)
                small_w_ref[q, h * LRU_HEAD_DIM:(h + 1) * LRU_HEAD_DIM, part * half:(part + 1) * half] = blk
        small_w_ref[q, :, 2 * half:3 * half] = jnp.zeros((half, half), bf16)
        for g in range(2):
            rows = slice(g * POOL_GROUP, (g + 1) * POOL_GROUP)
            cols = slice(2 * half + g * POOL_GROUP, 2 * half + (g + 1) * POOL_GROUP)
            small_w_ref[q, rows, cols] = pool_w_ref[0, 2 * q + g].astype(bf16)


def _layer_kernel(x_ref, ln1_ref, w_in_hbm, pool_w_ref, pool_scale_ref, conv_w_ref, conv_b_ref,
                  w_a_ref, b_a_ref, w_i_ref, b_i_ref, lam_ref, gn_pool_ref, gn_lru_ref, w_out_hbm,
                  ln2_ref, wg_hbm, wu_hbm, wd_hbm, lnf_ref, o_ref,
                  pbuf, lbuf, abuf, bbuf, hbuf, hcarry, h1buf, n2buf,
                  w_in_ref, w_out_ref, wgu_ref, wd_ref, small_w_ref, stage, stage_sems,
                  *, chunks_per_seq, n_chunks):
    i = pl.program_id(0)
    s = lax.rem(i, chunks_per_seq)
    T = SEQ_TILE
    half = LRU_WIDTH // 2

    @pl.when(i == 0)
    def _():
        g_map = tuple((c0, c1, 2 * c0) for c0, c1 in _ff_chunk_cols())
        u_map = tuple((c0, c1, c0 + c1) for c0, c1 in _ff_chunk_cols())
        for src, dst, scale, col_map in ((w_in_hbm, w_in_ref, None, None), (w_out_hbm, w_out_ref, None, None),
                                         (wg_hbm, wgu_ref, 0.5, g_map), (wu_hbm, wgu_ref, None, u_map),
                                         (wd_hbm, wd_ref, None, None)):
            _load_weight_bf16(src, dst, stage, stage_sems, scale, col_map)
        _build_small_weights(pool_w_ref, w_a_ref, w_i_ref, small_w_ref)

    slot = lax.rem(i, 2)
    prev = 1 - slot

    def mixer():
        @pl.when(s == 0)
        def _():
            pbuf[0:POOL_TAIL, :] = jnp.zeros((POOL_TAIL, POOL_WIDTH), jnp.float32)
            lbuf[0:CONV_TAIL, :] = jnp.zeros((CONV_TAIL, LRU_WIDTH), jnp.float32)
            hcarry[...] = jnp.zeros_like(hcarry)

        yield FFN_PIECES_AT[0]
        nb = _rms(x_ref[...], ln1_ref[...]).astype(jnp.bfloat16)

        lbuf[CONV_TAIL:CONV_TAIL + T, :] = _dot(nb, w_in_ref[:, POOL_WIDTH:POOL_WIDTH + LRU_WIDTH])
        u_gate = _dot(nb, w_in_ref[:, POOL_WIDTH + LRU_WIDTH:])

        yield FFN_PIECES_AT[2]
        assert CONV_WIDTH == 4
        u_ext = lbuf[...]
        u_prev = pltpu.roll(u_ext, 1, axis=0)
        tap = lambda k: conv_w_ref[0, k:k + 1, :]
        older = tap(1) * u_ext + tap(0) * u_prev
        xc = (conv_b_ref[...] + tap(3) * u_ext + tap(2) * u_prev + pltpu.roll(older, 2, axis=0))[CONV_TAIL:]
        xcb = xc.astype(jnp.bfloat16)
        neg_lam = -lam_ref[...]
        softplus = jnp.maximum(neg_lam, 0.0) + jnp.log1p(jnp.exp(-jnp.abs(neg_lam)))
        k_row = (0.5 * LRU_C) * softplus

        nq = LRU_WIDTH // LANES
        tiles_per_sub = SUB_LEN // SUBLANES
        blk_rows = T // GATE_BLOCKS
        for blk in range(GATE_BLOCKS):
            rows_blk = slice(blk * blk_rows, (blk + 1) * blk_rows)
            z = [_dot(xcb[rows_blk, q * half:(q + 1) * half], small_w_ref[q, :, 0:2 * half])
                 for q in range(2)]
            yield FFN_PIECES_AT[3][blk]
            za = jnp.concatenate([z[0][:, :half], z[1][:, :half]], axis=-1) + b_a_ref[...]
            zi = jnp.concatenate([z[0][:, half:], z[1][:, half:]], axis=-1) + b_i_ref[...]
            neg_log_a = jnp.tanh(0.5 * za) * k_row + k_row
            a = jnp.exp(-neg_log_a)
            one_minus_a2 = jnp.tanh(neg_log_a) * (a * a + 1.0)
            mult = jnp.where(one_minus_a2 > 0.0, one_minus_a2 * lax.rsqrt(one_minus_a2), 0.0)
            bterm = mult * (_sigmoid(zi) * xc[rows_blk, :])
            for q in range(nq):
                lanes = slice(q * LANES, (q + 1) * LANES)
                for t in range(blk_rows // SUBLANES):
                    j, m0 = divmod(blk * (blk_rows // SUBLANES) + t, tiles_per_sub)
                    dst = pl.ds(m0 * SUBLANES * SUBLANES + j, SUBLANES, stride=SUBLANES)
                    rows = slice(t * SUBLANES, (t + 1) * SUBLANES)
                    abuf[q, dst, :] = a[rows, lanes]
                    bbuf[q, dst, :] = bterm[rows, lanes]

        def tile(m):
            return slice(m * SUBLANES, (m + 1) * SUBLANES)

        yield FFN_PIECES_AT[4]
        prod = [jnp.ones((SUBLANES, LANES), jnp.float32) for _ in range(nq)]
        loc = [jnp.zeros((SUBLANES, LANES), jnp.float32) for _ in range(nq)]
        for m in range(SUB_LEN):
            for q in range(nq):
                am = abuf[q, tile(m), :]
                loc[q] = am * loc[q] + bbuf[q, tile(m), :]
                prod[q] = am * prod[q]
        row = lax.broadcasted_iota(jnp.int32, (SUBLANES, LANES), 0)
        h_in = []
        for q in range(nq):
            carry = hcarry[:, q * LANES:(q + 1) * LANES]
            start = jnp.zeros((SUBLANES, LANES), jnp.float32)
            for j in range(SUBLANES):
                start = jnp.where(row == j, carry, start)
                carry = prod[q][j:j + 1, :] * carry + loc[q][j:j + 1, :]
            h_in.append(start)
            hcarry[:, q * LANES:(q + 1) * LANES] = carry
        yield FFN_PIECES_AT[5]
        for m in range(SUB_LEN):
            for q in range(nq):
                h_in[q] = abuf[q, tile(m), :] * h_in[q] + bbuf[q, tile(m), :]
                hbuf[q, pl.ds(m, SUBLANES, stride=SUB_PITCH), :] = h_in[q]
        h_lru = jnp.concatenate(
            [jnp.concatenate([hbuf[q, j * SUB_PITCH:j * SUB_PITCH + SUB_LEN, :] for j in range(SUBLANES)], axis=0)
             for q in range(nq)], axis=-1)

        pbuf[POOL_TAIL:POOL_TAIL + T, :] = _dot(nb, w_in_ref[:, 0:POOL_WIDTH])
        yield FFN_PIECES_AT[1]
        n_groups = len(POOL_WINDOWS)
        level = pbuf[...]
        sums = []
        for g, w in enumerate(POOL_WINDOWS):
            assert w == 2 << g and w - 1 <= POOL_TAIL
            rest = level[:, (0 if g == 0 else POOL_GROUP):]
            rest = rest + pltpu.roll(rest, w // 2, axis=0)
            sums.append(rest[POOL_TAIL:, 0:POOL_GROUP])
            level = rest
        head_pos = (s * T + 1 + lax.broadcasted_iota(jnp.int32, (POOL_TAIL, POOL_GROUP), 0)).astype(jnp.float32)
        d_parts = []
        for g, w in enumerate(POOL_WINDOWS):
            ug = pbuf[POOL_TAIL:POOL_TAIL + T, g * POOL_GROUP:(g + 1) * POOL_GROUP]
            head = sums[g][0:POOL_TAIL] / jnp.minimum(head_pos, float(w))
            mean = jnp.concatenate([head, sums[g][POOL_TAIL:] * (1.0 / w)], axis=0)
            d_parts.append((mean - ug).astype(jnp.bfloat16))
        y_parts = []
        for p in range(n_groups // 2):
            d_pair = jnp.concatenate(d_parts[2 * p:2 * p + 2], axis=-1)
            y_parts.append(_dot(d_pair, small_w_ref[p, :, 2 * half:3 * half]))
        y_pool = jnp.concatenate(y_parts, axis=-1) * pool_scale_ref[...]
        mix_pool = _rms(y_pool, gn_pool_ref[...]).astype(jnp.bfloat16)

        yield FFN_PIECES_AT[6]
        y_lru = _gelu_tanh_times(u_gate, h_lru)
        mix_lru = _rms(y_lru, gn_lru_ref[...]).astype(jnp.bfloat16)

        mix = jnp.concatenate([mix_pool, mix_lru], axis=-1)
        h1_new = x_ref[...] + _dot(mix, w_out_ref[...])
        yield FFN_PIECES_AT[7]
        h1buf[slot] = h1_new
        n2buf[slot] = _rms(h1_new, ln2_ref[...]).astype(jnp.bfloat16)

        pbuf[0:POOL_TAIL, :] = pbuf[T:T + POOL_TAIL, :]
        lbuf[0:CONV_TAIL, :] = lbuf[T:T + CONV_TAIL, :]

    def make_ffn():
        acts = {}
        ffn_parts = []
        pending = _ffn_sequence()

        def ffn(n):
            for _ in range(n):
                op, c = pending.pop(0)
                if op == "u":
                    c0, c1 = _ff_chunk_cols()[c]
                    n2 = n2buf[prev]
                    gu = _dot(n2, wgu_ref[:, 2 * c0:2 * c1])
                    half_g, up = gu[:, :c1 - c0], gu[:, c1 - c0:]
                    acts[c] = (half_g * (jnp.tanh(half_g) + 1.0) * up).astype(jnp.bfloat16)
                else:
                    group = range(c, min(c + FF_DOWN_GROUP, N_FF_CHUNKS))
                    act = jnp.concatenate([acts.pop(k) for k in group], axis=-1)
                    ffn_parts.append(_dot(act, wd_ref[c * FF_CHUNK:min((group[-1] + 1) * FF_CHUNK, D_FF), :]))

        def finish():
            ffn(len(pending))
            ffn_out = ffn_parts[0]
            for part in ffn_parts[1:]:
                ffn_out = ffn_out + part
            o_ref[...] = _rms(h1buf[prev] + ffn_out, lnf_ref[...])

        return ffn, finish

    @pl.when(i == 0)
    def _():
        for _ in mixer():
            pass

    @pl.when(jnp.logical_and(i > 0, i < n_chunks))
    def _():
        ffn, finish = make_ffn()
        for n in mixer():
            ffn(n)
        finish()

    @pl.when(i == n_chunks)
    def _():
        _, finish = make_ffn()
        finish()


def kernel(x, ln1_g, w_in, pool_w, pool_scale, conv_w, conv_b, w_a, b_a, w_i, b_i, lam, gn_pool_g,
           gn_lru_g, w_out, ln2_g, w_ffn_gate, w_ffn_up, w_ffn_down, lnf_g):
    B, S, D = x.shape
    assert D == D_MODEL and S % SEQ_TILE == 0 and ln1_g.shape[0] == 1
    bf16 = jnp.bfloat16
    vmem = pl.BlockSpec(memory_space=pltpu.VMEM)
    hbm = pl.BlockSpec(memory_space=pl.ANY)
    nq = LRU_WIDTH // LANES
    scan_rows = SUBLANES * SUB_PITCH
    cps = S // SEQ_TILE
    n_chunks = B * cps

    def in_index(i):
        c = jnp.minimum(i, n_chunks - 1)
        return (c // cps, c % cps, 0)

    def out_index(i):
        c = jnp.maximum(i - 1, 0)
        return (c // cps, c % cps, 0)

    out = pl.pallas_call(
        functools.partial(_layer_kernel, chunks_per_seq=cps, n_chunks=n_chunks),
        grid=(n_chunks + 1,),
        in_specs=[pl.BlockSpec((None, SEQ_TILE, D), in_index)] + [
            hbm if k in HBM_WEIGHT_OPERANDS else vmem for k in range(1, 20)],
        out_specs=pl.BlockSpec((None, SEQ_TILE, D), out_index),
        out_shape=jax.ShapeDtypeStruct((B, S, D), jnp.float32),
        scratch_shapes=[
            pltpu.VMEM((POOL_TAIL + SEQ_TILE, POOL_WIDTH), jnp.float32),
            pltpu.VMEM((CONV_TAIL + SEQ_TILE, LRU_WIDTH), jnp.float32),
            pltpu.VMEM((nq, SEQ_TILE, LANES), jnp.float32),
            pltpu.VMEM((nq, SEQ_TILE, LANES), jnp.float32),
            pltpu.VMEM((nq, scan_rows, LANES), jnp.float32),
            pltpu.VMEM((1, LRU_WIDTH), jnp.float32),
            pltpu.VMEM((2, SEQ_TILE, D), jnp.float32),
            pltpu.VMEM((2, SEQ_TILE, D), jnp.bfloat16),
            pltpu.VMEM((D, IN_WIDTH), bf16),
            pltpu.VMEM((D, D), bf16),
            pltpu.VMEM((D, 2 * D_FF), bf16),
            pltpu.VMEM((D_FF, D), bf16),
            pltpu.VMEM((2, LRU_WIDTH // 2, LRU_WIDTH + LRU_WIDTH // 2), bf16),
            pltpu.VMEM((WEIGHT_STAGE_SLOTS, WEIGHT_STAGE_ROWS, D_FF), jnp.float32),
            pltpu.SemaphoreType.DMA((WEIGHT_STAGE_SLOTS,)),
        ],
        compiler_params=pltpu.CompilerParams(
            dimension_semantics=("arbitrary",),
            vmem_limit_bytes=VMEM_LIMIT_BYTES),
        name="hymba_layer",
    )(x, ln1_g, w_in[0], pool_w, pool_scale, conv_w, conv_b, w_a, b_a, w_i, b_i, lam, gn_pool_g, gn_lru_g,
      w_out[0], ln2_g, w_ffn_gate[0], w_ffn_up[0], w_ffn_down[0], lnf_g.reshape(1, D))
    return out
```

```python
import functools
import math

import jax
import jax.numpy as jnp
from jax import lax
from jax.experimental import pallas as pl
from jax.experimental.pallas import tpu as pltpu

D_MODEL = 1024
POOL_WINDOWS = (2, 4, 8, 16)
POOL_GROUP = 128
POOL_WIDTH = 512
LRU_WIDTH = 512
LRU_HEADS = 8
LRU_HEAD_DIM = 64
CONV_WIDTH = 4
LRU_C = 8.0
D_FF = 2816
IN_WIDTH = POOL_WIDTH + 2 * LRU_WIDTH
EPS = 1e-6

LANES = 128
SUBLANES = 8
SEQ_TILE = 512
POOL_TAIL = 16
CONV_TAIL = 8
SUB_LEN = SEQ_TILE // SUBLANES
SUB_PITCH = SUB_LEN + SUBLANES
FF_CHUNK = 512
N_FF_CHUNKS = -(-D_FF // FF_CHUNK)
FF_DOWN_GROUP = 3
GATE_BLOCKS = 1
FFN_PIECES_AT = (1, 1, 1, (1,), 1, 1, 1, 1)
HBM_WEIGHT_OPERANDS = (2, 14, 16, 17, 18)
WEIGHT_STAGE_ROWS = 256
WEIGHT_STAGE_SLOTS = 3
VMEM_LIMIT_BYTES = 58 * 1024 * 1024


def _ffn_sequence():
    ops = []
    for c in range(N_FF_CHUNKS):
        ops.append(("u", c))
        if c % FF_DOWN_GROUP == 0 and c > 0:
            ops.append(("d", c - FF_DOWN_GROUP))
    last = (N_FF_CHUNKS - 1) // FF_DOWN_GROUP * FF_DOWN_GROUP
    return ops + [("d", last)]


def _ff_chunk_cols():
    return tuple((c * FF_CHUNK, min((c + 1) * FF_CHUNK, D_FF)) for c in range(N_FF_CHUNKS))


def _rms(x, g):
    ms = jnp.mean(x * x, axis=-1, keepdims=True)
    return x * lax.rsqrt(ms + EPS) * g


def _sigmoid(x):
    return 0.5 * (jnp.tanh(0.5 * x) + 1.0)


def _gelu_tanh_times(x, h):
    c = math.sqrt(2.0 / math.pi)
    t = jnp.tanh(x * ((x * x) * (0.044715 * c) + c))
    return (h * x) * (0.5 * t + 0.5)


def _dot(a, b):
    return jnp.dot(a, b, preferred_element_type=jnp.float32)


def _load_weights_bf16(weights, stage, sems):
    chunks = [(src, dst, scale, col_map or ((0, src.shape[1], 0),), r * WEIGHT_STAGE_ROWS)
              for src, dst, scale, col_map in weights for r in range(src.shape[0] // WEIGHT_STAGE_ROWS)]

    def copy(k):
        src, _, _, _, row0 = chunks[k]
        slot = k % WEIGHT_STAGE_SLOTS
        return pltpu.make_async_copy(src.at[pl.ds(row0, WEIGHT_STAGE_ROWS), :],
                                     stage.at[slot, :, pl.ds(0, src.shape[1])], sems.at[slot])

    ahead = WEIGHT_STAGE_SLOTS - 1
    for k in range(min(ahead, len(chunks))):
        copy(k).start(priority=k % 2)
    for k, (_, dst, scale, col_map, row0) in enumerate(chunks):
        if k + ahead < len(chunks):
            copy(k + ahead).start(priority=(k + ahead) % 2)
        copy(k).wait()
        for c0, c1, d0 in col_map:
            w = stage[k % WEIGHT_STAGE_SLOTS, :, c0:c1]
            dst[row0:row0 + WEIGHT_STAGE_ROWS, d0:d0 + c1 - c0] = (
                w if scale is None else w * scale).astype(jnp.bfloat16)


def _build_small_weights(pool_w_ref, w_a_ref, w_i_ref, small_w_ref):
    half = LRU_WIDTH // 2
    heads = half // LRU_HEAD_DIM
    bf16 = jnp.bfloat16
    c_idx = lax.broadcasted_iota(jnp.int32, (LRU_HEAD_DIM, half), 0)
    j_idx = lax.broadcasted_iota(jnp.int32, (LRU_HEAD_DIM, half), 1)
    spread = (lax.rem(j_idx, LRU_HEAD_DIM) == c_idx).astype(bf16)
    col_head = j_idx // LRU_HEAD_DIM
    for q in range(2):
        for part, w_ref in enumerate((w_a_ref, w_i_ref)):
            for h in range(heads):
                rep = _dot(w_ref[0, heads * q + h].astype(bf16), spread)
                blk = jnp.where(col_head == h, rep, 0.0).astype(bf16)
                small_w_ref[q, h * LRU_HEAD_DIM:(h + 1) * LRU_HEAD_DIM, part * half:(part + 1) * half] = blk
        small_w_ref[q, :, 2 * half:3 * half] = jnp.zeros((half, half), bf16)
        for g in range(2):
            rows = slice(g * POOL_GROUP, (g + 1) * POOL_GROUP)
            cols = slice(2 * half + g * POOL_GROUP, 2 * half + (g + 1) * POOL_GROUP)
            small_w_ref[q, rows, cols] = pool_w_ref[0, 2 * q + g].astype(bf16)


def _layer_kernel(x_ref, ln1_ref, w_in_hbm, pool_w_ref, pool_scale_ref, conv_w_ref, conv_b_ref,
                  w_a_ref, b_a_ref, w_i_ref, b_i_ref, lam_ref, gn_pool_ref, gn_lru_ref, w_out_hbm,
                  ln2_ref, wg_hbm, wu_hbm, wd_hbm, lnf_ref, o_ref,
                  pbuf, lbuf, abuf, bbuf, hbuf, hcarry, h1buf, n2buf,
                  w_in_ref, w_out_ref, wgu_ref, wd_ref, small_w_ref, stage, stage_sems,
                  *, chunks_per_seq, n_chunks):
    i = pl.program_id(0)
    s = lax.rem(i, chunks_per_seq)
    T = SEQ_TILE
    half = LRU_WIDTH // 2

    @pl.when(i == 0)
    def _():
        g_map = tuple((c0, c1, 2 * c0) for c0, c1 in _ff_chunk_cols())
        u_map = tuple((c0, c1, c0 + c1) for c0, c1 in _ff_chunk_cols())
        _load_weights_bf16(((w_in_hbm, w_in_ref, None, None), (w_out_hbm, w_out_ref, None, None),
                            (wg_hbm, wgu_ref, 0.5, g_map), (wu_hbm, wgu_ref, None, u_map),
                            (wd_hbm, wd_ref, None, None)), stage, stage_sems)
        _build_small_weights(pool_w_ref, w_a_ref, w_i_ref, small_w_ref)

    slot = lax.rem(i, 2)
    prev = 1 - slot

    def mixer():
        @pl.when(s == 0)
        def _():
            pbuf[0:POOL_TAIL, :] = jnp.zeros((POOL_TAIL, POOL_WIDTH), jnp.float32)
            lbuf[0:CONV_TAIL, :] = jnp.zeros((CONV_TAIL, LRU_WIDTH), jnp.float32)
            hcarry[...] = jnp.zeros_like(hcarry)

        yield FFN_PIECES_AT[0]
        nb = _rms(x_ref[...], ln1_ref[...]).astype(jnp.bfloat16)

        lbuf[CONV_TAIL:CONV_TAIL + T, :] = _dot(nb, w_in_ref[:, POOL_WIDTH:POOL_WIDTH + LRU_WIDTH])
        u_gate = _dot(nb, w_in_ref[:, POOL_WIDTH + LRU_WIDTH:])

        yield FFN_PIECES_AT[2]
        assert CONV_WIDTH == 4
        u_ext = lbuf[...]
        u_prev = pltpu.roll(u_ext, 1, axis=0)
        tap = lambda k: conv_w_ref[0, k:k + 1, :]
        older = tap(1) * u_ext + tap(0) * u_prev
        xc = (conv_b_ref[...] + tap(3) * u_ext + tap(2) * u_prev + pltpu.roll(older, 2, axis=0))[CONV_TAIL:]
        xcb = xc.astype(jnp.bfloat16)
        neg_lam = -lam_ref[...]
        softplus = jnp.maximum(neg_lam, 0.0) + jnp.log1p(jnp.exp(-jnp.abs(neg_lam)))
        k_row = (0.5 * LRU_C) * softplus

        nq = LRU_WIDTH // LANES
        tiles_per_sub = SUB_LEN // SUBLANES
        blk_rows = T // GATE_BLOCKS
        for blk in range(GATE_BLOCKS):
            rows_blk = slice(blk * blk_rows, (blk + 1) * blk_rows)
            z = [_dot(xcb[rows_blk, q * half:(q + 1) * half], small_w_ref[q, :, 0:2 * half])
                 for q in range(2)]
            yield FFN_PIECES_AT[3][blk]
            za = jnp.concatenate([z[0][:, :half], z[1][:, :half]], axis=-1) + b_a_ref[...]
            zi = jnp.concatenate([z[0][:, half:], z[1][:, half:]], axis=-1) + b_i_ref[...]
            neg_log_a = jnp.tanh(0.5 * za) * k_row + k_row
            a = jnp.exp(-neg_log_a)
            one_minus_a2 = jnp.tanh(neg_log_a) * (a * a + 1.0)
            mult = jnp.where(one_minus_a2 > 0.0, one_minus_a2 * lax.rsqrt(one_minus_a2), 0.0)
            bterm = mult * (_sigmoid(zi) * xc[rows_blk, :])
            for q in range(nq):
                lanes = slice(q * LANES, (q + 1) * LANES)
                for t in range(blk_rows // SUBLANES):
                    j, m0 = divmod(blk * (blk_rows // SUBLANES) + t, tiles_per_sub)
                    dst = pl.ds(m0 * SUBLANES * SUBLANES + j, SUBLANES, stride=SUBLANES)
                    rows = slice(t * SUBLANES, (t + 1) * SUBLANES)
                    abuf[q, dst, :] = a[rows, lanes]
                    bbuf[q, dst, :] = bterm[rows, lanes]

        def tile(m):
            return slice(m * SUBLANES, (m + 1) * SUBLANES)

        yield FFN_PIECES_AT[4]
        prod = [jnp.ones((SUBLANES, LANES), jnp.float32) for _ in range(nq)]
        loc = [jnp.zeros((SUBLANES, LANES), jnp.float32) for _ in range(nq)]
        for m in range(SUB_LEN):
            for q in range(nq):
                am = abuf[q, tile(m), :]
                loc[q] = am * loc[q] + bbuf[q, tile(m), :]
                prod[q] = am * prod[q]
        row = lax.broadcasted_iota(jnp.int32, (SUBLANES, LANES), 0)
        h_in = []
        for q in range(nq):
            carry = hcarry[:, q * LANES:(q + 1) * LANES]
            start = jnp.zeros((SUBLANES, LANES), jnp.float32)
            for j in range(SUBLANES):
                start = jnp.where(row == j, carry, start)
                carry = prod[q][j:j + 1, :] * carry + loc[q][j:j + 1, :]
            h_in.append(start)
            hcarry[:, q * LANES:(q + 1) * LANES] = carry
        yield FFN_PIECES_AT[5]
        for m in range(SUB_LEN):
            for q in range(nq):
                h_in[q] = abuf[q, tile(m), :] * h_in[q] + bbuf[q, tile(m), :]
                hbuf[q, pl.ds(m, SUBLANES, stride=SUB_PITCH), :] = h_in[q]
        h_lru = jnp.concatenate(
            [jnp.concatenate([hbuf[q, j * SUB_PITCH:j * SUB_PITCH + SUB_LEN, :] for j in range(SUBLANES)], axis=0)
             for q in range(nq)], axis=-1)

        pbuf[POOL_TAIL:POOL_TAIL + T, :] = _dot(nb, w_in_ref[:, 0:POOL_WIDTH])
        yield FFN_PIECES_AT[1]
        n_groups = len(POOL_WINDOWS)
        level = pbuf[...]
        sums = []
        for g, w in enumerate(POOL_WINDOWS):
            assert w == 2 << g and w - 1 <= POOL_TAIL
            rest = level[:, (0 if g == 0 else POOL_GROUP):]
            rest = rest + pltpu.roll(rest, w // 2, axis=0)
            sums.append(rest[POOL_TAIL:, 0:POOL_GROUP])
            level = rest
        head_pos = (s * T + 1 + lax.broadcasted_iota(jnp.int32, (POOL_TAIL, POOL_GROUP), 0)).astype(jnp.float32)
        d_parts = []
        for g, w in enumerate(POOL_WINDOWS):
            ug = pbuf[POOL_TAIL:POOL_TAIL + T, g * POOL_GROUP:(g + 1) * POOL_GROUP]
            head = sums[g][0:POOL_TAIL] / jnp.minimum(head_pos, float(w))
            mean = jnp.concatenate([head, sums[g][POOL_TAIL:] * (1.0 / w)], axis=0)
            d_parts.append((mean - ug).astype(jnp.bfloat16))
        y_parts = []
        for p in range(n_groups // 2):
            d_pair = jnp.concatenate(d_parts[2 * p:2 * p + 2], axis=-1)
            y_parts.append(_dot(d_pair, small_w_ref[p, :, 2 * half:3 * half]))
        y_pool = jnp.concatenate(y_parts, axis=-1) * pool_scale_ref[...]
        mix_pool = _rms(y_pool, gn_pool_ref[...]).astype(jnp.bfloat16)

        yield FFN_PIECES_AT[6]
        y_lru = _gelu_tanh_times(u_gate, h_lru)
        mix_lru = _rms(y_lru, gn_lru_ref[...]).astype(jnp.bfloat16)

        mix = jnp.concatenate([mix_pool, mix_lru], axis=-1)
        h1_new = x_ref[...] + _dot(mix, w_out_ref[...])
        yield FFN_PIECES_AT[7]
        h1buf[slot] = h1_new
        n2buf[slot] = _rms(h1_new, ln2_ref[...]).astype(jnp.bfloat16)

        pbuf[0:POOL_TAIL, :] = pbuf[T:T + POOL_TAIL, :]
        lbuf[0:CONV_TAIL, :] = lbuf[T:T + CONV_TAIL, :]

    def make_ffn():
        acts = {}
        ffn_parts = []
        pending = _ffn_sequence()

        def ffn(n):
            for _ in range(n):
                op, c = pending.pop(0)
                if op == "u":
                    c0, c1 = _ff_chunk_cols()[c]
                    n2 = n2buf[prev]
                    gu = _dot(n2, wgu_ref[:, 2 * c0:2 * c1])
                    half_g, up = gu[:, :c1 - c0], gu[:, c1 - c0:]
                    acts[c] = (half_g * (jnp.tanh(half_g) + 1.0) * up).astype(jnp.bfloat16)
                else:
                    group = range(c, min(c + FF_DOWN_GROUP, N_FF_CHUNKS))
                    act = jnp.concatenate([acts.pop(k) for k in group], axis=-1)
                    ffn_parts.append(_dot(act, wd_ref[c * FF_CHUNK:min((group[-1] + 1) * FF_CHUNK, D_FF), :]))

        def finish():
            ffn(len(pending))
            ffn_out = ffn_parts[0]
            for part in ffn_parts[1:]:
                ffn_out = ffn_out + part
            o_ref[...] = _rms(h1buf[prev] + ffn_out, lnf_ref[...])

        return ffn, finish

    @pl.when(i == 0)
    def _():
        for _ in mixer():
            pass

    @pl.when(jnp.logical_and(i > 0, i < n_chunks))
    def _():
        ffn, finish = make_ffn()
        for n in mixer():
            ffn(n)
        finish()

    @pl.when(i == n_chunks)
    def _():
        _, finish = make_ffn()
        finish()


def kernel(x, ln1_g, w_in, pool_w, pool_scale, conv_w, conv_b, w_a, b_a, w_i, b_i, lam, gn_pool_g,
           gn_lru_g, w_out, ln2_g, w_ffn_gate, w_ffn_up, w_ffn_down, lnf_g):
    B, S, D = x.shape
    assert D == D_MODEL and S % SEQ_TILE == 0 and ln1_g.shape[0] == 1
    bf16 = jnp.bfloat16
    vmem = pl.BlockSpec(memory_space=pltpu.VMEM)
    hbm = pl.BlockSpec(memory_space=pl.ANY)
    nq = LRU_WIDTH // LANES
    scan_rows = SUBLANES * SUB_PITCH
    cps = S // SEQ_TILE
    n_chunks = B * cps

    def in_index(i):
        c = jnp.minimum(i, n_chunks - 1)
        return (c // cps, c % cps, 0)

    def out_index(i):
        c = jnp.maximum(i - 1, 0)
        return (c // cps, c % cps, 0)

    out = pl.pallas_call(
        functools.partial(_layer_kernel, chunks_per_seq=cps, n_chunks=n_chunks),
        grid=(n_chunks + 1,),
        in_specs=[pl.BlockSpec((None, SEQ_TILE, D), in_index)] + [
            hbm if k in HBM_WEIGHT_OPERANDS else vmem for k in range(1, 20)],
        out_specs=pl.BlockSpec((None, SEQ_TILE, D), out_index),
        out_shape=jax.ShapeDtypeStruct((B, S, D), jnp.float32),
        scratch_shapes=[
            pltpu.VMEM((POOL_TAIL + SEQ_TILE, POOL_WIDTH), jnp.float32),
            pltpu.VMEM((CONV_TAIL + SEQ_TILE, LRU_WIDTH), jnp.float32),
            pltpu.VMEM((nq, SEQ_TILE, LANES), jnp.float32),
            pltpu.VMEM((nq, SEQ_TILE, LANES), jnp.float32),
            pltpu.VMEM((nq, scan_rows, LANES), jnp.float32),
            pltpu.VMEM((1, LRU_WIDTH), jnp.float32),
            pltpu.VMEM((2, SEQ_TILE, D), jnp.float32),
            pltpu.VMEM((2, SEQ_TILE, D), jnp.bfloat16),
            pltpu.VMEM((D, IN_WIDTH), bf16),
            pltpu.VMEM((D, D), bf16),
            pltpu.VMEM((D, 2 * D_FF), bf16),
            pltpu.VMEM((D_FF, D), bf16),
            pltpu.VMEM((2, LRU_WIDTH // 2, LRU_WIDTH + LRU_WIDTH // 2), bf16),
            pltpu.VMEM((WEIGHT_STAGE_SLOTS, WEIGHT_STAGE_ROWS, D_FF), jnp.float32),
            pltpu.SemaphoreType.DMA((WEIGHT_STAGE_SLOTS,)),
        ],
        compiler_params=pltpu.CompilerParams(
            dimension_semantics=("arbitrary",),
            vmem_limit_bytes=VMEM_LIMIT_BYTES),
        name="hymba_layer",
    )(x, ln1_g, w_in[0], pool_w, pool_scale, conv_w, conv_b, w_a, b_a, w_i, b_i, lam, gn_pool_g, gn_lru_g,
      w_out[0], ln2_g, w_ffn_gate[0], w_ffn_up[0], w_ffn_down[0], lnf_g.reshape(1, D))
    return out
```

```python
import functools
import math

import jax
import jax.numpy as jnp
from jax import lax
from jax.experimental import pallas as pl
from jax.experimental.pallas import tpu as pltpu

D_MODEL = 1024
POOL_WINDOWS = (2, 4, 8, 16)
POOL_GROUP = 128
POOL_WIDTH = 512
LRU_WIDTH = 512
LRU_HEADS = 8
LRU_HEAD_DIM = 64
CONV_WIDTH = 4
LRU_C = 8.0
D_FF = 2816
IN_WIDTH = POOL_WIDTH + 2 * LRU_WIDTH
EPS = 1e-6

LANES = 128
SUBLANES = 8
SEQ_TILE = 512
POOL_TAIL = 16
CONV_TAIL = 8
SUB_LEN = SEQ_TILE // SUBLANES
SUB_PITCH = SUB_LEN + SUBLANES
FF_CHUNK = 512
N_FF_CHUNKS = -(-D_FF // FF_CHUNK)
FF_DOWN_GROUP = 3
GATE_BLOCKS = 1
FFN_PIECES_AT = (1, 1, 1, (1,), 1, 1, 1, 1)
HBM_WEIGHT_OPERANDS = (2, 14, 16, 17, 18)
WEIGHT_STAGE_ROWS = 64
WEIGHT_STAGE_SLOTS = 12
VMEM_LIMIT_BYTES = 58 * 1024 * 1024


def _ffn_sequence():
    ops = []
    for c in range(N_FF_CHUNKS):
        ops.append(("u", c))
        if c % FF_DOWN_GROUP == 0 and c > 0:
            ops.append(("d", c - FF_DOWN_GROUP))
    last = (N_FF_CHUNKS - 1) // FF_DOWN_GROUP * FF_DOWN_GROUP
    return ops + [("d", last)]


def _ff_chunk_cols():
    return tuple((c * FF_CHUNK, min((c + 1) * FF_CHUNK, D_FF)) for c in range(N_FF_CHUNKS))


def _rms(x, g):
    ms = jnp.mean(x * x, axis=-1, keepdims=True)
    return x * lax.rsqrt(ms + EPS) * g


def _sigmoid(x):
    return 0.5 * (jnp.tanh(0.5 * x) + 1.0)


def _gelu_tanh_times(x, h):
    c = math.sqrt(2.0 / math.pi)
    t = jnp.tanh(x * ((x * x) * (0.044715 * c) + c))
    return (h * x) * (0.5 * t + 0.5)


def _dot(a, b):
    return jnp.dot(a, b, preferred_element_type=jnp.float32)


def _load_weights_bf16(weights, stage, sems):
    chunks = [(src, dst, scale, col_map or ((0, src.shape[1], 0),), r * WEIGHT_STAGE_ROWS)
              for src, dst, scale, col_map in weights for r in range(src.shape[0] // WEIGHT_STAGE_ROWS)]

    def copy(k):
        src, _, _, _, row0 = chunks[k]
        slot = k % WEIGHT_STAGE_SLOTS
        return pltpu.make_async_copy(src.at[pl.ds(row0, WEIGHT_STAGE_ROWS), :],
                                     stage.at[slot, :, pl.ds(0, src.shape[1])], sems.at[slot])

    ahead = WEIGHT_STAGE_SLOTS - 1
    for k in range(min(ahead, len(chunks))):
        copy(k).start(priority=k % 2)
    for k, (_, dst, scale, col_map, row0) in enumerate(chunks):
        if k + ahead < len(chunks):
            copy(k + ahead).start(priority=(k + ahead) % 2)
        copy(k).wait()
        for c0, c1, d0 in col_map:
            w = stage[k % WEIGHT_STAGE_SLOTS, :, c0:c1]
            dst[row0:row0 + WEIGHT_STAGE_ROWS, d0:d0 + c1 - c0] = (
                w if scale is None else w * scale).astype(jnp.bfloat16)


def _build_small_weights(pool_w_ref, w_a_ref, w_i_ref, small_w_ref):
    half = LRU_WIDTH // 2
    heads = half // LRU_HEAD_DIM
    bf16 = jnp.bfloat16
    c_idx = lax.broadcasted_iota(jnp.int32, (LRU_HEAD_DIM, half), 0)
    j_idx = lax.broadcasted_iota(jnp.int32, (LRU_HEAD_DIM, half), 1)
    spread = (lax.rem(j_idx, LRU_HEAD_DIM) == c_idx).astype(bf16)
    col_head = j_idx // LRU_HEAD_DIM
    for q in range(2):
        for part, w_ref in enumerate((w_a_ref, w_i_ref)):
            for h in range(heads):
                rep = _dot(w_ref[0, heads * q + h].astype(bf16), spread)
                blk = jnp.where(col_head == h, rep, 0.0).astype(bf16)
                small_w_ref[q, h * LRU_HEAD_DIM:(h + 1) * LRU_HEAD_DIM, part * half:(part + 1) * half] = blk
        small_w_ref[q, :, 2 * half:3 * half] = jnp.zeros((half, half), bf16)
        for g in range(2):
            rows = slice(g * POOL_GROUP, (g + 1) * POOL_GROUP)
            cols = slice(2 * half + g * POOL_GROUP, 2 * half + (g + 1) * POOL_GROUP)
            small_w_ref[q, rows, cols] = pool_w_ref[0, 2 * q + g].astype(bf16)


def _layer_kernel(x_ref, ln1_ref, w_in_hbm, pool_w_ref, pool_scale_ref, conv_w_ref, conv_b_ref,
                  w_a_ref, b_a_ref, w_i_ref, b_i_ref, lam_ref, gn_pool_ref, gn_lru_ref, w_out_hbm,
                  ln2_ref, wg_hbm, wu_hbm, wd_hbm, lnf_ref, o_ref,
                  pbuf, lbuf, abuf, bbuf, hbuf, hcarry, h1buf, n2buf,
                  w_in_ref, w_out_ref, wgu_ref, wd_ref, small_w_ref, stage, stage_sems,
                  *, chunks_per_seq, n_chunks):
    i = pl.program_id(0)
    s = lax.rem(i, chunks_per_seq)
    T = SEQ_TILE
    half = LRU_WIDTH // 2

    @pl.when(i == 0)
    def _():
        g_map = tuple((c0, c1, 2 * c0) for c0, c1 in _ff_chunk_cols())
        u_map = tuple((c0, c1, c0 + c1) for c0, c1 in _ff_chunk_cols())
        _load_weights_bf16(((w_in_hbm, w_in_ref, None, None), (w_out_hbm, w_out_ref, None, None),
                            (wg_hbm, wgu_ref, 0.5, g_map), (wu_hbm, wgu_ref, None, u_map),
                            (wd_hbm, wd_ref, None, None)), stage, stage_sems)
        _build_small_weights(pool_w_ref, w_a_ref, w_i_ref, small_w_ref)

    slot = lax.rem(i, 2)
    prev = 1 - slot

    def mixer():
        @pl.when(s == 0)
        def _():
            pbuf[0:POOL_TAIL, :] = jnp.zeros((POOL_TAIL, POOL_WIDTH), jnp.float32)
            lbuf[0:CONV_TAIL, :] = jnp.zeros((CONV_TAIL, LRU_WIDTH), jnp.float32)
            hcarry[...] = jnp.zeros_like(hcarry)

        yield FFN_PIECES_AT[0]
        nb = _rms(x_ref[...], ln1_ref[...]).astype(jnp.bfloat16)

        lbuf[CONV_TAIL:CONV_TAIL + T, :] = _dot(nb, w_in_ref[:, POOL_WIDTH:POOL_WIDTH + LRU_WIDTH])
        u_gate = _dot(nb, w_in_ref[:, POOL_WIDTH + LRU_WIDTH:])

        yield FFN_PIECES_AT[2]
        assert CONV_WIDTH == 4
        u_ext = lbuf[...]
        u_prev = pltpu.roll(u_ext, 1, axis=0)
        tap = lambda k: conv_w_ref[0, k:k + 1, :]
        older = tap(1) * u_ext + tap(0) * u_prev
        xc = (conv_b_ref[...] + tap(3) * u_ext + tap(2) * u_prev + pltpu.roll(older, 2, axis=0))[CONV_TAIL:]
        xcb = xc.astype(jnp.bfloat16)
        neg_lam = -lam_ref[...]
        softplus = jnp.maximum(neg_lam, 0.0) + jnp.log1p(jnp.exp(-jnp.abs(neg_lam)))
        k_row = (0.5 * LRU_C) * softplus

        nq = LRU_WIDTH // LANES
        tiles_per_sub = SUB_LEN // SUBLANES
        blk_rows = T // GATE_BLOCKS
        for blk in range(GATE_BLOCKS):
            rows_blk = slice(blk * blk_rows, (blk + 1) * blk_rows)
            z = [_dot(xcb[rows_blk, q * half:(q + 1) * half], small_w_ref[q, :, 0:2 * half])
                 for q in range(2)]
            yield FFN_PIECES_AT[3][blk]
            za = jnp.concatenate([z[0][:, :half], z[1][:, :half]], axis=-1) + b_a_ref[...]
            zi = jnp.concatenate([z[0][:, half:], z[1][:, half:]], axis=-1) + b_i_ref[...]
            neg_log_a = jnp.tanh(0.5 * za) * k_row + k_row
            a = jnp.exp(-neg_log_a)
            one_minus_a2 = jnp.tanh(neg_log_a) * (a * a + 1.0)
            mult = jnp.where(one_minus_a2 > 0.0, one_minus_a2 * lax.rsqrt(one_minus_a2), 0.0)
            bterm = mult * (_sigmoid(zi) * xc[rows_blk, :])
            for q in range(nq):
                lanes = slice(q * LANES, (q + 1) * LANES)
                for t in range(blk_rows // SUBLANES):
                    j, m0 = divmod(blk * (blk_rows // SUBLANES) + t, tiles_per_sub)
                    dst = pl.ds(m0 * SUBLANES * SUBLANES + j, SUBLANES, stride=SUBLANES)
                    rows = slice(t * SUBLANES, (t + 1) * SUBLANES)
                    abuf[q, dst, :] = a[rows, lanes]
                    bbuf[q, dst, :] = bterm[rows, lanes]

        def tile(m):
            return slice(m * SUBLANES, (m + 1) * SUBLANES)

        yield FFN_PIECES_AT[4]
        prod = [jnp.ones((SUBLANES, LANES), jnp.float32) for _ in range(nq)]
        loc = [jnp.zeros((SUBLANES, LANES), jnp.float32) for _ in range(nq)]
        for m in range(SUB_LEN):
            for q in range(nq):
                am = abuf[q, tile(m), :]
                loc[q] = am * loc[q] + bbuf[q, tile(m), :]
                prod[q] = am * prod[q]
        row = lax.broadcasted_iota(jnp.int32, (SUBLANES, LANES), 0)
        h_in = []
        for q in range(nq):
            carry = hcarry[:, q * LANES:(q + 1) * LANES]
            start = jnp.zeros((SUBLANES, LANES), jnp.float32)
            for j in range(SUBLANES):
                start = jnp.where(row == j, carry, start)
                carry = prod[q][j:j + 1, :] * carry + loc[q][j:j + 1, :]
            h_in.append(start)
            hcarry[:, q * LANES:(q + 1) * LANES] = carry
        yield FFN_PIECES_AT[5]
        for m in range(SUB_LEN):
            for q in range(nq):
                h_in[q] = abuf[q, tile(m), :] * h_in[q] + bbuf[q, tile(m), :]
                hbuf[q, pl.ds(m, SUBLANES, stride=SUB_PITCH), :] = h_in[q]
        h_lru = jnp.concatenate(
            [jnp.concatenate([hbuf[q, j * SUB_PITCH:j * SUB_PITCH + SUB_LEN, :] for j in range(SUBLANES)], axis=0)
             for q in range(nq)], axis=-1)

        pbuf[POOL_TAIL:POOL_TAIL + T, :] = _dot(nb, w_in_ref[:, 0:POOL_WIDTH])
        yield FFN_PIECES_AT[1]
        n_groups = len(POOL_WINDOWS)
        level = pbuf[...]
        sums = []
        for g, w in enumerate(POOL_WINDOWS):
            assert w == 2 << g and w - 1 <= POOL_TAIL
            rest = level[:, (0 if g == 0 else POOL_GROUP):]
            rest = rest + pltpu.roll(rest, w // 2, axis=0)
            sums.append(rest[POOL_TAIL:, 0:POOL_GROUP])
            level = rest
        head_pos = (s * T + 1 + lax.broadcasted_iota(jnp.int32, (POOL_TAIL, POOL_GROUP), 0)).astype(jnp.float32)
        d_parts = []
        for g, w in enumerate(POOL_WINDOWS):
            ug = pbuf[POOL_TAIL:POOL_TAIL + T, g * POOL_GROUP:(g + 1) * POOL_GROUP]
            head = sums[g][0:POOL_TAIL] / jnp.minimum(head_pos, float(w))
            mean = jnp.concatenate([head, sums[g][POOL_TAIL:] * (1.0 / w)], axis=0)
            d_parts.append((mean - ug).astype(jnp.bfloat16))
        y_parts = []
        for p in range(n_groups // 2):
            d_pair = jnp.concatenate(d_parts[2 * p:2 * p + 2], axis=-1)
            y_parts.append(_dot(d_pair, small_w_ref[p, :, 2 * half:3 * half]))
        y_pool = jnp.concatenate(y_parts, axis=-1) * pool_scale_ref[...]
        mix_pool = _rms(y_pool, gn_pool_ref[...]).astype(jnp.bfloat16)

        yield FFN_PIECES_AT[6]
        y_lru = _gelu_tanh_times(u_gate, h_lru)
        mix_lru = _rms(y_lru, gn_lru_ref[...]).astype(jnp.bfloat16)

        mix = jnp.concatenate([mix_pool, mix_lru], axis=-1)
        h1_new = x_ref[...] + _dot(mix, w_out_ref[...])
        yield FFN_PIECES_AT[7]
        h1buf[slot] = h1_new
        n2buf[slot] = _rms(h1_new, ln2_ref[...]).astype(jnp.bfloat16)

        pbuf[0:POOL_TAIL, :] = pbuf[T:T + POOL_TAIL, :]
        lbuf[0:CONV_TAIL, :] = lbuf[T:T + CONV_TAIL, :]

    def make_ffn():
        acts = {}
        ffn_parts = []
        pending = _ffn_sequence()

        def ffn(n):
            for _ in range(n):
                op, c = pending.pop(0)
                if op == "u":
                    c0, c1 = _ff_chunk_cols()[c]
                    n2 = n2buf[prev]
                    gu = _dot(n2, wgu_ref[:, 2 * c0:2 * c1])
                    half_g, up = gu[:, :c1 - c0], gu[:, c1 - c0:]
                    acts[c] = (half_g * (jnp.tanh(half_g) + 1.0) * up).astype(jnp.bfloat16)
                else:
                    group = range(c, min(c + FF_DOWN_GROUP, N_FF_CHUNKS))
                    act = jnp.concatenate([acts.pop(k) for k in group], axis=-1)
                    ffn_parts.append(_dot(act, wd_ref[c * FF_CHUNK:min((group[-1] + 1) * FF_CHUNK, D_FF), :]))

        def finish():
            ffn(len(pending))
            ffn_out = ffn_parts[0]
            for part in ffn_parts[1:]:
                ffn_out = ffn_out + part
            o_ref[...] = _rms(h1buf[prev] + ffn_out, lnf_ref[...])

        return ffn, finish

    @pl.when(i == 0)
    def _():
        for _ in mixer():
            pass

    @pl.when(jnp.logical_and(i > 0, i < n_chunks))
    def _():
        ffn, finish = make_ffn()
        for n in mixer():
            ffn(n)
        finish()

    @pl.when(i == n_chunks)
    def _():
        _, finish = make_ffn()
        finish()


def kernel(x, ln1_g, w_in, pool_w, pool_scale, conv_w, conv_b, w_a, b_a, w_i, b_i, lam, gn_pool_g,
           gn_lru_g, w_out, ln2_g, w_ffn_gate, w_ffn_up, w_ffn_down, lnf_g):
    B, S, D = x.shape
    assert D == D_MODEL and S % SEQ_TILE == 0 and ln1_g.shape[0] == 1
    bf16 = jnp.bfloat16
    vmem = pl.BlockSpec(memory_space=pltpu.VMEM)
    hbm = pl.BlockSpec(memory_space=pl.ANY)
    nq = LRU_WIDTH // LANES
    scan_rows = SUBLANES * SUB_PITCH
    cps = S // SEQ_TILE
    n_chunks = B * cps

    def in_index(i):
        c = jnp.minimum(i, n_chunks - 1)
        return (c // cps, c % cps, 0)

    def out_index(i):
        c = jnp.maximum(i - 1, 0)
        return (c // cps, c % cps, 0)

    out = pl.pallas_call(
        functools.partial(_layer_kernel, chunks_per_seq=cps, n_chunks=n_chunks),
        grid=(n_chunks + 1,),
        in_specs=[pl.BlockSpec((None, SEQ_TILE, D), in_index)] + [
            hbm if k in HBM_WEIGHT_OPERANDS else vmem for k in range(1, 20)],
        out_specs=pl.BlockSpec((None, SEQ_TILE, D), out_index),
        out_shape=jax.ShapeDtypeStruct((B, S, D), jnp.float32),
        scratch_shapes=[
            pltpu.VMEM((POOL_TAIL + SEQ_TILE, POOL_WIDTH), jnp.float32),
            pltpu.VMEM((CONV_TAIL + SEQ_TILE, LRU_WIDTH), jnp.float32),
            pltpu.VMEM((nq, SEQ_TILE, LANES), jnp.float32),
            pltpu.VMEM((nq, SEQ_TILE, LANES), jnp.float32),
            pltpu.VMEM((nq, scan_rows, LANES), jnp.float32),
            pltpu.VMEM((1, LRU_WIDTH), jnp.float32),
            pltpu.VMEM((2, SEQ_TILE, D), jnp.float32),
            pltpu.VMEM((2, SEQ_TILE, D), jnp.bfloat16),
            pltpu.VMEM((D, IN_WIDTH), bf16),
            pltpu.VMEM((D, D), bf16),
            pltpu.VMEM((D, 2 * D_FF), bf16),
            pltpu.VMEM((D_FF, D), bf16),
            pltpu.VMEM((2, LRU_WIDTH // 2, LRU_WIDTH + LRU_WIDTH // 2), bf16),
            pltpu.VMEM((WEIGHT_STAGE_SLOTS, WEIGHT_STAGE_ROWS, D_FF), jnp.float32),
            pltpu.SemaphoreType.DMA((WEIGHT_STAGE_SLOTS,)),
        ],
        compiler_params=pltpu.CompilerParams(
            dimension_semantics=("arbitrary",),
            vmem_limit_bytes=VMEM_LIMIT_BYTES),
        name="hymba_layer",
    )(x, ln1_g, w_in[0], pool_w, pool_scale, conv_w, conv_b, w_a, b_a, w_i, b_i, lam, gn_pool_g, gn_lru_g,
      w_out[0], ln2_g, w_ffn_gate[0], w_ffn_up[0], w_ffn_down[0], lnf_g.reshape(1, D))
    return out
```

```python
import functools
import math

import jax
import jax.numpy as jnp
from jax import lax
from jax.experimental import pallas as pl
from jax.experimental.pallas import tpu as pltpu

D_MODEL = 1024
POOL_WINDOWS = (2, 4, 8, 16)
POOL_GROUP = 128
POOL_WIDTH = 512
LRU_WIDTH = 512
LRU_HEADS = 8
LRU_HEAD_DIM = 64
CONV_WIDTH = 4
LRU_C = 8.0
D_FF = 2816
IN_WIDTH = POOL_WIDTH + 2 * LRU_WIDTH
EPS = 1e-6

LANES = 128
SUBLANES = 8
SEQ_TILE = 512
POOL_TAIL = 16
CONV_TAIL = 8
SUB_LEN = SEQ_TILE // SUBLANES
SUB_PITCH = SUB_LEN + SUBLANES
FF_CHUNK = 512
N_FF_CHUNKS = -(-D_FF // FF_CHUNK)
FF_DOWN_GROUP = 6
GATE_BLOCKS = 1
FFN_PIECES_AT = (1, 1, 1, (1,), 1, 1, 0, 1)
HBM_WEIGHT_OPERANDS = (2, 14, 16, 17, 18)
WEIGHT_STAGE_ROWS = 128
WEIGHT_STAGE_SLOTS = 6
VMEM_LIMIT_BYTES = 58 * 1024 * 1024


def _ffn_sequence():
    ops = []
    for c in range(N_FF_CHUNKS):
        ops.append(("u", c))
        if c % FF_DOWN_GROUP == 0 and c > 0:
            ops.append(("d", c - FF_DOWN_GROUP))
    last = (N_FF_CHUNKS - 1) // FF_DOWN_GROUP * FF_DOWN_GROUP
    return ops + [("d", last)]


def _ff_chunk_cols():
    return tuple((c * FF_CHUNK, min((c + 1) * FF_CHUNK, D_FF)) for c in range(N_FF_CHUNKS))


def _rms(x, g):
    ms = jnp.mean(x * x, axis=-1, keepdims=True)
    return x * lax.rsqrt(ms + EPS) * g


def _sigmoid(x):
    return 0.5 * (jnp.tanh(0.5 * x) + 1.0)


def _gelu_tanh_times(x, h):
    c = math.sqrt(2.0 / math.pi)
    t = jnp.tanh(x * ((x * x) * (0.044715 * c) + c))
    return (h * x) * (0.5 * t + 0.5)


def _dot(a, b):
    return jnp.dot(a, b, preferred_element_type=jnp.float32)


def _load_weights_bf16(weights, stage, sems):
    chunks = [(src, dst, scale, col_map or ((0, src.shape[1], 0),), r * WEIGHT_STAGE_ROWS)
              for src, dst, scale, col_map in weights for r in range(src.shape[0] // WEIGHT_STAGE_ROWS)]

    def copy(k):
        src, _, _, _, row0 = chunks[k]
        slot = k % WEIGHT_STAGE_SLOTS
        return pltpu.make_async_copy(src.at[pl.ds(row0, WEIGHT_STAGE_ROWS), :],
                                     stage.at[slot, :, pl.ds(0, src.shape[1])], sems.at[slot])

    ahead = WEIGHT_STAGE_SLOTS - 1
    for k in range(min(ahead, len(chunks))):
        copy(k).start(priority=k % 2)
    for k, (_, dst, scale, col_map, row0) in enumerate(chunks):
        if k + ahead < len(chunks):
            copy(k + ahead).start(priority=(k + ahead) % 2)
        copy(k).wait()
        for c0, c1, d0 in col_map:
            w = stage[k % WEIGHT_STAGE_SLOTS, :, c0:c1]
            dst[row0:row0 + WEIGHT_STAGE_ROWS, d0:d0 + c1 - c0] = (
                w if scale is None else w * scale).astype(jnp.bfloat16)


def _build_small_weights(pool_w_ref, w_a_ref, w_i_ref, small_w_ref):
    half = LRU_WIDTH // 2
    heads = half // LRU_HEAD_DIM
    bf16 = jnp.bfloat16
    c_idx = lax.broadcasted_iota(jnp.int32, (LRU_HEAD_DIM, half), 0)
    j_idx = lax.broadcasted_iota(jnp.int32, (LRU_HEAD_DIM, half), 1)
    spread = (lax.rem(j_idx, LRU_HEAD_DIM) == c_idx).astype(bf16)
    col_head = j_idx // LRU_HEAD_DIM
    for q in range(2):
        for part, w_ref in enumerate((w_a_ref, w_i_ref)):
            for h in range(heads):
                rep = _dot(w_ref[0, heads * q + h].astype(bf16), spread)
                blk = jnp.where(col_head == h, rep, 0.0).astype(bf16)
                small_w_ref[q, h * LRU_HEAD_DIM:(h + 1) * LRU_HEAD_DIM, part * half:(part + 1) * half] = blk
        small_w_ref[q, :, 2 * half:3 * half] = jnp.zeros((half, half), bf16)
        for g in range(2):
            rows = slice(g * POOL_GROUP, (g + 1) * POOL_GROUP)
            cols = slice(2 * half + g * POOL_GROUP, 2 * half + (g + 1) * POOL_GROUP)
            small_w_ref[q, rows, cols] = pool_w_ref[0, 2 * q + g].astype(bf16)


def _layer_kernel(x_ref, ln1_ref, w_in_hbm, pool_w_ref, pool_scale_ref, conv_w_ref, conv_b_ref,
                  w_a_ref, b_a_ref, w_i_ref, b_i_ref, lam_ref, gn_pool_ref, gn_lru_ref, w_out_hbm,
                  ln2_ref, wg_hbm, wu_hbm, wd_hbm, lnf_ref, o_ref,
                  pbuf, lbuf, abuf, bbuf, hbuf, hcarry, h1buf, n2buf,
                  w_in_ref, w_out_ref, wgu_ref, wd_ref, small_w_ref, stage, stage_sems,
                  *, chunks_per_seq, n_chunks):
    i = pl.program_id(0)
    s = lax.rem(i, chunks_per_seq)
    T = SEQ_TILE
    half = LRU_WIDTH // 2

    @pl.when(i == 0)
    def _():
        g_map = tuple((c0, c1, 2 * c0) for c0, c1 in _ff_chunk_cols())
        u_map = tuple((c0, c1, c0 + c1) for c0, c1 in _ff_chunk_cols())
        _load_weights_bf16(((w_in_hbm, w_in_ref, None, None), (w_out_hbm, w_out_ref, None, None),
                            (wg_hbm, wgu_ref, 0.5, g_map), (wu_hbm, wgu_ref, None, u_map),
                            (wd_hbm, wd_ref, None, None)), stage, stage_sems)
        _build_small_weights(pool_w_ref, w_a_ref, w_i_ref, small_w_ref)

    slot = lax.rem(i, 2)
    prev = 1 - slot

    def mixer():
        @pl.when(s == 0)
        def _():
            pbuf[0:POOL_TAIL, :] = jnp.zeros((POOL_TAIL, POOL_WIDTH), jnp.float32)
            lbuf[0:CONV_TAIL, :] = jnp.zeros((CONV_TAIL, LRU_WIDTH), jnp.float32)
            hcarry[...] = jnp.zeros_like(hcarry)

        yield FFN_PIECES_AT[0]
        nb = _rms(x_ref[...], ln1_ref[...]).astype(jnp.bfloat16)

        lbuf[CONV_TAIL:CONV_TAIL + T, :] = _dot(nb, w_in_ref[:, POOL_WIDTH:POOL_WIDTH + LRU_WIDTH])
        u_gate = _dot(nb, w_in_ref[:, POOL_WIDTH + LRU_WIDTH:])

        yield FFN_PIECES_AT[2]
        assert CONV_WIDTH == 4
        u_ext = lbuf[...]
        u_prev = pltpu.roll(u_ext, 1, axis=0)
        tap = lambda k: conv_w_ref[0, k:k + 1, :]
        older = tap(1) * u_ext + tap(0) * u_prev
        xc = (conv_b_ref[...] + tap(3) * u_ext + tap(2) * u_prev + pltpu.roll(older, 2, axis=0))[CONV_TAIL:]
        xcb = xc.astype(jnp.bfloat16)
        neg_lam = -lam_ref[...]
        softplus = jnp.maximum(neg_lam, 0.0) + jnp.log1p(jnp.exp(-jnp.abs(neg_lam)))
        k_row = (0.5 * LRU_C) * softplus

        nq = LRU_WIDTH // LANES
        tiles_per_sub = SUB_LEN // SUBLANES
        blk_rows = T // GATE_BLOCKS
        for blk in range(GATE_BLOCKS):
            rows_blk = slice(blk * blk_rows, (blk + 1) * blk_rows)
            z = [_dot(xcb[rows_blk, q * half:(q + 1) * half], small_w_ref[q, :, 0:2 * half])
                 for q in range(2)]
            yield FFN_PIECES_AT[3][blk]
            za = jnp.concatenate([z[0][:, :half], z[1][:, :half]], axis=-1) + b_a_ref[...]
            zi = jnp.concatenate([z[0][:, half:], z[1][:, half:]], axis=-1) + b_i_ref[...]
            neg_log_a = jnp.tanh(0.5 * za) * k_row + k_row
            a = jnp.exp(-neg_log_a)
            one_minus_a2 = jnp.tanh(neg_log_a) * (a * a + 1.0)
            mult = jnp.where(one_minus_a2 > 0.0, one_minus_a2 * lax.rsqrt(one_minus_a2), 0.0)
            bterm = mult * (_sigmoid(zi) * xc[rows_blk, :])
            for q in range(nq):
                lanes = slice(q * LANES, (q + 1) * LANES)
                for t in range(blk_rows // SUBLANES):
                    j, m0 = divmod(blk * (blk_rows // SUBLANES) + t, tiles_per_sub)
                    dst = pl.ds(m0 * SUBLANES * SUBLANES + j, SUBLANES, stride=SUBLANES)
                    rows = slice(t * SUBLANES, (t + 1) * SUBLANES)
                    abuf[q, dst, :] = a[rows, lanes]
                    bbuf[q, dst, :] = bterm[rows, lanes]

        def tile(m):
            return slice(m * SUBLANES, (m + 1) * SUBLANES)

        yield FFN_PIECES_AT[4]
        prod = [jnp.ones((SUBLANES, LANES), jnp.float32) for _ in range(nq)]
        loc = [jnp.zeros((SUBLANES, LANES), jnp.float32) for _ in range(nq)]
        for m in range(SUB_LEN):
            for q in range(nq):
                am = abuf[q, tile(m), :]
                loc[q] = am * loc[q] + bbuf[q, tile(m), :]
                prod[q] = am * prod[q]
        row = lax.broadcasted_iota(jnp.int32, (SUBLANES, LANES), 0)
        h_in = []
        for q in range(nq):
            carry = hcarry[:, q * LANES:(q + 1) * LANES]
            start = jnp.zeros((SUBLANES, LANES), jnp.float32)
            for j in range(SUBLANES):
                start = jnp.where(row == j, carry, start)
                carry = prod[q][j:j + 1, :] * carry + loc[q][j:j + 1, :]
            h_in.append(start)
            hcarry[:, q * LANES:(q + 1) * LANES] = carry
        yield FFN_PIECES_AT[5]
        for m in range(SUB_LEN):
            for q in range(nq):
                h_in[q] = abuf[q, tile(m), :] * h_in[q] + bbuf[q, tile(m), :]
                hbuf[q, pl.ds(m, SUBLANES, stride=SUB_PITCH), :] = h_in[q]
        h_lru = jnp.concatenate(
            [jnp.concatenate([hbuf[q, j * SUB_PITCH:j * SUB_PITCH + SUB_LEN, :] for j in range(SUBLANES)], axis=0)
             for q in range(nq)], axis=-1)

        pbuf[POOL_TAIL:POOL_TAIL + T, :] = _dot(nb, w_in_ref[:, 0:POOL_WIDTH])
        yield FFN_PIECES_AT[1]
        n_groups = len(POOL_WINDOWS)
        level = pbuf[...]
        sums = []
        for g, w in enumerate(POOL_WINDOWS):
            assert w == 2 << g and w - 1 <= POOL_TAIL
            rest = level[:, (0 if g == 0 else POOL_GROUP):]
            rest = rest + pltpu.roll(rest, w // 2, axis=0)
            sums.append(rest[POOL_TAIL:, 0:POOL_GROUP])
            level = rest
        head_pos = (s * T + 1 + lax.broadcasted_iota(jnp.int32, (POOL_TAIL, POOL_GROUP), 0)).astype(jnp.float32)
        d_parts = []
        for g, w in enumerate(POOL_WINDOWS):
            ug = pbuf[POOL_TAIL:POOL_TAIL + T, g * POOL_GROUP:(g + 1) * POOL_GROUP]
            head = sums[g][0:POOL_TAIL] / jnp.minimum(head_pos, float(w))
            mean = jnp.concatenate([head, sums[g][POOL_TAIL:] * (1.0 / w)], axis=0)
            d_parts.append((mean - ug).astype(jnp.bfloat16))
        y_parts = []
        for p in range(n_groups // 2):
            d_pair = jnp.concatenate(d_parts[2 * p:2 * p + 2], axis=-1)
            y_parts.append(_dot(d_pair, small_w_ref[p, :, 2 * half:3 * half]))
        y_pool = jnp.concatenate(y_parts, axis=-1) * pool_scale_ref[...]
        mix_pool = _rms(y_pool, gn_pool_ref[...]).astype(jnp.bfloat16)

        yield FFN_PIECES_AT[6]
        y_lru = _gelu_tanh_times(u_gate, h_lru)
        mix_lru = _rms(y_lru, gn_lru_ref[...]).astype(jnp.bfloat16)

        mix = jnp.concatenate([mix_pool, mix_lru], axis=-1)
        h1_new = x_ref[...] + _dot(mix, w_out_ref[...])
        yield FFN_PIECES_AT[7]
        h1buf[slot] = h1_new
        n2buf[slot] = _rms(h1_new, ln2_ref[...]).astype(jnp.bfloat16)

        pbuf[0:POOL_TAIL, :] = pbuf[T:T + POOL_TAIL, :]
        lbuf[0:CONV_TAIL, :] = lbuf[T:T + CONV_TAIL, :]

    def make_ffn():
        acts = {}
        ffn_parts = []
        pending = _ffn_sequence()

        def ffn(n):
            for _ in range(n):
                op, c = pending.pop(0)
                if op == "u":
                    c0, c1 = _ff_chunk_cols()[c]
                    n2 = n2buf[prev]
                    gu = _dot(n2, wgu_ref[:, 2 * c0:2 * c1])
                    half_g, up = gu[:, :c1 - c0], gu[:, c1 - c0:]
                    acts[c] = (half_g * (jnp.tanh(half_g) + 1.0) * up).astype(jnp.bfloat16)
                else:
                    group = range(c, min(c + FF_DOWN_GROUP, N_FF_CHUNKS))
                    act = jnp.concatenate([acts.pop(k) for k in group], axis=-1)
                    ffn_parts.append(_dot(act, wd_ref[c * FF_CHUNK:min((group[-1] + 1) * FF_CHUNK, D_FF), :]))

        def finish():
            ffn(len(pending))
            ffn_out = ffn_parts[0]
            for part in ffn_parts[1:]:
                ffn_out = ffn_out + part
            o_ref[...] = _rms(h1buf[prev] + ffn_out, lnf_ref[...])

        return ffn, finish

    @pl.when(i == 0)
    def _():
        for _ in mixer():
            pass

    @pl.when(jnp.logical_and(i > 0, i < n_chunks))
    def _():
        ffn, finish = make_ffn()
        for n in mixer():
            ffn(n)
        finish()

    @pl.when(i == n_chunks)
    def _():
        _, finish = make_ffn()
        finish()


def kernel(x, ln1_g, w_in, pool_w, pool_scale, conv_w, conv_b, w_a, b_a, w_i, b_i, lam, gn_pool_g,
           gn_lru_g, w_out, ln2_g, w_ffn_gate, w_ffn_up, w_ffn_down, lnf_g):
    B, S, D = x.shape
    assert D == D_MODEL and S % SEQ_TILE == 0 and ln1_g.shape[0] == 1
    bf16 = jnp.bfloat16
    vmem = pl.BlockSpec(memory_space=pltpu.VMEM)
    hbm = pl.BlockSpec(memory_space=pl.ANY)
    nq = LRU_WIDTH // LANES
    scan_rows = SUBLANES * SUB_PITCH
    cps = S // SEQ_TILE
    n_chunks = B * cps

    def in_index(i):
        c = jnp.minimum(i, n_chunks - 1)
        return (c // cps, c % cps, 0)

    def out_index(i):
        c = jnp.maximum(i - 1, 0)
        return (c // cps, c % cps, 0)

    out = pl.pallas_call(
        functools.partial(_layer_kernel, chunks_per_seq=cps, n_chunks=n_chunks),
        grid=(n_chunks + 1,),
        in_specs=[pl.BlockSpec((None, SEQ_TILE, D), in_index)] + [
            hbm if k in HBM_WEIGHT_OPERANDS else vmem for k in range(1, 20)],
        out_specs=pl.BlockSpec((None, SEQ_TILE, D), out_index),
        out_shape=jax.ShapeDtypeStruct((B, S, D), jnp.float32),
        scratch_shapes=[
            pltpu.VMEM((POOL_TAIL + SEQ_TILE, POOL_WIDTH), jnp.float32),
            pltpu.VMEM((CONV_TAIL + SEQ_TILE, LRU_WIDTH), jnp.float32),
            pltpu.VMEM((nq, SEQ_TILE, LANES), jnp.float32),
            pltpu.VMEM((nq, SEQ_TILE, LANES), jnp.float32),
            pltpu.VMEM((nq, scan_rows, LANES), jnp.float32),
            pltpu.VMEM((1, LRU_WIDTH), jnp.float32),
            pltpu.VMEM((2, SEQ_TILE, D), jnp.float32),
            pltpu.VMEM((2, SEQ_TILE, D), jnp.bfloat16),
            pltpu.VMEM((D, IN_WIDTH), bf16),
            pltpu.VMEM((D, D), bf16),
            pltpu.VMEM((D, 2 * D_FF), bf16),
            pltpu.VMEM((D_FF, D), bf16),
            pltpu.VMEM((2, LRU_WIDTH // 2, LRU_WIDTH + LRU_WIDTH // 2), bf16),
            pltpu.VMEM((WEIGHT_STAGE_SLOTS, WEIGHT_STAGE_ROWS, D_FF), jnp.float32),
            pltpu.SemaphoreType.DMA((WEIGHT_STAGE_SLOTS,)),
        ],
        compiler_params=pltpu.CompilerParams(
            dimension_semantics=("arbitrary",),
            vmem_limit_bytes=VMEM_LIMIT_BYTES),
        name="hymba_layer",
    )(x, ln1_g, w_in[0], pool_w, pool_scale, conv_w, conv_b, w_a, b_a, w_i, b_i, lam, gn_pool_g, gn_lru_g,
      w_out[0], ln2_g, w_ffn_gate[0], w_ffn_up[0], w_ffn_down[0], lnf_g.reshape(1, D))
    return out
```

```python
import functools
import math

import jax
import jax.numpy as jnp
from jax import lax
from jax.experimental import pallas as pl
from jax.experimental.pallas import tpu as pltpu

D_MODEL = 1024
POOL_WINDOWS = (2, 4, 8, 16)
POOL_GROUP = 128
POOL_WIDTH = 512
LRU_WIDTH = 512
LRU_HEADS = 8
LRU_HEAD_DIM = 64
CONV_WIDTH = 4
LRU_C = 8.0
D_FF = 2816
IN_WIDTH = POOL_WIDTH + 2 * LRU_WIDTH
EPS = 1e-6

LANES = 128
SUBLANES = 8
SEQ_TILE = 512
POOL_TAIL = 16
CONV_TAIL = 8
SUB_LEN = SEQ_TILE // SUBLANES
SUB_PITCH = SUB_LEN + SUBLANES
FF_CHUNK = 512
N_FF_CHUNKS = -(-D_FF // FF_CHUNK)
FF_DOWN_GROUP = 3
FFN_PIECES_AT = (1, 1, 1, 1, 1, 1, 1, 1)
HBM_WEIGHT_OPERANDS = (2, 14, 16, 17, 18)
WEIGHT_STAGE_ROWS = 128
WEIGHT_STAGE_SLOTS = 6
VMEM_LIMIT_BYTES = 58 * 1024 * 1024


def _ffn_sequence():
    ops = []
    for c in range(N_FF_CHUNKS):
        ops.append(("u", c))
        if c % FF_DOWN_GROUP == 0 and c > 0:
            ops.append(("d", c - FF_DOWN_GROUP))
    last = (N_FF_CHUNKS - 1) // FF_DOWN_GROUP * FF_DOWN_GROUP
    return ops + [("d", last)]


def _ff_chunk_cols():
    return tuple((c * FF_CHUNK, min((c + 1) * FF_CHUNK, D_FF)) for c in range(N_FF_CHUNKS))


def _rms(x, g):
    ms = jnp.mean(x * x, axis=-1, keepdims=True)
    return x * lax.rsqrt(ms + EPS) * g


def _sigmoid(x):
    return 0.5 * (jnp.tanh(0.5 * x) + 1.0)


def _gelu_tanh_times(x, h):
    c = math.sqrt(2.0 / math.pi)
    t = jnp.tanh(x * ((x * x) * (0.044715 * c) + c))
    return (h * x) * (0.5 * t + 0.5)


def _dot(a, b):
    return jnp.dot(a, b, preferred_element_type=jnp.float32)


def _load_weights_bf16(weights, stage, sems):
    chunks = [(src, dst, scale, col_map or ((0, src.shape[1], 0),), r * WEIGHT_STAGE_ROWS)
              for src, dst, scale, col_map in weights for r in range(src.shape[0] // WEIGHT_STAGE_ROWS)]

    def copy(k):
        src, _, _, _, row0 = chunks[k]
        slot = k % WEIGHT_STAGE_SLOTS
        return pltpu.make_async_copy(src.at[pl.ds(row0, WEIGHT_STAGE_ROWS), :],
                                     stage.at[slot, :, pl.ds(0, src.shape[1])], sems.at[slot])

    ahead = WEIGHT_STAGE_SLOTS - 1
    for k in range(min(ahead, len(chunks))):
        copy(k).start(priority=k % 2)
    for k, (_, dst, scale, col_map, row0) in enumerate(chunks):
        if k + ahead < len(chunks):
            copy(k + ahead).start(priority=(k + ahead) % 2)
        copy(k).wait()
        for c0, c1, d0 in col_map:
            w = stage[k % WEIGHT_STAGE_SLOTS, :, c0:c1]
            dst[row0:row0 + WEIGHT_STAGE_ROWS, d0:d0 + c1 - c0] = (
                w if scale is None else w * scale).astype(jnp.bfloat16)


def _build_small_weights(pool_w_ref, w_a_ref, w_i_ref, small_w_ref):
    half = LRU_WIDTH // 2
    heads = half // LRU_HEAD_DIM
    bf16 = jnp.bfloat16
    c_idx = lax.broadcasted_iota(jnp.int32, (LRU_HEAD_DIM, half), 0)
    j_idx = lax.broadcasted_iota(jnp.int32, (LRU_HEAD_DIM, half), 1)
    spread = (lax.rem(j_idx, LRU_HEAD_DIM) == c_idx).astype(bf16)
    col_head = j_idx // LRU_HEAD_DIM
    for q in range(2):
        for part, w_ref in enumerate((w_a_ref, w_i_ref)):
            for h in range(heads):
                rep = _dot(w_ref[0, heads * q + h].astype(bf16), spread)
                blk = jnp.where(col_head == h, rep, 0.0).astype(bf16)
                small_w_ref[q, h * LRU_HEAD_DIM:(h + 1) * LRU_HEAD_DIM, part * half:(part + 1) * half] = blk
        small_w_ref[q, :, 2 * half:3 * half] = jnp.zeros((half, half), bf16)
        for g in range(2):
            rows = slice(g * POOL_GROUP, (g + 1) * POOL_GROUP)
            cols = slice(2 * half + g * POOL_GROUP, 2 * half + (g + 1) * POOL_GROUP)
            small_w_ref[q, rows, cols] = pool_w_ref[0, 2 * q + g].astype(bf16)


def _layer_kernel(x_ref, ln1_ref, w_in_hbm, pool_w_ref, pool_scale_ref, conv_w_ref, conv_b_ref,
                  w_a_ref, b_a_ref, w_i_ref, b_i_ref, lam_ref, gn_pool_ref, gn_lru_ref, w_out_hbm,
                  ln2_ref, wg_hbm, wu_hbm, wd_hbm, lnf_ref, o_ref,
                  pbuf, lbuf, abuf, bbuf, hbuf, hcarry, h1buf, n2buf,
                  w_in_ref, w_out_ref, wgu_ref, wd_ref, small_w_ref, stage, stage_sems,
                  *, chunks_per_seq, n_chunks):
    i = pl.program_id(0)
    s = lax.rem(i, chunks_per_seq)
    T = SEQ_TILE
    half = LRU_WIDTH // 2

    @pl.when(i == 0)
    def _():
        g_map = tuple((c0, c1, 2 * c0) for c0, c1 in _ff_chunk_cols())
        u_map = tuple((c0, c1, c0 + c1) for c0, c1 in _ff_chunk_cols())
        _load_weights_bf16(((w_in_hbm, w_in_ref, None, None), (w_out_hbm, w_out_ref, None, None),
                            (wg_hbm, wgu_ref, 0.5, g_map), (wu_hbm, wgu_ref, None, u_map),
                            (wd_hbm, wd_ref, None, None)), stage, stage_sems)
        _build_small_weights(pool_w_ref, w_a_ref, w_i_ref, small_w_ref)

    slot = lax.rem(i, 2)
    prev = 1 - slot

    def mixer():
        @pl.when(s == 0)
        def _():
            pbuf[0:POOL_TAIL, :] = jnp.zeros((POOL_TAIL, POOL_WIDTH), jnp.float32)
            lbuf[0:CONV_TAIL, :] = jnp.zeros((CONV_TAIL, LRU_WIDTH), jnp.float32)
            hcarry[...] = jnp.zeros_like(hcarry)

        yield FFN_PIECES_AT[0]
        nb = _rms(x_ref[...], ln1_ref[...]).astype(jnp.bfloat16)

        lbuf[CONV_TAIL:CONV_TAIL + T, :] = _dot(nb, w_in_ref[:, POOL_WIDTH:POOL_WIDTH + LRU_WIDTH])
        u_gate = _dot(nb, w_in_ref[:, POOL_WIDTH + LRU_WIDTH:])

        yield FFN_PIECES_AT[1]
        assert CONV_WIDTH == 4
        u_ext = lbuf[...]
        u_prev = pltpu.roll(u_ext, 1, axis=0)
        tap = lambda k: conv_w_ref[0, k:k + 1, :]
        older = tap(1) * u_ext + tap(0) * u_prev
        xc = (conv_b_ref[...] + tap(3) * u_ext + tap(2) * u_prev + pltpu.roll(older, 2, axis=0))[CONV_TAIL:]
        xcb = xc.astype(jnp.bfloat16)
        neg_lam = -lam_ref[...]
        softplus = jnp.maximum(neg_lam, 0.0) + jnp.log1p(jnp.exp(-jnp.abs(neg_lam)))
        k_row = (0.5 * LRU_C) * softplus

        z = [_dot(xcb[:, q * half:(q + 1) * half], small_w_ref[q, :, 0:2 * half])
             for q in range(2)]
        yield FFN_PIECES_AT[2]
        za = jnp.concatenate([z[0][:, :half], z[1][:, :half]], axis=-1) + b_a_ref[...]
        zi = jnp.concatenate([z[0][:, half:], z[1][:, half:]], axis=-1) + b_i_ref[...]
        neg_log_a = jnp.tanh(0.5 * za) * k_row + k_row
        a = jnp.exp(-neg_log_a)
        one_minus_a2 = jnp.tanh(neg_log_a) * (a * a + 1.0)
        mult = jnp.where(one_minus_a2 > 0.0, one_minus_a2 * lax.rsqrt(one_minus_a2), 0.0)
        bterm = mult * (_sigmoid(zi) * xc)

        nq = LRU_WIDTH // LANES
        tiles_per_sub = SUB_LEN // SUBLANES
        for q in range(nq):
            lanes = slice(q * LANES, (q + 1) * LANES)
            for t in range(T // SUBLANES):
                j, m0 = divmod(t, tiles_per_sub)
                dst = pl.ds(m0 * SUBLANES * SUBLANES + j, SUBLANES, stride=SUBLANES)
                rows = slice(t * SUBLANES, (t + 1) * SUBLANES)
                abuf[q, dst, :] = a[rows, lanes]
                bbuf[q, dst, :] = bterm[rows, lanes]

        def tile(m):
            return slice(m * SUBLANES, (m + 1) * SUBLANES)

        yield FFN_PIECES_AT[3]
        prod = [jnp.ones((SUBLANES, LANES), jnp.float32) for _ in range(nq)]
        loc = [jnp.zeros((SUBLANES, LANES), jnp.float32) for _ in range(nq)]
        for m in range(SUB_LEN):
            for q in range(nq):
                am = abuf[q, tile(m), :]
                loc[q] = am * loc[q] + bbuf[q, tile(m), :]
                prod[q] = am * prod[q]
        row = lax.broadcasted_iota(jnp.int32, (SUBLANES, LANES), 0)
        h_in = []
        for q in range(nq):
            carry = hcarry[:, q * LANES:(q + 1) * LANES]
            start = jnp.zeros((SUBLANES, LANES), jnp.float32)
            for j in range(SUBLANES):
                start = jnp.where(row == j, carry, start)
                carry = prod[q][j:j + 1, :] * carry + loc[q][j:j + 1, :]
            h_in.append(start)
            hcarry[:, q * LANES:(q + 1) * LANES] = carry
        yield FFN_PIECES_AT[4]
        for m in range(SUB_LEN):
            for q in range(nq):
                h_in[q] = abuf[q, tile(m), :] * h_in[q] + bbuf[q, tile(m), :]
                hbuf[q, pl.ds(m, SUBLANES, stride=SUB_PITCH), :] = h_in[q]
        h_lru = jnp.concatenate(
            [jnp.concatenate([hbuf[q, j * SUB_PITCH:j * SUB_PITCH + SUB_LEN, :] for j in range(SUBLANES)], axis=0)
             for q in range(nq)], axis=-1)

        pbuf[POOL_TAIL:POOL_TAIL + T, :] = _dot(nb, w_in_ref[:, 0:POOL_WIDTH])
        yield FFN_PIECES_AT[5]
        n_groups = len(POOL_WINDOWS)
        level = pbuf[...]
        sums = []
        for g, w in enumerate(POOL_WINDOWS):
            assert w == 2 << g and w - 1 <= POOL_TAIL
            rest = level[:, (0 if g == 0 else POOL_GROUP):]
            rest = rest + pltpu.roll(rest, w // 2, axis=0)
            sums.append(rest[POOL_TAIL:, 0:POOL_GROUP])
            level = rest
        head_pos = (s * T + 1 + lax.broadcasted_iota(jnp.int32, (POOL_TAIL, POOL_GROUP), 0)).astype(jnp.float32)
        d_parts = []
        for g, w in enumerate(POOL_WINDOWS):
            ug = pbuf[POOL_TAIL:POOL_TAIL + T, g * POOL_GROUP:(g + 1) * POOL_GROUP]
            head = sums[g][0:POOL_TAIL] / jnp.minimum(head_pos, float(w))
            mean = jnp.concatenate([head, sums[g][POOL_TAIL:] * (1.0 / w)], axis=0)
            d_parts.append((mean - ug).astype(jnp.bfloat16))
        y_parts = []
        for p in range(n_groups // 2):
            d_pair = jnp.concatenate(d_parts[2 * p:2 * p + 2], axis=-1)
            y_parts.append(_dot(d_pair, small_w_ref[p, :, 2 * half:3 * half]))
        y_pool = jnp.concatenate(y_parts, axis=-1) * pool_scale_ref[...]
        mix_pool = _rms(y_pool, gn_pool_ref[...]).astype(jnp.bfloat16)

        yield FFN_PIECES_AT[6]
        y_lru = _gelu_tanh_times(u_gate, h_lru)
        mix_lru = _rms(y_lru, gn_lru_ref[...]).astype(jnp.bfloat16)

        mix = jnp.concatenate([mix_pool, mix_lru], axis=-1)
        h1_new = x_ref[...] + _dot(mix, w_out_ref[...])
        yield FFN_PIECES_AT[7]
        h1buf[slot] = h1_new
        n2buf[slot] = _rms(h1_new, ln2_ref[...]).astype(jnp.bfloat16)

        pbuf[0:POOL_TAIL, :] = pbuf[T:T + POOL_TAIL, :]
        lbuf[0:CONV_TAIL, :] = lbuf[T:T + CONV_TAIL, :]

    def make_ffn():
        acts = {}
        ffn_parts = []
        pending = _ffn_sequence()

        def ffn(n):
            for _ in range(n):
                op, c = pending.pop(0)
                if op == "u":
                    c0, c1 = _ff_chunk_cols()[c]
                    n2 = n2buf[prev]
                    gu = _dot(n2, wgu_ref[:, 2 * c0:2 * c1])
                    half_g, up = gu[:, :c1 - c0], gu[:, c1 - c0:]
                    acts[c] = (half_g * (jnp.tanh(half_g) + 1.0) * up).astype(jnp.bfloat16)
                else:
                    group = range(c, min(c + FF_DOWN_GROUP, N_FF_CHUNKS))
                    act = jnp.concatenate([acts.pop(k) for k in group], axis=-1)
                    ffn_parts.append(_dot(act, wd_ref[c * FF_CHUNK:min((group[-1] + 1) * FF_CHUNK, D_FF), :]))

        def finish():
            ffn(len(pending))
            ffn_out = ffn_parts[0]
            for part in ffn_parts[1:]:
                ffn_out = ffn_out + part
            o_ref[...] = _rms(h1buf[prev] + ffn_out, lnf_ref[...])

        return ffn, finish

    @pl.when(i == 0)
    def _():
        for _ in mixer():
            pass

    @pl.when(jnp.logical_and(i > 0, i < n_chunks))
    def _():
        ffn, finish = make_ffn()
        for n in mixer():
            ffn(n)
        finish()

    @pl.when(i == n_chunks)
    def _():
        _, finish = make_ffn()
        finish()


def kernel(x, ln1_g, w_in, pool_w, pool_scale, conv_w, conv_b, w_a, b_a, w_i, b_i, lam, gn_pool_g,
           gn_lru_g, w_out, ln2_g, w_ffn_gate, w_ffn_up, w_ffn_down, lnf_g):
    B, S, D = x.shape
    assert D == D_MODEL and S % SEQ_TILE == 0 and ln1_g.shape[0] == 1
    bf16 = jnp.bfloat16
    vmem = pl.BlockSpec(memory_space=pltpu.VMEM)
    hbm = pl.BlockSpec(memory_space=pl.ANY)
    nq = LRU_WIDTH // LANES
    scan_rows = SUBLANES * SUB_PITCH
    cps = S // SEQ_TILE
    n_chunks = B * cps

    def in_index(i):
        c = jnp.minimum(i, n_chunks - 1)
        return (c // cps, c % cps, 0)

    def out_index(i):
        c = jnp.maximum(i - 1, 0)
        return (c // cps, c % cps, 0)

    out = pl.pallas_call(
        functools.partial(_layer_kernel, chunks_per_seq=cps, n_chunks=n_chunks),
        grid=(n_chunks + 1,),
        in_specs=[pl.BlockSpec((None, SEQ_TILE, D), in_index)] + [
            hbm if k in HBM_WEIGHT_OPERANDS else vmem for k in range(1, 20)],
        out_specs=pl.BlockSpec((None, SEQ_TILE, D), out_index),
        out_shape=jax.ShapeDtypeStruct((B, S, D), jnp.float32),
        scratch_shapes=[
            pltpu.VMEM((POOL_TAIL + SEQ_TILE, POOL_WIDTH), jnp.float32),
            pltpu.VMEM((CONV_TAIL + SEQ_TILE, LRU_WIDTH), jnp.float32),
            pltpu.VMEM((nq, SEQ_TILE, LANES), jnp.float32),
            pltpu.VMEM((nq, SEQ_TILE, LANES), jnp.float32),
            pltpu.VMEM((nq, scan_rows, LANES), jnp.float32),
            pltpu.VMEM((1, LRU_WIDTH), jnp.float32),
            pltpu.VMEM((2, SEQ_TILE, D), jnp.float32),
            pltpu.VMEM((2, SEQ_TILE, D), jnp.bfloat16),
            pltpu.VMEM((D, IN_WIDTH), bf16),
            pltpu.VMEM((D, D), bf16),
            pltpu.VMEM((D, 2 * D_FF), bf16),
            pltpu.VMEM((D_FF, D), bf16),
            pltpu.VMEM((2, LRU_WIDTH // 2, LRU_WIDTH + LRU_WIDTH // 2), bf16),
            pltpu.VMEM((WEIGHT_STAGE_SLOTS, WEIGHT_STAGE_ROWS, D_FF), jnp.float32),
            pltpu.SemaphoreType.DMA((WEIGHT_STAGE_SLOTS,)),
        ],
        compiler_params=pltpu.CompilerParams(
            dimension_semantics=("arbitrary",),
            vmem_limit_bytes=VMEM_LIMIT_BYTES),
        name="hymba_layer",
    )(x, ln1_g, w_in[0], pool_w, pool_scale, conv_w, conv_b, w_a, b_a, w_i, b_i, lam, gn_pool_g, gn_lru_g,
      w_out[0], ln2_g, w_ffn_gate[0], w_ffn_up[0], w_ffn_down[0], lnf_g.reshape(1, D))
    return out
```

```python
import functools
import math

import jax
import jax.numpy as jnp
from jax import lax
from jax.experimental import pallas as pl
from jax.experimental.pallas import tpu as pltpu

D_MODEL = 1024
POOL_WINDOWS = (2, 4, 8, 16)
POOL_GROUP = 128
POOL_WIDTH = 512
LRU_WIDTH = 512
LRU_HEADS = 8
LRU_HEAD_DIM = 64
CONV_WIDTH = 4
LRU_C = 8.0
D_FF = 2816
IN_WIDTH = POOL_WIDTH + 2 * LRU_WIDTH
EPS = 1e-6

LANES = 128
SUBLANES = 8
SEQ_TILE = 512
POOL_TAIL = 16
CONV_TAIL = 8
SUB_LEN = SEQ_TILE // SUBLANES
SUB_PITCH = SUB_LEN + SUBLANES
FF_CHUNK = 512
N_FF_CHUNKS = -(-D_FF // FF_CHUNK)
FF_DOWN_GROUP = 3
GATE_BLOCKS = 1
FFN_PIECES_AT = (1, 1, 1, (1,), 1, 1, 1, 1)
HBM_WEIGHT_OPERANDS = (2, 14, 16, 17, 18)
WEIGHT_STAGE_ROWS = 128
WEIGHT_STAGE_SLOTS = 6
VMEM_LIMIT_BYTES = 58 * 1024 * 1024


def _ffn_sequence():
    ops = []
    for c in range(N_FF_CHUNKS):
        ops.append(("u", c))
        if c % FF_DOWN_GROUP == 0 and c > 0:
            ops.append(("d", c - FF_DOWN_GROUP))
    last = (N_FF_CHUNKS - 1) // FF_DOWN_GROUP * FF_DOWN_GROUP
    return ops + [("d", last)]


def _ff_chunk_cols():
    return tuple((c * FF_CHUNK, min((c + 1) * FF_CHUNK, D_FF)) for c in range(N_FF_CHUNKS))


def _rms(x, g):
    ms = jnp.mean(x * x, axis=-1, keepdims=True)
    return x * lax.rsqrt(ms + EPS) * g


def _sigmoid(x):
    return 0.5 * (jnp.tanh(0.5 * x) + 1.0)


def _gelu_tanh_times(x, h):
    c = math.sqrt(2.0 / math.pi)
    t = jnp.tanh(x * ((x * x) * (0.044715 * c) + c))
    return (h * x) * (0.5 * t + 0.5)


def _dot(a, b):
    return jnp.dot(a, b, preferred_element_type=jnp.float32)


def _load_weights_bf16(weights, stage, sems):
    chunks = [(src, dst, scale, col_map or ((0, src.shape[1], 0),), r * WEIGHT_STAGE_ROWS)
              for src, dst, scale, col_map in weights for r in range(src.shape[0] // WEIGHT_STAGE_ROWS)]

    def copy(k):
        src, _, _, _, row0 = chunks[k]
        slot = k % WEIGHT_STAGE_SLOTS
        return pltpu.make_async_copy(src.at[pl.ds(row0, WEIGHT_STAGE_ROWS), :],
                                     stage.at[slot, :, pl.ds(0, src.shape[1])], sems.at[slot])

    ahead = WEIGHT_STAGE_SLOTS - 1
    for k in range(min(ahead, len(chunks))):
        copy(k).start(priority=k % 2)
    for k, (_, dst, scale, col_map, row0) in enumerate(chunks):
        if k + ahead < len(chunks):
            copy(k + ahead).start(priority=(k + ahead) % 2)
        copy(k).wait()
        for c0, c1, d0 in col_map:
            w = stage[k % WEIGHT_STAGE_SLOTS, :, c0:c1]
            dst[row0:row0 + WEIGHT_STAGE_ROWS, d0:d0 + c1 - c0] = (
                w if scale is None else w * scale).astype(jnp.bfloat16)


def _build_small_weights(pool_w_ref, w_a_ref, w_i_ref, small_w_ref):
    half = LRU_WIDTH // 2
    heads = half // LRU_HEAD_DIM
    bf16 = jnp.bfloat16
    c_idx = lax.broadcasted_iota(jnp.int32, (LRU_HEAD_DIM, half), 0)
    j_idx = lax.broadcasted_iota(jnp.int32, (LRU_HEAD_DIM, half), 1)
    spread = (lax.rem(j_idx, LRU_HEAD_DIM) == c_idx).astype(bf16)
    col_head = j_idx // LRU_HEAD_DIM
    for q in range(2):
        for part, w_ref in enumerate((w_a_ref, w_i_ref)):
            for h in range(heads):
                rep = _dot(w_ref[0, heads * q + h].astype(bf16), spread)
                blk = jnp.where(col_head == h, rep, 0.0).astype(bf16)
                small_w_ref[q, h * LRU_HEAD_DIM:(h + 1) * LRU_HEAD_DIM, part * half:(part + 1) * half] = blk
        small_w_ref[q, :, 2 * half:3 * half] = jnp.zeros((half, half), bf16)
        for g in range(2):
            rows = slice(g * POOL_GROUP, (g + 1) * POOL_GROUP)
            cols = slice(2 * half + g * POOL_GROUP, 2 * half + (g + 1) * POOL_GROUP)
            small_w_ref[q, rows, cols] = pool_w_ref[0, 2 * q + g].astype(bf16)


def _layer_kernel(x_ref, ln1_ref, w_in_hbm, pool_w_ref, pool_scale_ref, conv_w_ref, conv_b_ref,
                  w_a_ref, b_a_ref, w_i_ref, b_i_ref, lam_ref, gn_pool_ref, gn_lru_ref, w_out_hbm,
                  ln2_ref, wg_hbm, wu_hbm, wd_hbm, lnf_ref, o_ref,
                  pbuf, lbuf, abuf, bbuf, hbuf, hcarry, h1buf, n2buf,
                  w_in_ref, w_out_ref, wgu_ref, wd_ref, small_w_ref, stage, stage_sems,
                  *, chunks_per_seq, n_chunks):
    i = pl.program_id(0)
    s = lax.rem(i, chunks_per_seq)
    T = SEQ_TILE
    half = LRU_WIDTH // 2

    @pl.when(i == 0)
    def _():
        g_map = tuple((c0, c1, 2 * c0) for c0, c1 in _ff_chunk_cols())
        u_map = tuple((c0, c1, c0 + c1) for c0, c1 in _ff_chunk_cols())
        _load_weights_bf16(((w_in_hbm, w_in_ref, None, None), (w_out_hbm, w_out_ref, None, None),
                            (wg_hbm, wgu_ref, 0.5, g_map), (wu_hbm, wgu_ref, None, u_map),
                            (wd_hbm, wd_ref, None, None)), stage, stage_sems)
        _build_small_weights(pool_w_ref, w_a_ref, w_i_ref, small_w_ref)

    slot = lax.rem(i, 2)
    prev = 1 - slot

    def mixer():
        @pl.when(s == 0)
        def _():
            pbuf[0:POOL_TAIL, :] = jnp.zeros((POOL_TAIL, POOL_WIDTH), jnp.float32)
            lbuf[0:CONV_TAIL, :] = jnp.zeros((CONV_TAIL, LRU_WIDTH), jnp.float32)
            hcarry[...] = jnp.zeros_like(hcarry)

        yield FFN_PIECES_AT[0]
        nb = _rms(x_ref[...], ln1_ref[...]).astype(jnp.bfloat16)

        lbuf[CONV_TAIL:CONV_TAIL + T, :] = _dot(nb, w_in_ref[:, POOL_WIDTH:POOL_WIDTH + LRU_WIDTH])

        yield FFN_PIECES_AT[2]
        assert CONV_WIDTH == 4
        u_ext = lbuf[...]
        u_prev = pltpu.roll(u_ext, 1, axis=0)
        tap = lambda k: conv_w_ref[0, k:k + 1, :]
        older = tap(1) * u_ext + tap(0) * u_prev
        xc = (conv_b_ref[...] + tap(3) * u_ext + tap(2) * u_prev + pltpu.roll(older, 2, axis=0))[CONV_TAIL:]
        xcb = xc.astype(jnp.bfloat16)
        neg_lam = -lam_ref[...]
        softplus = jnp.maximum(neg_lam, 0.0) + jnp.log1p(jnp.exp(-jnp.abs(neg_lam)))
        k_row = (0.5 * LRU_C) * softplus

        nq = LRU_WIDTH // LANES
        tiles_per_sub = SUB_LEN // SUBLANES
        blk_rows = T // GATE_BLOCKS
        for blk in range(GATE_BLOCKS):
            rows_blk = slice(blk * blk_rows, (blk + 1) * blk_rows)
            z = [_dot(xcb[rows_blk, q * half:(q + 1) * half], small_w_ref[q, :, 0:2 * half])
                 for q in range(2)]
            yield FFN_PIECES_AT[3][blk]
            za = jnp.concatenate([z[0][:, :half], z[1][:, :half]], axis=-1) + b_a_ref[...]
            zi = jnp.concatenate([z[0][:, half:], z[1][:, half:]], axis=-1) + b_i_ref[...]
            neg_log_a = jnp.tanh(0.5 * za) * k_row + k_row
            a = jnp.exp(-neg_log_a)
            one_minus_a2 = jnp.tanh(neg_log_a) * (a * a + 1.0)
            mult = jnp.where(one_minus_a2 > 0.0, one_minus_a2 * lax.rsqrt(one_minus_a2), 0.0)
            bterm = mult * (_sigmoid(zi) * xc[rows_blk, :])
            for q in range(nq):
                lanes = slice(q * LANES, (q + 1) * LANES)
                for t in range(blk_rows // SUBLANES):
                    j, m0 = divmod(blk * (blk_rows // SUBLANES) + t, tiles_per_sub)
                    dst = pl.ds(m0 * SUBLANES * SUBLANES + j, SUBLANES, stride=SUBLANES)
                    rows = slice(t * SUBLANES, (t + 1) * SUBLANES)
                    abuf[q, dst, :] = a[rows, lanes]
                    bbuf[q, dst, :] = bterm[rows, lanes]

        def tile(m):
            return slice(m * SUBLANES, (m + 1) * SUBLANES)

        yield FFN_PIECES_AT[4]
        prod = [jnp.ones((SUBLANES, LANES), jnp.float32) for _ in range(nq)]
        loc = [jnp.zeros((SUBLANES, LANES), jnp.float32) for _ in range(nq)]
        for m in range(SUB_LEN):
            for q in range(nq):
                am = abuf[q, tile(m), :]
                loc[q] = am * loc[q] + bbuf[q, tile(m), :]
                prod[q] = am * prod[q]
        row = lax.broadcasted_iota(jnp.int32, (SUBLANES, LANES), 0)
        h_in = []
        for q in range(nq):
            carry = hcarry[:, q * LANES:(q + 1) * LANES]
            start = jnp.zeros((SUBLANES, LANES), jnp.float32)
            for j in range(SUBLANES):
                start = jnp.where(row == j, carry, start)
                carry = prod[q][j:j + 1, :] * carry + loc[q][j:j + 1, :]
            h_in.append(start)
            hcarry[:, q * LANES:(q + 1) * LANES] = carry
        yield FFN_PIECES_AT[5]
        for m in range(SUB_LEN):
            for q in range(nq):
                h_in[q] = abuf[q, tile(m), :] * h_in[q] + bbuf[q, tile(m), :]
                hbuf[q, pl.ds(m, SUBLANES, stride=SUB_PITCH), :] = h_in[q]
        h_lru = jnp.concatenate(
            [jnp.concatenate([hbuf[q, j * SUB_PITCH:j * SUB_PITCH + SUB_LEN, :] for j in range(SUBLANES)], axis=0)
             for q in range(nq)], axis=-1)

        pbuf[POOL_TAIL:POOL_TAIL + T, :] = _dot(nb, w_in_ref[:, 0:POOL_WIDTH])
        yield FFN_PIECES_AT[1]
        n_groups = len(POOL_WINDOWS)
        level = pbuf[...]
        sums = []
        for g, w in enumerate(POOL_WINDOWS):
            assert w == 2 << g and w - 1 <= POOL_TAIL
            rest = level[:, (0 if g == 0 else POOL_GROUP):]
            rest = rest + pltpu.roll(rest, w // 2, axis=0)
            sums.append(rest[POOL_TAIL:, 0:POOL_GROUP])
            level = rest
        head_pos = (s * T + 1 + lax.broadcasted_iota(jnp.int32, (POOL_TAIL, POOL_GROUP), 0)).astype(jnp.float32)
        d_parts = []
        for g, w in enumerate(POOL_WINDOWS):
            ug = pbuf[POOL_TAIL:POOL_TAIL + T, g * POOL_GROUP:(g + 1) * POOL_GROUP]
            head = sums[g][0:POOL_TAIL] / jnp.minimum(head_pos, float(w))
            mean = jnp.concatenate([head, sums[g][POOL_TAIL:] * (1.0 / w)], axis=0)
            d_parts.append((mean - ug).astype(jnp.bfloat16))
        y_parts = []
        for p in range(n_groups // 2):
            d_pair = jnp.concatenate(d_parts[2 * p:2 * p + 2], axis=-1)
            y_parts.append(_dot(d_pair, small_w_ref[p, :, 2 * half:3 * half]))
        y_pool = jnp.concatenate(y_parts, axis=-1) * pool_scale_ref[...]
        mix_pool = _rms(y_pool, gn_pool_ref[...]).astype(jnp.bfloat16)

        u_gate = _dot(nb, w_in_ref[:, POOL_WIDTH + LRU_WIDTH:])
        yield FFN_PIECES_AT[6]
        y_lru = _gelu_tanh_times(u_gate, h_lru)
        mix_lru = _rms(y_lru, gn_lru_ref[...]).astype(jnp.bfloat16)

        mix = jnp.concatenate([mix_pool, mix_lru], axis=-1)
        h1_new = x_ref[...] + _dot(mix, w_out_ref[...])
        yield FFN_PIECES_AT[7]
        h1buf[slot] = h1_new
        n2buf[slot] = _rms(h1_new, ln2_ref[...]).astype(jnp.bfloat16)

        pbuf[0:POOL_TAIL, :] = pbuf[T:T + POOL_TAIL, :]
        lbuf[0:CONV_TAIL, :] = lbuf[T:T + CONV_TAIL, :]

    def make_ffn():
        acts = {}
        ffn_parts = []
        pending = _ffn_sequence()

        def ffn(n):
            for _ in range(n):
                op, c = pending.pop(0)
                if op == "u":
                    c0, c1 = _ff_chunk_cols()[c]
                    n2 = n2buf[prev]
                    gu = _dot(n2, wgu_ref[:, 2 * c0:2 * c1])
                    half_g, up = gu[:, :c1 - c0], gu[:, c1 - c0:]
                    acts[c] = (half_g * (jnp.tanh(half_g) + 1.0) * up).astype(jnp.bfloat16)
                else:
                    group = range(c, min(c + FF_DOWN_GROUP, N_FF_CHUNKS))
                    act = jnp.concatenate([acts.pop(k) for k in group], axis=-1)
                    ffn_parts.append(_dot(act, wd_ref[c * FF_CHUNK:min((group[-1] + 1) * FF_CHUNK, D_FF), :]))

        def finish():
            ffn(len(pending))
            ffn_out = ffn_parts[0]
            for part in ffn_parts[1:]:
                ffn_out = ffn_out + part
            o_ref[...] = _rms(h1buf[prev] + ffn_out, lnf_ref[...])

        return ffn, finish

    @pl.when(i == 0)
    def _():
        for _ in mixer():
            pass

    @pl.when(jnp.logical_and(i > 0, i < n_chunks))
    def _():
        ffn, finish = make_ffn()
        for n in mixer():
            ffn(n)
        finish()

    @pl.when(i == n_chunks)
    def _():
        _, finish = make_ffn()
        finish()


def kernel(x, ln1_g, w_in, pool_w, pool_scale, conv_w, conv_b, w_a, b_a, w_i, b_i, lam, gn_pool_g,
           gn_lru_g, w_out, ln2_g, w_ffn_gate, w_ffn_up, w_ffn_down, lnf_g):
    B, S, D = x.shape
    assert D == D_MODEL and S % SEQ_TILE == 0 and ln1_g.shape[0] == 1
    bf16 = jnp.bfloat16
    vmem = pl.BlockSpec(memory_space=pltpu.VMEM)
    hbm = pl.BlockSpec(memory_space=pl.ANY)
    nq = LRU_WIDTH // LANES
    scan_rows = SUBLANES * SUB_PITCH
    cps = S // SEQ_TILE
    n_chunks = B * cps

    def in_index(i):
        c = jnp.minimum(i, n_chunks - 1)
        return (c // cps, c % cps, 0)

    def out_index(i):
        c = jnp.maximum(i - 1, 0)
        return (c // cps, c % cps, 0)

    out = pl.pallas_call(
        functools.partial(_layer_kernel, chunks_per_seq=cps, n_chunks=n_chunks),
        grid=(n_chunks + 1,),
        in_specs=[pl.BlockSpec((None, SEQ_TILE, D), in_index)] + [
            hbm if k in HBM_WEIGHT_OPERANDS else vmem for k in range(1, 20)],
        out_specs=pl.BlockSpec((None, SEQ_TILE, D), out_index),
        out_shape=jax.ShapeDtypeStruct((B, S, D), jnp.float32),
        scratch_shapes=[
            pltpu.VMEM((POOL_TAIL + SEQ_TILE, POOL_WIDTH), jnp.float32),
            pltpu.VMEM((CONV_TAIL + SEQ_TILE, LRU_WIDTH), jnp.float32),
            pltpu.VMEM((nq, SEQ_TILE, LANES), jnp.float32),
            pltpu.VMEM((nq, SEQ_TILE, LANES), jnp.float32),
            pltpu.VMEM((nq, scan_rows, LANES), jnp.float32),
            pltpu.VMEM((1, LRU_WIDTH), jnp.float32),
            pltpu.VMEM((2, SEQ_TILE, D), jnp.float32),
            pltpu.VMEM((2, SEQ_TILE, D), jnp.bfloat16),
            pltpu.VMEM((D, IN_WIDTH), bf16),
            pltpu.VMEM((D, D), bf16),
            pltpu.VMEM((D, 2 * D_FF), bf16),
            pltpu.VMEM((D_FF, D), bf16),
            pltpu.VMEM((2, LRU_WIDTH // 2, LRU_WIDTH + LRU_WIDTH // 2), bf16),
            pltpu.VMEM((WEIGHT_STAGE_SLOTS, WEIGHT_STAGE_ROWS, D_FF), jnp.float32),
            pltpu.SemaphoreType.DMA((WEIGHT_STAGE_SLOTS,)),
        ],
        compiler_params=pltpu.CompilerParams(
            dimension_semantics=("arbitrary",),
            vmem_limit_bytes=VMEM_LIMIT_BYTES),
        name="hymba_layer",
    )(x, ln1_g, w_in[0], pool_w, pool_scale, conv_w, conv_b, w_a, b_a, w_i, b_i, lam, gn_pool_g, gn_lru_g,
      w_out[0], ln2_g, w_ffn_gate[0], w_ffn_up[0], w_ffn_down[0], lnf_g.reshape(1, D))
    return out
```

```python
import functools
import math

import jax
import jax.numpy as jnp
from jax import lax
from jax.experimental import pallas as pl
from jax.experimental.pallas import tpu as pltpu

D_MODEL = 1024
POOL_WINDOWS = (2, 4, 8, 16)
POOL_GROUP = 128
POOL_WIDTH = 512
LRU_WIDTH = 512
LRU_HEADS = 8
LRU_HEAD_DIM = 64
CONV_WIDTH = 4
LRU_C = 8.0
D_FF = 2816
IN_WIDTH = POOL_WIDTH + 2 * LRU_WIDTH
EPS = 1e-6

LANES = 128
SUBLANES = 8
SEQ_TILE = 512
POOL_TAIL = 16
CONV_TAIL = 8
SUB_LEN = SEQ_TILE // SUBLANES
SUB_PITCH = SUB_LEN + SUBLANES
FF_CHUNK = 512
N_FF_CHUNKS = -(-D_FF // FF_CHUNK)
FF_DOWN_GROUP = 3
GATE_BLOCKS = 1
FFN_PIECES_AT = (1, 1, 1, (1,), 1, 1, 1, 1)
HBM_WEIGHT_OPERANDS = (2, 14, 16, 17, 18)
WEIGHT_STAGE_ROWS = 128
WEIGHT_STAGE_SLOTS = 6
VMEM_LIMIT_BYTES = 58 * 1024 * 1024


def _ffn_sequence():
    ops = []
    for c in range(N_FF_CHUNKS):
        ops.append(("u", c))
        if c % FF_DOWN_GROUP == 0 and c > 0:
            ops.append(("d", c - FF_DOWN_GROUP))
    last = (N_FF_CHUNKS - 1) // FF_DOWN_GROUP * FF_DOWN_GROUP
    return ops + [("d", last)]


def _ff_chunk_cols():
    return tuple((c * FF_CHUNK, min((c + 1) * FF_CHUNK, D_FF)) for c in range(N_FF_CHUNKS))


def _rms(x, g):
    ms = jnp.mean(x * x, axis=-1, keepdims=True)
    return x * lax.rsqrt(ms + EPS) * g


def _sigmoid(x):
    return 0.5 * (jnp.tanh(0.5 * x) + 1.0)


def _gelu_tanh_times(x, h):
    c = math.sqrt(2.0 / math.pi)
    t = jnp.tanh(x * ((x * x) * (0.044715 * c) + c))
    return (h * x) * (0.5 * t + 0.5)


def _dot(a, b):
    return jnp.dot(a, b, preferred_element_type=jnp.float32)


def _load_weights_bf16(weights, stage, sems):
    chunks = [(src, dst, scale, col_map or ((0, src.shape[1], 0),), r * WEIGHT_STAGE_ROWS)
              for src, dst, scale, col_map in weights for r in range(src.shape[0] // WEIGHT_STAGE_ROWS)]

    def copy(k):
        src, _, _, _, row0 = chunks[k]
        slot = k % WEIGHT_STAGE_SLOTS
        return pltpu.make_async_copy(src.at[pl.ds(row0, WEIGHT_STAGE_ROWS), :],
                                     stage.at[slot, :, pl.ds(0, src.shape[1])], sems.at[slot])

    ahead = WEIGHT_STAGE_SLOTS - 1
    for k in range(min(ahead, len(chunks))):
        copy(k).start(priority=k % 2)
    for k, (_, dst, scale, col_map, row0) in enumerate(chunks):
        if k + ahead < len(chunks):
            copy(k + ahead).start(priority=(k + ahead) % 2)
        copy(k).wait()
        for c0, c1, d0 in col_map:
            w = stage[k % WEIGHT_STAGE_SLOTS, :, c0:c1]
            dst[row0:row0 + WEIGHT_STAGE_ROWS, d0:d0 + c1 - c0] = (
                w if scale is None else w * scale).astype(jnp.bfloat16)


def _build_small_weights(pool_w_ref, w_a_ref, w_i_ref, small_w_ref):
    half = LRU_WIDTH // 2
    heads = half // LRU_HEAD_DIM
    bf16 = jnp.bfloat16
    c_idx = lax.broadcasted_iota(jnp.int32, (LRU_HEAD_DIM, half), 0)
    j_idx = lax.broadcasted_iota(jnp.int32, (LRU_HEAD_DIM, half), 1)
    spread = (lax.rem(j_idx, LRU_HEAD_DIM) == c_idx).astype(bf16)
    col_head = j_idx // LRU_HEAD_DIM
    for q in range(2):
        for part, w_ref in enumerate((w_a_ref, w_i_ref)):
            for h in range(heads):
                rep = _dot(w_ref[0, heads * q + h].astype(bf16), spread)
                blk = jnp.where(col_head == h, rep, 0.0).astype(bf16)
                small_w_ref[q, h * LRU_HEAD_DIM:(h + 1) * LRU_HEAD_DIM, part * half:(part + 1) * half] = blk
        small_w_ref[q, :, 2 * half:3 * half] = jnp.zeros((half, half), bf16)
        for g in range(2):
            rows = slice(g * POOL_GROUP, (g + 1) * POOL_GROUP)
            cols = slice(2 * half + g * POOL_GROUP, 2 * half + (g + 1) * POOL_GROUP)
            small_w_ref[q, rows, cols] = pool_w_ref[0, 2 * q + g].astype(bf16)


def _layer_kernel(x_ref, ln1_ref, w_in_hbm, pool_w_ref, pool_scale_ref, conv_w_ref, conv_b_ref,
                  w_a_ref, b_a_ref, w_i_ref, b_i_ref, lam_ref, gn_pool_ref, gn_lru_ref, w_out_hbm,
                  ln2_ref, wg_hbm, wu_hbm, wd_hbm, lnf_ref, o_ref,
                  pbuf, lbuf, abuf, bbuf, hbuf, hcarry, h1buf, n2buf,
                  w_in_ref, w_out_ref, wgu_ref, wd_ref, small_w_ref, stage, stage_sems,
                  *, chunks_per_seq, n_chunks):
    i = pl.program_id(0)
    s = lax.rem(i, chunks_per_seq)
    T = SEQ_TILE
    half = LRU_WIDTH // 2

    @pl.when(i == 0)
    def _():
        g_map = tuple((c0, c1, 2 * c0) for c0, c1 in _ff_chunk_cols())
        u_map = tuple((c0, c1, c0 + c1) for c0, c1 in _ff_chunk_cols())
        _load_weights_bf16(((w_in_hbm, w_in_ref, None, None), (w_out_hbm, w_out_ref, None, None),
                            (wg_hbm, wgu_ref, 0.5, g_map), (wu_hbm, wgu_ref, None, u_map),
                            (wd_hbm, wd_ref, None, None)), stage, stage_sems)
        _build_small_weights(pool_w_ref, w_a_ref, w_i_ref, small_w_ref)

    slot = lax.rem(i, 2)
    prev = 1 - slot

    def mixer():
        @pl.when(s == 0)
        def _():
            pbuf[0:POOL_TAIL, :] = jnp.zeros((POOL_TAIL, POOL_WIDTH), jnp.float32)
            lbuf[0:CONV_TAIL, :] = jnp.zeros((CONV_TAIL, LRU_WIDTH), jnp.float32)
            hcarry[...] = jnp.zeros_like(hcarry)

        yield FFN_PIECES_AT[0]
        nb = _rms(x_ref[...], ln1_ref[...]).astype(jnp.bfloat16)

        lbuf[CONV_TAIL:CONV_TAIL + T, :] = _dot(nb, w_in_ref[:, POOL_WIDTH:POOL_WIDTH + LRU_WIDTH])
        u_gate = _dot(nb, w_in_ref[:, POOL_WIDTH + LRU_WIDTH:])

        yield FFN_PIECES_AT[2]
        assert CONV_WIDTH == 4
        u_ext = lbuf[...]
        u_prev = pltpu.roll(u_ext, 1, axis=0)
        tap = lambda k: conv_w_ref[0, k:k + 1, :]
        older = tap(1) * u_ext + tap(0) * u_prev
        xc = (conv_b_ref[...] + tap(3) * u_ext + tap(2) * u_prev + pltpu.roll(older, 2, axis=0))[CONV_TAIL:]
        xcb = xc.astype(jnp.bfloat16)
        neg_lam = -lam_ref[...]
        softplus = jnp.maximum(neg_lam, 0.0) + jnp.log1p(jnp.exp(-jnp.abs(neg_lam)))
        k_row = (0.5 * LRU_C) * softplus

        nq = LRU_WIDTH // LANES
        tiles_per_sub = SUB_LEN // SUBLANES
        blk_rows = T // GATE_BLOCKS
        for blk in range(GATE_BLOCKS):
            rows_blk = slice(blk * blk_rows, (blk + 1) * blk_rows)
            z = [_dot(xcb[rows_blk, q * half:(q + 1) * half], small_w_ref[q, :, 0:2 * half])
                 for q in range(2)]
            yield FFN_PIECES_AT[3][blk]
            za = jnp.concatenate([z[0][:, :half], z[1][:, :half]], axis=-1) + b_a_ref[...]
            zi = jnp.concatenate([z[0][:, half:], z[1][:, half:]], axis=-1) + b_i_ref[...]
            neg_log_a = jnp.tanh(0.5 * za) * k_row + k_row
            a = jnp.exp(-neg_log_a)
            one_minus_a2 = jnp.tanh(neg_log_a) * (a * a + 1.0)
            mult = jnp.where(one_minus_a2 > 0.0, one_minus_a2 * lax.rsqrt(one_minus_a2), 0.0)
            bterm = mult * (_sigmoid(zi) * xc[rows_blk, :])
            for q in range(nq):
                lanes = slice(q * LANES, (q + 1) * LANES)
                for t in range(blk_rows // SUBLANES):
                    j, m0 = divmod(blk * (blk_rows // SUBLANES) + t, tiles_per_sub)
                    dst = pl.ds(m0 * SUBLANES * SUBLANES + j, SUBLANES, stride=SUBLANES)
                    rows = slice(t * SUBLANES, (t + 1) * SUBLANES)
                    abuf[q, dst, :] = a[rows, lanes]
                    bbuf[q, dst, :] = bterm[rows, lanes]

        def tile(m):
            return slice(m * SUBLANES, (m + 1) * SUBLANES)

        yield FFN_PIECES_AT[4]
        prod = [jnp.ones((SUBLANES, LANES), jnp.float32) for _ in range(nq)]
        loc = [jnp.zeros((SUBLANES, LANES), jnp.float32) for _ in range(nq)]
        for m in range(SUB_LEN):
            for q in range(nq):
                am = abuf[q, tile(m), :]
                loc[q] = am * loc[q] + bbuf[q, tile(m), :]
                prod[q] = am * prod[q]
        row = lax.broadcasted_iota(jnp.int32, (SUBLANES, LANES), 0)
        h_in = []
        for q in range(nq):
            carry = hcarry[:, q * LANES:(q + 1) * LANES]
            start = jnp.zeros((SUBLANES, LANES), jnp.float32)
            for j in range(SUBLANES):
                start = jnp.where(row == j, carry, start)
                carry = prod[q][j:j + 1, :] * carry + loc[q][j:j + 1, :]
            h_in.append(start)
            hcarry[:, q * LANES:(q + 1) * LANES] = carry
        yield FFN_PIECES_AT[5]
        for m in range(SUB_LEN):
            for q in range(nq):
                h_in[q] = abuf[q, tile(m), :] * h_in[q] + bbuf[q, tile(m), :]
                hbuf[q, pl.ds(m, SUBLANES, stride=SUB_PITCH), :] = h_in[q]
        h_lru = jnp.concatenate(
            [jnp.concatenate([hbuf[q, j * SUB_PITCH:j * SUB_PITCH + SUB_LEN, :] for j in range(SUBLANES)], axis=0)
             for q in range(nq)], axis=-1)

        pbuf[POOL_TAIL:POOL_TAIL + T, :] = _dot(nb, w_in_ref[:, 0:POOL_WIDTH])
        yield FFN_PIECES_AT[1]
        n_groups = len(POOL_WINDOWS)
        level = pbuf[...]
        sums = []
        for g, w in enumerate(POOL_WINDOWS):
            assert w == 2 << g and w - 1 <= POOL_TAIL
            rest = level[:, (0 if g == 0 else POOL_GROUP):]
            rest = rest + pltpu.roll(rest, w // 2, axis=0)
            sums.append(rest[POOL_TAIL:, 0:POOL_GROUP])
            level = rest
        head_pos = (s * T + 1 + lax.broadcasted_iota(jnp.int32, (POOL_TAIL, POOL_GROUP), 0)).astype(jnp.float32)
        d_parts = []
        for g, w in enumerate(POOL_WINDOWS):
            ug = pbuf[POOL_TAIL:POOL_TAIL + T, g * POOL_GROUP:(g + 1) * POOL_GROUP]
            head = sums[g][0:POOL_TAIL] / jnp.minimum(head_pos, float(w))
            mean = jnp.concatenate([head, sums[g][POOL_TAIL:] * (1.0 / w)], axis=0)
            d_parts.append((mean - ug).astype(jnp.bfloat16))
        y_parts = []
        for p in range(n_groups // 2):
            d_pair = jnp.concatenate(d_parts[2 * p:2 * p + 2], axis=-1)
            y_parts.append(_dot(d_pair, small_w_ref[p, :, 2 * half:3 * half]))
        y_pool = jnp.concatenate(y_parts, axis=-1) * pool_scale_ref[...]
        mix_pool = _rms(y_pool, gn_pool_ref[...]).astype(jnp.bfloat16)

        yield FFN_PIECES_AT[6]
        y_lru = _gelu_tanh_times(u_gate, h_lru)
        mix_lru = _rms(y_lru, gn_lru_ref[...]).astype(jnp.bfloat16)

        mix = jnp.concatenate([mix_pool, mix_lru], axis=-1)
        h1_new = x_ref[...] + _dot(mix, w_out_ref[...])
        yield FFN_PIECES_AT[7]
        h1buf[slot] = h1_new
        n2buf[slot] = _rms(h1_new, ln2_ref[...]).astype(jnp.bfloat16)

        pbuf[0:POOL_TAIL, :] = pbuf[T:T + POOL_TAIL, :]
        lbuf[0:CONV_TAIL, :] = lbuf[T:T + CONV_TAIL, :]

    def make_ffn():
        acts = {}
        ffn_parts = []
        pending = _ffn_sequence()

        def ffn(n):
            for _ in range(n):
                op, c = pending.pop(0)
                if op == "u":
                    c0, c1 = _ff_chunk_cols()[c]
                    n2 = n2buf[prev]
                    gu = _dot(n2, wgu_ref[:, 2 * c0:2 * c1])
                    half_g, up = gu[:, :c1 - c0], gu[:, c1 - c0:]
                    acts[c] = (half_g * (jnp.tanh(half_g) + 1.0) * up).astype(jnp.bfloat16)
                else:
                    group = range(c, min(c + FF_DOWN_GROUP, N_FF_CHUNKS))
                    act = jnp.concatenate([acts.pop(k) for k in group], axis=-1)
                    part = _dot(act, wd_ref[c * FF_CHUNK:min((group[-1] + 1) * FF_CHUNK, D_FF), :])
                    if pending:
                        h1buf[prev] = h1buf[prev] + part
                    else:
                        ffn_parts.append(part)

        def finish():
            ffn(len(pending))
            o_ref[...] = _rms(h1buf[prev] + ffn_parts[0], lnf_ref[...])

        return ffn, finish

    @pl.when(i == 0)
    def _():
        for _ in mixer():
            pass

    @pl.when(jnp.logical_and(i > 0, i < n_chunks))
    def _():
        ffn, finish = make_ffn()
        for n in mixer():
            ffn(n)
        finish()

    @pl.when(i == n_chunks)
    def _():
        _, finish = make_ffn()
        finish()


def kernel(x, ln1_g, w_in, pool_w, pool_scale, conv_w, conv_b, w_a, b_a, w_i, b_i, lam, gn_pool_g,
           gn_lru_g, w_out, ln2_g, w_ffn_gate, w_ffn_up, w_ffn_down, lnf_g):
    B, S, D = x.shape
    assert D == D_MODEL and S % SEQ_TILE == 0 and ln1_g.shape[0] == 1
    bf16 = jnp.bfloat16
    vmem = pl.BlockSpec(memory_space=pltpu.VMEM)
    hbm = pl.BlockSpec(memory_space=pl.ANY)
    nq = LRU_WIDTH // LANES
    scan_rows = SUBLANES * SUB_PITCH
    cps = S // SEQ_TILE
    n_chunks = B * cps

    def in_index(i):
        c = jnp.minimum(i, n_chunks - 1)
        return (c // cps, c % cps, 0)

    def out_index(i):
        c = jnp.maximum(i - 1, 0)
        return (c // cps, c % cps, 0)

    out = pl.pallas_call(
        functools.partial(_layer_kernel, chunks_per_seq=cps, n_chunks=n_chunks),
        grid=(n_chunks + 1,),
        in_specs=[pl.BlockSpec((None, SEQ_TILE, D), in_index)] + [
            hbm if k in HBM_WEIGHT_OPERANDS else vmem for k in range(1, 20)],
        out_specs=pl.BlockSpec((None, SEQ_TILE, D), out_index),
        out_shape=jax.ShapeDtypeStruct((B, S, D), jnp.float32),
        scratch_shapes=[
            pltpu.VMEM((POOL_TAIL + SEQ_TILE, POOL_WIDTH), jnp.float32),
            pltpu.VMEM((CONV_TAIL + SEQ_TILE, LRU_WIDTH), jnp.float32),
            pltpu.VMEM((nq, SEQ_TILE, LANES), jnp.float32),
            pltpu.VMEM((nq, SEQ_TILE, LANES), jnp.float32),
            pltpu.VMEM((nq, scan_rows, LANES), jnp.float32),
            pltpu.VMEM((1, LRU_WIDTH), jnp.float32),
            pltpu.VMEM((2, SEQ_TILE, D), jnp.float32),
            pltpu.VMEM((2, SEQ_TILE, D), jnp.bfloat16),
            pltpu.VMEM((D, IN_WIDTH), bf16),
            pltpu.VMEM((D, D), bf16),
            pltpu.VMEM((D, 2 * D_FF), bf16),
            pltpu.VMEM((D_FF, D), bf16),
            pltpu.VMEM((2, LRU_WIDTH // 2, LRU_WIDTH + LRU_WIDTH // 2), bf16),
            pltpu.VMEM((WEIGHT_STAGE_SLOTS, WEIGHT_STAGE_ROWS, D_FF), jnp.float32),
            pltpu.SemaphoreType.DMA((WEIGHT_STAGE_SLOTS,)),
        ],
        compiler_params=pltpu.CompilerParams(
            dimension_semantics=("arbitrary",),
            vmem_limit_bytes=VMEM_LIMIT_BYTES),
        name="hymba_layer",
    )(x, ln1_g, w_in[0], pool_w, pool_scale, conv_w, conv_b, w_a, b_a, w_i, b_i, lam, gn_pool_g, gn_lru_g,
      w_out[0], ln2_g, w_ffn_gate[0], w_ffn_up[0], w_ffn_down[0], lnf_g.reshape(1, D))
    return out
```

```python
import functools
import math

import jax
import jax.numpy as jnp
from jax import lax
from jax.experimental import pallas as pl
from jax.experimental.pallas import tpu as pltpu

D_MODEL = 1024
POOL_WINDOWS = (2, 4, 8, 16)
POOL_GROUP = 128
POOL_WIDTH = 512
LRU_WIDTH = 512
LRU_HEADS = 8
LRU_HEAD_DIM = 64
CONV_WIDTH = 4
LRU_C = 8.0
D_FF = 2816
IN_WIDTH = POOL_WIDTH + 2 * LRU_WIDTH
EPS = 1e-6

LANES = 128
SUBLANES = 8
SEQ_TILE = 512
POOL_TAIL = 16
CONV_TAIL = 8
SUB_LEN = SEQ_TILE // SUBLANES
SUB_PITCH = SUB_LEN + SUBLANES
FF_CHUNK = 512
N_FF_CHUNKS = -(-D_FF // FF_CHUNK)
FF_DOWN_GROUP = 3
GATE_BLOCKS = 1
FFN_PIECES_AT = (1, 1, 1, (1,), 1, 1, 1, 1)
HBM_WEIGHT_OPERANDS = (2, 14, 16, 17, 18)
WEIGHT_STAGE_ROWS = 128
WEIGHT_STAGE_SLOTS = 6
VMEM_LIMIT_BYTES = 58 * 1024 * 1024


def _ffn_sequence():
    ops = []
    for c in range(N_FF_CHUNKS):
        ops.append(("u", c))
        if c % FF_DOWN_GROUP == 0 and c > 0:
            ops.append(("d", c - FF_DOWN_GROUP))
    last = (N_FF_CHUNKS - 1) // FF_DOWN_GROUP * FF_DOWN_GROUP
    return ops + [("d", last)]


def _ff_chunk_cols():
    return tuple((c * FF_CHUNK, min((c + 1) * FF_CHUNK, D_FF)) for c in range(N_FF_CHUNKS))


def _rms(x, g=None):
    ms = jnp.mean(x * x, axis=-1, keepdims=True)
    y = x * lax.rsqrt(ms + EPS)
    return y if g is None else y * g


def _as_column(row):
    n = row.shape[1]
    r = lax.broadcasted_iota(jnp.int32, (n, n), 0)
    c = lax.broadcasted_iota(jnp.int32, (n, n), 1)
    return jnp.sum(jnp.where(r == c, row, 0.0), axis=1, keepdims=True)


def _sigmoid(x):
    return 0.5 * (jnp.tanh(0.5 * x) + 1.0)


def _gelu_tanh_times(x, h):
    c = math.sqrt(2.0 / math.pi)
    t = jnp.tanh(x * ((x * x) * (0.044715 * c) + c))
    return (h * x) * (0.5 * t + 0.5)


def _dot(a, b):
    return jnp.dot(a, b, preferred_element_type=jnp.float32)


def _load_weights_bf16(weights, stage, sems):
    chunks = [(src, dst, scale, col_map or ((0, src.shape[1], 0),), row_gain, r * WEIGHT_STAGE_ROWS)
              for src, dst, scale, col_map, row_gain in weights
              for r in range(src.shape[0] // WEIGHT_STAGE_ROWS)]

    def copy(k):
        src, _, _, _, _, row0 = chunks[k]
        slot = k % WEIGHT_STAGE_SLOTS
        return pltpu.make_async_copy(src.at[pl.ds(row0, WEIGHT_STAGE_ROWS), :],
                                     stage.at[slot, :, pl.ds(0, src.shape[1])], sems.at[slot])

    ahead = WEIGHT_STAGE_SLOTS - 1
    for k in range(min(ahead, len(chunks))):
        copy(k).start(priority=k % 2)
    for k, (_, dst, scale, col_map, row_gain, row0) in enumerate(chunks):
        if k + ahead < len(chunks):
            copy(k + ahead).start(priority=(k + ahead) % 2)
        copy(k).wait()
        factor = None if row_gain is None else _as_column(row_gain[:, row0:row0 + WEIGHT_STAGE_ROWS])
        if scale is not None:
            factor = scale if factor is None else factor * scale
        for c0, c1, d0 in col_map:
            w = stage[k % WEIGHT_STAGE_SLOTS, :, c0:c1]
            dst[row0:row0 + WEIGHT_STAGE_ROWS, d0:d0 + c1 - c0] = (
                w if factor is None else w * factor).astype(jnp.bfloat16)


def _build_small_weights(pool_w_ref, w_a_ref, w_i_ref, small_w_ref):
    half = LRU_WIDTH // 2
    heads = half // LRU_HEAD_DIM
    bf16 = jnp.bfloat16
    c_idx = lax.broadcasted_iota(jnp.int32, (LRU_HEAD_DIM, half), 0)
    j_idx = lax.broadcasted_iota(jnp.int32, (LRU_HEAD_DIM, half), 1)
    spread = (lax.rem(j_idx, LRU_HEAD_DIM) == c_idx).astype(bf16)
    col_head = j_idx // LRU_HEAD_DIM
    for q in range(2):
        for part, w_ref in enumerate((w_a_ref, w_i_ref)):
            for h in range(heads):
                rep = _dot(w_ref[0, heads * q + h].astype(bf16), spread)
                blk = jnp.where(col_head == h, rep, 0.0).astype(bf16)
                small_w_ref[q, h * LRU_HEAD_DIM:(h + 1) * LRU_HEAD_DIM, part * half:(part + 1) * half] = blk
        small_w_ref[q, :, 2 * half:3 * half] = jnp.zeros((half, half), bf16)
        for g in range(2):
            rows = slice(g * POOL_GROUP, (g + 1) * POOL_GROUP)
            cols = slice(2 * half + g * POOL_GROUP, 2 * half + (g + 1) * POOL_GROUP)
            small_w_ref[q, rows, cols] = pool_w_ref[0, 2 * q + g].astype(bf16)


def _layer_kernel(x_ref, ln1_ref, w_in_hbm, pool_w_ref, pool_scale_ref, conv_w_ref, conv_b_ref,
                  w_a_ref, b_a_ref, w_i_ref, b_i_ref, lam_ref, gn_pool_ref, gn_lru_ref, w_out_hbm,
                  ln2_ref, wg_hbm, wu_hbm, wd_hbm, lnf_ref, o_ref,
                  pbuf, lbuf, abuf, bbuf, hbuf, hcarry, h1buf, n2buf,
                  w_in_ref, w_out_ref, wgu_ref, wd_ref, small_w_ref, stage, stage_sems,
                  *, chunks_per_seq, n_chunks):
    i = pl.program_id(0)
    s = lax.rem(i, chunks_per_seq)
    T = SEQ_TILE
    half = LRU_WIDTH // 2

    @pl.when(i == 0)
    def _():
        g_map = tuple((c0, c1, 2 * c0) for c0, c1 in _ff_chunk_cols())
        u_map = tuple((c0, c1, c0 + c1) for c0, c1 in _ff_chunk_cols())
        ln1, ln2 = ln1_ref[...], ln2_ref[...]
        gn_mix = jnp.concatenate([gn_pool_ref[...], gn_lru_ref[...]], axis=-1)
        _load_weights_bf16(((w_in_hbm, w_in_ref, None, None, ln1), (w_out_hbm, w_out_ref, None, None, gn_mix),
                            (wg_hbm, wgu_ref, 0.5, g_map, ln2), (wu_hbm, wgu_ref, None, u_map, ln2),
                            (wd_hbm, wd_ref, None, None, None)), stage, stage_sems)
        _build_small_weights(pool_w_ref, w_a_ref, w_i_ref, small_w_ref)

    slot = lax.rem(i, 2)
    prev = 1 - slot

    def mixer():
        @pl.when(s == 0)
        def _():
            pbuf[0:POOL_TAIL, :] = jnp.zeros((POOL_TAIL, POOL_WIDTH), jnp.float32)
            lbuf[0:CONV_TAIL, :] = jnp.zeros((CONV_TAIL, LRU_WIDTH), jnp.float32)
            hcarry[...] = jnp.zeros_like(hcarry)

        yield FFN_PIECES_AT[0]
        nb = _rms(x_ref[...]).astype(jnp.bfloat16)

        lbuf[CONV_TAIL:CONV_TAIL + T, :] = _dot(nb, w_in_ref[:, POOL_WIDTH:POOL_WIDTH + LRU_WIDTH])
        u_gate = _dot(nb, w_in_ref[:, POOL_WIDTH + LRU_WIDTH:])

        yield FFN_PIECES_AT[2]
        assert CONV_WIDTH == 4
        u_ext = lbuf[...]
        u_prev = pltpu.roll(u_ext, 1, axis=0)
        tap = lambda k: conv_w_ref[0, k:k + 1, :]
        older = tap(1) * u_ext + tap(0) * u_prev
        xc = (conv_b_ref[...] + tap(3) * u_ext + tap(2) * u_prev + pltpu.roll(older, 2, axis=0))[CONV_TAIL:]
        xcb = xc.astype(jnp.bfloat16)
        neg_lam = -lam_ref[...]
        softplus = jnp.maximum(neg_lam, 0.0) + jnp.log1p(jnp.exp(-jnp.abs(neg_lam)))
        k_row = (0.5 * LRU_C) * softplus

        nq = LRU_WIDTH // LANES
        tiles_per_sub = SUB_LEN // SUBLANES
        blk_rows = T // GATE_BLOCKS
        for blk in range(GATE_BLOCKS):
            rows_blk = slice(blk * blk_rows, (blk + 1) * blk_rows)
            z = [_dot(xcb[rows_blk, q * half:(q + 1) * half], small_w_ref[q, :, 0:2 * half])
                 for q in range(2)]
            yield FFN_PIECES_AT[3][blk]
            za = jnp.concatenate([z[0][:, :half], z[1][:, :half]], axis=-1) + b_a_ref[...]
            zi = jnp.concatenate([z[0][:, half:], z[1][:, half:]], axis=-1) + b_i_ref[...]
            neg_log_a = jnp.tanh(0.5 * za) * k_row + k_row
            a = jnp.exp(-neg_log_a)
            one_minus_a2 = jnp.tanh(neg_log_a) * (a * a + 1.0)
            mult = jnp.where(one_minus_a2 > 0.0, one_minus_a2 * lax.rsqrt(one_minus_a2), 0.0)
            bterm = mult * (_sigmoid(zi) * xc[rows_blk, :])
            for q in range(nq):
                lanes = slice(q * LANES, (q + 1) * LANES)
                for t in range(blk_rows // SUBLANES):
                    j, m0 = divmod(blk * (blk_rows // SUBLANES) + t, tiles_per_sub)
                    dst = pl.ds(m0 * SUBLANES * SUBLANES + j, SUBLANES, stride=SUBLANES)
                    rows = slice(t * SUBLANES, (t + 1) * SUBLANES)
                    abuf[q, dst, :] = a[rows, lanes]
                    bbuf[q, dst, :] = bterm[rows, lanes]

        def tile(m):
            return slice(m * SUBLANES, (m + 1) * SUBLANES)

        yield FFN_PIECES_AT[4]
        prod = [jnp.ones((SUBLANES, LANES), jnp.float32) for _ in range(nq)]
        loc = [jnp.zeros((SUBLANES, LANES), jnp.float32) for _ in range(nq)]
        for m in range(SUB_LEN):
            for q in range(nq):
                am = abuf[q, tile(m), :]
                loc[q] = am * loc[q] + bbuf[q, tile(m), :]
                prod[q] = am * prod[q]
        row = lax.broadcasted_iota(jnp.int32, (SUBLANES, LANES), 0)
        h_in = []
        for q in range(nq):
            carry = hcarry[:, q * LANES:(q + 1) * LANES]
            start = jnp.zeros((SUBLANES, LANES), jnp.float32)
            for j in range(SUBLANES):
                start = jnp.where(row == j, carry, start)
                carry = prod[q][j:j + 1, :] * carry + loc[q][j:j + 1, :]
            h_in.append(start)
            hcarry[:, q * LANES:(q + 1) * LANES] = carry
        yield FFN_PIECES_AT[5]
        for m in range(SUB_LEN):
            for q in range(nq):
                h_in[q] = abuf[q, tile(m), :] * h_in[q] + bbuf[q, tile(m), :]
                hbuf[q, pl.ds(m, SUBLANES, stride=SUB_PITCH), :] = h_in[q]
        h_lru = jnp.concatenate(
            [jnp.concatenate([hbuf[q, j * SUB_PITCH:j * SUB_PITCH + SUB_LEN, :] for j in range(SUBLANES)], axis=0)
             for q in range(nq)], axis=-1)

        pbuf[POOL_TAIL:POOL_TAIL + T, :] = _dot(nb, w_in_ref[:, 0:POOL_WIDTH])
        yield FFN_PIECES_AT[1]
        n_groups = len(POOL_WINDOWS)
        level = pbuf[...]
        sums = []
        for g, w in enumerate(POOL_WINDOWS):
            assert w == 2 << g and w - 1 <= POOL_TAIL
            rest = level[:, (0 if g == 0 else POOL_GROUP):]
            rest = rest + pltpu.roll(rest, w // 2, axis=0)
            sums.append(rest[POOL_TAIL:, 0:POOL_GROUP])
            level = rest
        head_pos = (s * T + 1 + lax.broadcasted_iota(jnp.int32, (POOL_TAIL, POOL_GROUP), 0)).astype(jnp.float32)
        d_parts = []
        for g, w in enumerate(POOL_WINDOWS):
            ug = pbuf[POOL_TAIL:POOL_TAIL + T, g * POOL_GROUP:(g + 1) * POOL_GROUP]
            head = sums[g][0:POOL_TAIL] / jnp.minimum(head_pos, float(w))
            mean = jnp.concatenate([head, sums[g][POOL_TAIL:] * (1.0 / w)], axis=0)
            d_parts.append((mean - ug).astype(jnp.bfloat16))
        y_parts = []
        for p in range(n_groups // 2):
            d_pair = jnp.concatenate(d_parts[2 * p:2 * p + 2], axis=-1)
            y_parts.append(_dot(d_pair, small_w_ref[p, :, 2 * half:3 * half]))
        y_pool = jnp.concatenate(y_parts, axis=-1) * pool_scale_ref[...]
        mix_pool = _rms(y_pool).astype(jnp.bfloat16)

        yield FFN_PIECES_AT[6]
        y_lru = _gelu_tanh_times(u_gate, h_lru)
        mix_lru = _rms(y_lru).astype(jnp.bfloat16)

        mix = jnp.concatenate([mix_pool, mix_lru], axis=-1)
        h1_new = x_ref[...] + _dot(mix, w_out_ref[...])
        yield FFN_PIECES_AT[7]
        h1buf[slot] = h1_new
        n2buf[slot] = _rms(h1_new).astype(jnp.bfloat16)

        pbuf[0:POOL_TAIL, :] = pbuf[T:T + POOL_TAIL, :]
        lbuf[0:CONV_TAIL, :] = lbuf[T:T + CONV_TAIL, :]

    def make_ffn():
        acts = {}
        ffn_parts = []
        pending = _ffn_sequence()

        def ffn(n):
            for _ in range(n):
                op, c = pending.pop(0)
                if op == "u":
                    c0, c1 = _ff_chunk_cols()[c]
                    n2 = n2buf[prev]
                    gu = _dot(n2, wgu_ref[:, 2 * c0:2 * c1])
                    half_g, up = gu[:, :c1 - c0], gu[:, c1 - c0:]
                    acts[c] = (half_g * (jnp.tanh(half_g) + 1.0) * up).astype(jnp.bfloat16)
                else:
                    group = range(c, min(c + FF_DOWN_GROUP, N_FF_CHUNKS))
                    act = jnp.concatenate([acts.pop(k) for k in group], axis=-1)
                    ffn_parts.append(_dot(act, wd_ref[c * FF_CHUNK:min((group[-1] + 1) * FF_CHUNK, D_FF), :]))

        def finish():
            ffn(len(pending))
            ffn_out = ffn_parts[0]
            for part in ffn_parts[1:]:
                ffn_out = ffn_out + part
            o_ref[...] = _rms(h1buf[prev] + ffn_out, lnf_ref[...])

        return ffn, finish

    @pl.when(i == 0)
    def _():
        for _ in mixer():
            pass

    @pl.when(jnp.logical_and(i > 0, i < n_chunks))
    def _():
        ffn, finish = make_ffn()
        for n in mixer():
            ffn(n)
        finish()

    @pl.when(i == n_chunks)
    def _():
        _, finish = make_ffn()
        finish()


def kernel(x, ln1_g, w_in, pool_w, pool_scale, conv_w, conv_b, w_a, b_a, w_i, b_i, lam, gn_pool_g,
           gn_lru_g, w_out, ln2_g, w_ffn_gate, w_ffn_up, w_ffn_down, lnf_g):
    B, S, D = x.shape
    assert D == D_MODEL and S % SEQ_TILE == 0 and ln1_g.shape[0] == 1
    bf16 = jnp.bfloat16
    vmem = pl.BlockSpec(memory_space=pltpu.VMEM)
    hbm = pl.BlockSpec(memory_space=pl.ANY)
    nq = LRU_WIDTH // LANES
    scan_rows = SUBLANES * SUB_PITCH
    cps = S // SEQ_TILE
    n_chunks = B * cps

    def in_index(i):
        c = jnp.minimum(i, n_chunks - 1)
        return (c // cps, c % cps, 0)

    def out_index(i):
        c = jnp.maximum(i - 1, 0)
        return (c // cps, c % cps, 0)

    out = pl.pallas_call(
        functools.partial(_layer_kernel, chunks_per_seq=cps, n_chunks=n_chunks),
        grid=(n_chunks + 1,),
        in_specs=[pl.BlockSpec((None, SEQ_TILE, D), in_index)] + [
            hbm if k in HBM_WEIGHT_OPERANDS else vmem for k in range(1, 20)],
        out_specs=pl.BlockSpec((None, SEQ_TILE, D), out_index),
        out_shape=jax.ShapeDtypeStruct((B, S, D), jnp.float32),
        scratch_shapes=[
            pltpu.VMEM((POOL_TAIL + SEQ_TILE, POOL_WIDTH), jnp.float32),
            pltpu.VMEM((CONV_TAIL + SEQ_TILE, LRU_WIDTH), jnp.float32),
            pltpu.VMEM((nq, SEQ_TILE, LANES), jnp.float32),
            pltpu.VMEM((nq, SEQ_TILE, LANES), jnp.float32),
            pltpu.VMEM((nq, scan_rows, LANES), jnp.float32),
            pltpu.VMEM((1, LRU_WIDTH), jnp.float32),
            pltpu.VMEM((2, SEQ_TILE, D), jnp.float32),
            pltpu.VMEM((2, SEQ_TILE, D), jnp.bfloat16),
            pltpu.VMEM((D, IN_WIDTH), bf16),
            pltpu.VMEM((D, D), bf16),
            pltpu.VMEM((D, 2 * D_FF), bf16),
            pltpu.VMEM((D_FF, D), bf16),
            pltpu.VMEM((2, LRU_WIDTH // 2, LRU_WIDTH + LRU_WIDTH // 2), bf16),
            pltpu.VMEM((WEIGHT_STAGE_SLOTS, WEIGHT_STAGE_ROWS, D_FF), jnp.float32),
            pltpu.SemaphoreType.DMA((WEIGHT_STAGE_SLOTS,)),
        ],
        compiler_params=pltpu.CompilerParams(
            dimension_semantics=("arbitrary",),
            vmem_limit_bytes=VMEM_LIMIT_BYTES),
        name="hymba_layer",
    )(x, ln1_g, w_in[0], pool_w, pool_scale, conv_w, conv_b, w_a, b_a, w_i, b_i, lam, gn_pool_g, gn_lru_g,
      w_out[0], ln2_g, w_ffn_gate[0], w_ffn_up[0], w_ffn_down[0], lnf_g.reshape(1, D))
    return out
```

```python
import functools
import math

import jax
import jax.numpy as jnp
from jax import lax
from jax.experimental import pallas as pl
from jax.experimental.pallas import tpu as pltpu

D_MODEL = 1024
POOL_WINDOWS = (2, 4, 8, 16)
POOL_GROUP = 128
POOL_WIDTH = 512
LRU_WIDTH = 512
LRU_HEADS = 8
LRU_HEAD_DIM = 64
CONV_WIDTH = 4
LRU_C = 8.0
D_FF = 2816
IN_WIDTH = POOL_WIDTH + 2 * LRU_WIDTH
EPS = 1e-6

LANES = 128
SUBLANES = 8
SEQ_TILE = 512
POOL_TAIL = 16
CONV_TAIL = 8
SUB_LEN = SEQ_TILE // SUBLANES
SUB_PITCH = SUB_LEN + SUBLANES
FF_CHUNK = 512
N_FF_CHUNKS = -(-D_FF // FF_CHUNK)
FF_DOWN_GROUP = 3
GATE_BLOCKS = 1
FFN_PIECES_AT = (1, 1, 1, (1,), 1, 1, 1, 1)
HBM_WEIGHT_OPERANDS = (2, 14, 16, 17, 18)
WEIGHT_STAGE_ROWS = 128
WEIGHT_STAGE_SLOTS = 6
VMEM_LIMIT_BYTES = 58 * 1024 * 1024


def _ffn_sequence():
    ops = []
    for c in range(N_FF_CHUNKS):
        ops.append(("u", c))
        if c % FF_DOWN_GROUP == 0 and c > 0:
            ops.append(("d", c - FF_DOWN_GROUP))
    last = (N_FF_CHUNKS - 1) // FF_DOWN_GROUP * FF_DOWN_GROUP
    return ops + [("d", last)]


def _ff_chunk_cols():
    return tuple((c * FF_CHUNK, min((c + 1) * FF_CHUNK, D_FF)) for c in range(N_FF_CHUNKS))


def _rms(x, g):
    ms = jnp.mean(x * x, axis=-1, keepdims=True)
    return x * lax.rsqrt(ms + EPS) * g


def _sigmoid_of_twice(half_x):
    return 0.5 * (jnp.tanh(half_x) + 1.0)


def _gelu_tanh_times(x, h):
    c = math.sqrt(2.0 / math.pi)
    t = jnp.tanh(x * ((x * x) * (0.044715 * c) + c))
    return (h * x) * (0.5 * t + 0.5)


def _dot(a, b):
    return jnp.dot(a, b, preferred_element_type=jnp.float32)


def _load_weights_bf16(weights, stage, sems):
    chunks = [(src, dst, scale, col_map or ((0, src.shape[1], 0),), r * WEIGHT_STAGE_ROWS)
              for src, dst, scale, col_map in weights for r in range(src.shape[0] // WEIGHT_STAGE_ROWS)]

    def copy(k):
        src, _, _, _, row0 = chunks[k]
        slot = k % WEIGHT_STAGE_SLOTS
        return pltpu.make_async_copy(src.at[pl.ds(row0, WEIGHT_STAGE_ROWS), :],
                                     stage.at[slot, :, pl.ds(0, src.shape[1])], sems.at[slot])

    ahead = WEIGHT_STAGE_SLOTS - 1
    for k in range(min(ahead, len(chunks))):
        copy(k).start(priority=k % 2)
    for k, (_, dst, scale, col_map, row0) in enumerate(chunks):
        if k + ahead < len(chunks):
            copy(k + ahead).start(priority=(k + ahead) % 2)
        copy(k).wait()
        for c0, c1, d0 in col_map:
            w = stage[k % WEIGHT_STAGE_SLOTS, :, c0:c1]
            dst[row0:row0 + WEIGHT_STAGE_ROWS, d0:d0 + c1 - c0] = (
                w if scale is None else w * scale).astype(jnp.bfloat16)


def _build_small_weights(pool_w_ref, w_a_ref, w_i_ref, small_w_ref):
    half = LRU_WIDTH // 2
    heads = half // LRU_HEAD_DIM
    bf16 = jnp.bfloat16
    c_idx = lax.broadcasted_iota(jnp.int32, (LRU_HEAD_DIM, half), 0)
    j_idx = lax.broadcasted_iota(jnp.int32, (LRU_HEAD_DIM, half), 1)
    spread = (lax.rem(j_idx, LRU_HEAD_DIM) == c_idx).astype(bf16)
    col_head = j_idx // LRU_HEAD_DIM
    for q in range(2):
        for part, w_ref in enumerate((w_a_ref, w_i_ref)):
            for h in range(heads):
                rep = _dot((0.5 * w_ref[0, heads * q + h]).astype(bf16), spread)
                blk = jnp.where(col_head == h, rep, 0.0).astype(bf16)
                small_w_ref[q, h * LRU_HEAD_DIM:(h + 1) * LRU_HEAD_DIM, part * half:(part + 1) * half] = blk
        small_w_ref[q, :, 2 * half:3 * half] = jnp.zeros((half, half), bf16)
        for g in range(2):
            rows = slice(g * POOL_GROUP, (g + 1) * POOL_GROUP)
            cols = slice(2 * half + g * POOL_GROUP, 2 * half + (g + 1) * POOL_GROUP)
            small_w_ref[q, rows, cols] = pool_w_ref[0, 2 * q + g].astype(bf16)


def _layer_kernel(x_ref, ln1_ref, w_in_hbm, pool_w_ref, pool_scale_ref, conv_w_ref, conv_b_ref,
                  w_a_ref, b_a_ref, w_i_ref, b_i_ref, lam_ref, gn_pool_ref, gn_lru_ref, w_out_hbm,
                  ln2_ref, wg_hbm, wu_hbm, wd_hbm, lnf_ref, o_ref,
                  pbuf, lbuf, abuf, bbuf, hbuf, hcarry, h1buf, n2buf,
                  w_in_ref, w_out_ref, wgu_ref, wd_ref, small_w_ref, stage, stage_sems,
                  *, chunks_per_seq, n_chunks):
    i = pl.program_id(0)
    s = lax.rem(i, chunks_per_seq)
    T = SEQ_TILE
    half = LRU_WIDTH // 2

    @pl.when(i == 0)
    def _():
        g_map = tuple((c0, c1, 2 * c0) for c0, c1 in _ff_chunk_cols())
        u_map = tuple((c0, c1, c0 + c1) for c0, c1 in _ff_chunk_cols())
        _load_weights_bf16(((w_in_hbm, w_in_ref, None, None), (w_out_hbm, w_out_ref, None, None),
                            (wg_hbm, wgu_ref, 0.5, g_map), (wu_hbm, wgu_ref, None, u_map),
                            (wd_hbm, wd_ref, None, None)), stage, stage_sems)
        _build_small_weights(pool_w_ref, w_a_ref, w_i_ref, small_w_ref)

    slot = lax.rem(i, 2)
    prev = 1 - slot

    def mixer():
        @pl.when(s == 0)
        def _():
            pbuf[0:POOL_TAIL, :] = jnp.zeros((POOL_TAIL, POOL_WIDTH), jnp.float32)
            lbuf[0:CONV_TAIL, :] = jnp.zeros((CONV_TAIL, LRU_WIDTH), jnp.float32)
            hcarry[...] = jnp.zeros_like(hcarry)

        yield FFN_PIECES_AT[0]
        nb = _rms(x_ref[...], ln1_ref[...]).astype(jnp.bfloat16)

        lbuf[CONV_TAIL:CONV_TAIL + T, :] = _dot(nb, w_in_ref[:, POOL_WIDTH:POOL_WIDTH + LRU_WIDTH])
        u_gate = _dot(nb, w_in_ref[:, POOL_WIDTH + LRU_WIDTH:])

        yield FFN_PIECES_AT[2]
        assert CONV_WIDTH == 4
        u_ext = lbuf[...]
        u_prev = pltpu.roll(u_ext, 1, axis=0)
        tap = lambda k: conv_w_ref[0, k:k + 1, :]
        older = tap(1) * u_ext + tap(0) * u_prev
        xc = (conv_b_ref[...] + tap(3) * u_ext + tap(2) * u_prev + pltpu.roll(older, 2, axis=0))[CONV_TAIL:]
        xcb = xc.astype(jnp.bfloat16)
        neg_lam = -lam_ref[...]
        softplus = jnp.maximum(neg_lam, 0.0) + jnp.log1p(jnp.exp(-jnp.abs(neg_lam)))
        k_row = (0.5 * LRU_C) * softplus

        nq = LRU_WIDTH // LANES
        tiles_per_sub = SUB_LEN // SUBLANES
        blk_rows = T // GATE_BLOCKS
        for blk in range(GATE_BLOCKS):
            rows_blk = slice(blk * blk_rows, (blk + 1) * blk_rows)
            z = [_dot(xcb[rows_blk, q * half:(q + 1) * half], small_w_ref[q, :, 0:2 * half])
                 for q in range(2)]
            yield FFN_PIECES_AT[3][blk]
            half_za = jnp.concatenate([z[0][:, :half], z[1][:, :half]], axis=-1) + 0.5 * b_a_ref[...]
            half_zi = jnp.concatenate([z[0][:, half:], z[1][:, half:]], axis=-1) + 0.5 * b_i_ref[...]
            neg_log_a = jnp.tanh(half_za) * k_row + k_row
            a = jnp.exp(-neg_log_a)
            one_minus_a2 = jnp.tanh(neg_log_a) * (a * a + 1.0)
            mult = jnp.where(one_minus_a2 > 0.0, one_minus_a2 * lax.rsqrt(one_minus_a2), 0.0)
            bterm = mult * (_sigmoid_of_twice(half_zi) * xc[rows_blk, :])
            for q in range(nq):
                lanes = slice(q * LANES, (q + 1) * LANES)
                for t in range(blk_rows // SUBLANES):
                    j, m0 = divmod(blk * (blk_rows // SUBLANES) + t, tiles_per_sub)
                    dst = pl.ds(m0 * SUBLANES * SUBLANES + j, SUBLANES, stride=SUBLANES)
                    rows = slice(t * SUBLANES, (t + 1) * SUBLANES)
                    abuf[q, dst, :] = a[rows, lanes]
                    bbuf[q, dst, :] = bterm[rows, lanes]

        def tile(m):
            return slice(m * SUBLANES, (m + 1) * SUBLANES)

        yield FFN_PIECES_AT[4]
        prod = [jnp.ones((SUBLANES, LANES), jnp.float32) for _ in range(nq)]
        loc = [jnp.zeros((SUBLANES, LANES), jnp.float32) for _ in range(nq)]
        for m in range(SUB_LEN):
            for q in range(nq):
                am = abuf[q, tile(m), :]
                loc[q] = am * loc[q] + bbuf[q, tile(m), :]
                prod[q] = am * prod[q]
        row = lax.broadcasted_iota(jnp.int32, (SUBLANES, LANES), 0)
        h_in = []
        for q in range(nq):
            carry = hcarry[:, q * LANES:(q + 1) * LANES]
            start = jnp.zeros((SUBLANES, LANES), jnp.float32)
            for j in range(SUBLANES):
                start = jnp.where(row == j, carry, start)
                carry = prod[q][j:j + 1, :] * carry + loc[q][j:j + 1, :]
            h_in.append(start)
            hcarry[:, q * LANES:(q + 1) * LANES] = carry
        yield FFN_PIECES_AT[5]
        for m in range(SUB_LEN):
            for q in range(nq):
                h_in[q] = abuf[q, tile(m), :] * h_in[q] + bbuf[q, tile(m), :]
                hbuf[q, pl.ds(m, SUBLANES, stride=SUB_PITCH), :] = h_in[q]
        h_lru = jnp.concatenate(
            [jnp.concatenate([hbuf[q, j * SUB_PITCH:j * SUB_PITCH + SUB_LEN, :] for j in range(SUBLANES)], axis=0)
             for q in range(nq)], axis=-1)

        pbuf[POOL_TAIL:POOL_TAIL + T, :] = _dot(nb, w_in_ref[:, 0:POOL_WIDTH])
        yield FFN_PIECES_AT[1]
        n_groups = len(POOL_WINDOWS)
        level = pbuf[...]
        sums = []
        for g, w in enumerate(POOL_WINDOWS):
            assert w == 2 << g and w - 1 <= POOL_TAIL
            rest = level[:, (0 if g == 0 else POOL_GROUP):]
            rest = rest + pltpu.roll(rest, w // 2, axis=0)
            sums.append(rest[POOL_TAIL:, 0:POOL_GROUP])
            level = rest
        head_pos = (s * T + 1 + lax.broadcasted_iota(jnp.int32, (POOL_TAIL, POOL_GROUP), 0)).astype(jnp.float32)
        d_parts = []
        for g, w in enumerate(POOL_WINDOWS):
            ug = pbuf[POOL_TAIL:POOL_TAIL + T, g * POOL_GROUP:(g + 1) * POOL_GROUP]
            head = sums[g][0:POOL_TAIL] / jnp.minimum(head_pos, float(w))
            mean = jnp.concatenate([head, sums[g][POOL_TAIL:] * (1.0 / w)], axis=0)
            d_parts.append((mean - ug).astype(jnp.bfloat16))
        y_parts = []
        for p in range(n_groups // 2):
            d_pair = jnp.concatenate(d_parts[2 * p:2 * p + 2], axis=-1)
            y_parts.append(_dot(d_pair, small_w_ref[p, :, 2 * half:3 * half]))
        y_pool = jnp.concatenate(y_parts, axis=-1) * pool_scale_ref[...]
        mix_pool = _rms(y_pool, gn_pool_ref[...]).astype(jnp.bfloat16)

        yield FFN_PIECES_AT[6]
        y_lru = _gelu_tanh_times(u_gate, h_lru)
        mix_lru = _rms(y_lru, gn_lru_ref[...]).astype(jnp.bfloat16)

        mix = jnp.concatenate([mix_pool, mix_lru], axis=-1)
        h1_new = x_ref[...] + _dot(mix, w_out_ref[...])
        yield FFN_PIECES_AT[7]
        h1buf[slot] = h1_new
        n2buf[slot] = _rms(h1_new, ln2_ref[...]).astype(jnp.bfloat16)

        pbuf[0:POOL_TAIL, :] = pbuf[T:T + POOL_TAIL, :]
        lbuf[0:CONV_TAIL, :] = lbuf[T:T + CONV_TAIL, :]

    def make_ffn():
        acts = {}
        ffn_parts = []
        pending = _ffn_sequence()

        def ffn(n):
            for _ in range(n):
                op, c = pending.pop(0)
                if op == "u":
                    c0, c1 = _ff_chunk_cols()[c]
                    n2 = n2buf[prev]
                    gu = _dot(n2, wgu_ref[:, 2 * c0:2 * c1])
                    half_g, up = gu[:, :c1 - c0], gu[:, c1 - c0:]
                    acts[c] = (half_g * (jnp.tanh(half_g) + 1.0) * up).astype(jnp.bfloat16)
                else:
                    group = range(c, min(c + FF_DOWN_GROUP, N_FF_CHUNKS))
                    act = jnp.concatenate([acts.pop(k) for k in group], axis=-1)
                    ffn_parts.append(_dot(act, wd_ref[c * FF_CHUNK:min((group[-1] + 1) * FF_CHUNK, D_FF), :]))

        def finish():
            ffn(len(pending))
            ffn_out = ffn_parts[0]
            for part in ffn_parts[1:]:
                ffn_out = ffn_out + part
            o_ref[...] = _rms(h1buf[prev] + ffn_out, lnf_ref[...])

        return ffn, finish

    @pl.when(i == 0)
    def _():
        for _ in mixer():
            pass

    @pl.when(jnp.logical_and(i > 0, i < n_chunks))
    def _():
        ffn, finish = make_ffn()
        for n in mixer():
            ffn(n)
        finish()

    @pl.when(i == n_chunks)
    def _():
        _, finish = make_ffn()
        finish()


def kernel(x, ln1_g, w_in, pool_w, pool_scale, conv_w, conv_b, w_a, b_a, w_i, b_i, lam, gn_pool_g,
           gn_lru_g, w_out, ln2_g, w_ffn_gate, w_ffn_up, w_ffn_down, lnf_g):
    B, S, D = x.shape
    assert D == D_MODEL and S % SEQ_TILE == 0 and ln1_g.shape[0] == 1
    bf16 = jnp.bfloat16
    vmem = pl.BlockSpec(memory_space=pltpu.VMEM)
    hbm = pl.BlockSpec(memory_space=pl.ANY)
    nq = LRU_WIDTH // LANES
    scan_rows = SUBLANES * SUB_PITCH
    cps = S // SEQ_TILE
    n_chunks = B * cps

    def in_index(i):
        c = jnp.minimum(i, n_chunks - 1)
        return (c // cps, c % cps, 0)

    def out_index(i):
        c = jnp.maximum(i - 1, 0)
        return (c // cps, c % cps, 0)

    out = pl.pallas_call(
        functools.partial(_layer_kernel, chunks_per_seq=cps, n_chunks=n_chunks),
        grid=(n_chunks + 1,),
        in_specs=[pl.BlockSpec((None, SEQ_TILE, D), in_index)] + [
            hbm if k in HBM_WEIGHT_OPERANDS else vmem for k in range(1, 20)],
        out_specs=pl.BlockSpec((None, SEQ_TILE, D), out_index),
        out_shape=jax.ShapeDtypeStruct((B, S, D), jnp.float32),
        scratch_shapes=[
            pltpu.VMEM((POOL_TAIL + SEQ_TILE, POOL_WIDTH), jnp.float32),
            pltpu.VMEM((CONV_TAIL + SEQ_TILE, LRU_WIDTH), jnp.float32),
            pltpu.VMEM((nq, SEQ_TILE, LANES), jnp.float32),
            pltpu.VMEM((nq, SEQ_TILE, LANES), jnp.float32),
            pltpu.VMEM((nq, scan_rows, LANES), jnp.float32),
            pltpu.VMEM((1, LRU_WIDTH), jnp.float32),
            pltpu.VMEM((2, SEQ_TILE, D), jnp.float32),
            pltpu.VMEM((2, SEQ_TILE, D), jnp.bfloat16),
            pltpu.VMEM((D, IN_WIDTH), bf16),
            pltpu.VMEM((D, D), bf16),
            pltpu.VMEM((D, 2 * D_FF), bf16),
            pltpu.VMEM((D_FF, D), bf16),
            pltpu.VMEM((2, LRU_WIDTH // 2, LRU_WIDTH + LRU_WIDTH // 2), bf16),
            pltpu.VMEM((WEIGHT_STAGE_SLOTS, WEIGHT_STAGE_ROWS, D_FF), jnp.float32),
            pltpu.SemaphoreType.DMA((WEIGHT_STAGE_SLOTS,)),
        ],
        compiler_params=pltpu.CompilerParams(
            dimension_semantics=("arbitrary",),
            vmem_limit_bytes=VMEM_LIMIT_BYTES),
        name="hymba_layer",
    )(x, ln1_g, w_in[0], pool_w, pool_scale, conv_w, conv_b, w_a, b_a, w_i, b_i, lam, gn_pool_g, gn_lru_g,
      w_out[0], ln2_g, w_ffn_gate[0], w_ffn_up[0], w_ffn_down[0], lnf_g.reshape(1, D))
    return out
```

```python
import functools
import math

import jax
import jax.numpy as jnp
from jax import lax
from jax.experimental import pallas as pl
from jax.experimental.pallas import tpu as pltpu

D_MODEL = 1024
POOL_WINDOWS = (2, 4, 8, 16)
POOL_GROUP = 128
POOL_WIDTH = 512
LRU_WIDTH = 512
LRU_HEADS = 8
LRU_HEAD_DIM = 64
CONV_WIDTH = 4
LRU_C = 8.0
D_FF = 2816
IN_WIDTH = POOL_WIDTH + 2 * LRU_WIDTH
EPS = 1e-6

LANES = 128
SUBLANES = 8
SEQ_TILE = 512
POOL_TAIL = 16
CONV_TAIL = 8
SUB_LEN = SEQ_TILE // SUBLANES
SUB_PITCH = SUB_LEN + SUBLANES
FF_CHUNK = 512
N_FF_CHUNKS = -(-D_FF // FF_CHUNK)
FF_DOWN_GROUP = 3
GATE_BLOCKS = 1
FFN_PIECES_AT = (1, 1, 1, (1,), 1, 1, 1, 1)
HBM_WEIGHT_OPERANDS = (2, 14, 16, 17, 18)
WEIGHT_STAGE_ROWS = 128
WEIGHT_STAGE_SLOTS = 6
VMEM_LIMIT_BYTES = 58 * 1024 * 1024


def _ffn_sequence():
    ops = []
    for c in range(N_FF_CHUNKS):
        ops.append(("u", c))
        if c % FF_DOWN_GROUP == 0 and c > 0:
            ops.append(("d", c - FF_DOWN_GROUP))
    last = (N_FF_CHUNKS - 1) // FF_DOWN_GROUP * FF_DOWN_GROUP
    return ops + [("d", last)]


def _ff_chunk_cols():
    return tuple((c * FF_CHUNK, min((c + 1) * FF_CHUNK, D_FF)) for c in range(N_FF_CHUNKS))


def _rms(x, g):
    ms = jnp.mean(x * x, axis=-1, keepdims=True)
    return x * lax.rsqrt(ms + EPS) * g


def _sigmoid(x):
    return 0.5 * (jnp.tanh(0.5 * x) + 1.0)


def _gelu_tanh_times(x, h):
    c = math.sqrt(2.0 / math.pi)
    t = jnp.tanh(x * ((x * x) * (0.044715 * c) + c))
    return (h * x) * (0.5 * t + 0.5)


def _dot(a, b):
    return jnp.dot(a, b, preferred_element_type=jnp.float32)


def _load_weights_bf16(weights, stage, sems):
    chunks = [(src, dst, scale, col_map or ((0, src.shape[1], 0),), r * WEIGHT_STAGE_ROWS)
              for src, dst, scale, col_map in weights for r in range(src.shape[0] // WEIGHT_STAGE_ROWS)]

    def copy(k):
        src, _, _, _, row0 = chunks[k]
        slot = k % WEIGHT_STAGE_SLOTS
        return pltpu.make_async_copy(src.at[pl.ds(row0, WEIGHT_STAGE_ROWS), :],
                                     stage.at[slot, :, pl.ds(0, src.shape[1])], sems.at[slot])

    ahead = WEIGHT_STAGE_SLOTS - 1
    for k in range(min(ahead, len(chunks))):
        copy(k).start(priority=k % 2)
    for k, (_, dst, scale, col_map, row0) in enumerate(chunks):
        if k + ahead < len(chunks):
            copy(k + ahead).start(priority=(k + ahead) % 2)
        copy(k).wait()
        for c0, c1, d0 in col_map:
            w = stage[k % WEIGHT_STAGE_SLOTS, :, c0:c1]
            dst[row0:row0 + WEIGHT_STAGE_ROWS, d0:d0 + c1 - c0] = (
                w if scale is None else w * scale).astype(jnp.bfloat16)


def _build_small_weights(pool_w_ref, w_a_ref, w_i_ref, small_w_ref):
    half = LRU_WIDTH // 2
    heads = half // LRU_HEAD_DIM
    bf16 = jnp.bfloat16
    c_idx = lax.broadcasted_iota(jnp.int32, (LRU_HEAD_DIM, half), 0)
    j_idx = lax.broadcasted_iota(jnp.int32, (LRU_HEAD_DIM, half), 1)
    spread = (lax.rem(j_idx, LRU_HEAD_DIM) == c_idx).astype(bf16)
    col_head = j_idx // LRU_HEAD_DIM
    for q in range(2):
        for part, w_ref in enumerate((w_a_ref, w_i_ref)):
            for h in range(heads):
                rep = _dot(w_ref[0, heads * q + h].astype(bf16), spread)
                blk = jnp.where(col_head == h, rep, 0.0).astype(bf16)
                small_w_ref[q, h * LRU_HEAD_DIM:(h + 1) * LRU_HEAD_DIM, part * half:(part + 1) * half] = blk
        small_w_ref[q, :, 2 * half:3 * half] = jnp.zeros((half, half), bf16)
        for g in range(2):
            rows = slice(g * POOL_GROUP, (g + 1) * POOL_GROUP)
            cols = slice(2 * half + g * POOL_GROUP, 2 * half + (g + 1) * POOL_GROUP)
            small_w_ref[q, rows, cols] = pool_w_ref[0, 2 * q + g].astype(bf16)


def _layer_kernel(x_ref, ln1_ref, w_in_hbm, pool_w_ref, pool_scale_ref, conv_w_ref, conv_b_ref,
                  w_a_ref, b_a_ref, w_i_ref, b_i_ref, lam_ref, gn_pool_ref, gn_lru_ref, w_out_hbm,
                  ln2_ref, wg_hbm, wu_hbm, wd_hbm, lnf_ref, o_ref,
                  pbuf, lbuf, abuf, bbuf, hbuf, hcarry, h1buf, n2buf,
                  w_in_ref, w_out_ref, wgu_ref, wd_ref, small_w_ref, stage, stage_sems,
                  *, chunks_per_seq, n_chunks):
    i = pl.program_id(0)
    s = lax.rem(i, chunks_per_seq)
    T = SEQ_TILE
    half = LRU_WIDTH // 2

    @pl.when(i == 0)
    def _():
        g_map = tuple((c0, c1, 2 * c0) for c0, c1 in _ff_chunk_cols())
        u_map = tuple((c0, c1, c0 + c1) for c0, c1 in _ff_chunk_cols())
        _load_weights_bf16(((w_in_hbm, w_in_ref, None, None), (w_out_hbm, w_out_ref, None, None),
                            (wg_hbm, wgu_ref, 0.5, g_map), (wu_hbm, wgu_ref, None, u_map),
                            (wd_hbm, wd_ref, None, None)), stage, stage_sems)
        _build_small_weights(pool_w_ref, w_a_ref, w_i_ref, small_w_ref)

    slot = lax.rem(i, 2)
    prev = 1 - slot

    def mixer():
        @pl.when(s == 0)
        def _():
            pbuf[0:POOL_TAIL, :] = jnp.zeros((POOL_TAIL, POOL_WIDTH), jnp.float32)
            lbuf[0:CONV_TAIL, :] = jnp.zeros((CONV_TAIL, LRU_WIDTH), jnp.float32)
            hcarry[...] = jnp.zeros_like(hcarry)

        yield FFN_PIECES_AT[0]
        nb = _rms(x_ref[...], ln1_ref[...]).astype(jnp.bfloat16)

        lbuf[CONV_TAIL:CONV_TAIL + T, :] = _dot(nb, w_in_ref[:, POOL_WIDTH:POOL_WIDTH + LRU_WIDTH])
        u_gate = _dot(nb, w_in_ref[:, POOL_WIDTH + LRU_WIDTH:])
        pbuf[POOL_TAIL:POOL_TAIL + T, :] = _dot(nb, w_in_ref[:, 0:POOL_WIDTH])

        yield FFN_PIECES_AT[2]
        assert CONV_WIDTH == 4
        u_ext = lbuf[...]
        u_prev = pltpu.roll(u_ext, 1, axis=0)
        tap = lambda k: conv_w_ref[0, k:k + 1, :]
        older = tap(1) * u_ext + tap(0) * u_prev
        xc = (conv_b_ref[...] + tap(3) * u_ext + tap(2) * u_prev + pltpu.roll(older, 2, axis=0))[CONV_TAIL:]
        xcb = xc.astype(jnp.bfloat16)
        neg_lam = -lam_ref[...]
        softplus = jnp.maximum(neg_lam, 0.0) + jnp.log1p(jnp.exp(-jnp.abs(neg_lam)))
        k_row = (0.5 * LRU_C) * softplus

        nq = LRU_WIDTH // LANES
        tiles_per_sub = SUB_LEN // SUBLANES
        blk_rows = T // GATE_BLOCKS
        for blk in range(GATE_BLOCKS):
            rows_blk = slice(blk * blk_rows, (blk + 1) * blk_rows)
            z = [_dot(xcb[rows_blk, q * half:(q + 1) * half], small_w_ref[q, :, 0:2 * half])
                 for q in range(2)]
            yield FFN_PIECES_AT[3][blk]
            za = jnp.concatenate([z[0][:, :half], z[1][:, :half]], axis=-1) + b_a_ref[...]
            zi = jnp.concatenate([z[0][:, half:], z[1][:, half:]], axis=-1) + b_i_ref[...]
            neg_log_a = jnp.tanh(0.5 * za) * k_row + k_row
            a = jnp.exp(-neg_log_a)
            one_minus_a2 = jnp.tanh(neg_log_a) * (a * a + 1.0)
            mult = jnp.where(one_minus_a2 > 0.0, one_minus_a2 * lax.rsqrt(one_minus_a2), 0.0)
            bterm = mult * (_sigmoid(zi) * xc[rows_blk, :])
            for q in range(nq):
                lanes = slice(q * LANES, (q + 1) * LANES)
                for t in range(blk_rows // SUBLANES):
                    j, m0 = divmod(blk * (blk_rows // SUBLANES) + t, tiles_per_sub)
                    dst = pl.ds(m0 * SUBLANES * SUBLANES + j, SUBLANES, stride=SUBLANES)
                    rows = slice(t * SUBLANES, (t + 1) * SUBLANES)
                    abuf[q, dst, :] = a[rows, lanes]
                    bbuf[q, dst, :] = bterm[rows, lanes]

        def tile(m):
            return slice(m * SUBLANES, (m + 1) * SUBLANES)

        yield FFN_PIECES_AT[4]
        prod = [jnp.ones((SUBLANES, LANES), jnp.float32) for _ in range(nq)]
        loc = [jnp.zeros((SUBLANES, LANES), jnp.float32) for _ in range(nq)]
        for m in range(SUB_LEN):
            for q in range(nq):
                am = abuf[q, tile(m), :]
                loc[q] = am * loc[q] + bbuf[q, tile(m), :]
                prod[q] = am * prod[q]
        row = lax.broadcasted_iota(jnp.int32, (SUBLANES, LANES), 0)
        h_in = []
        for q in range(nq):
            carry = hcarry[:, q * LANES:(q + 1) * LANES]
            start = jnp.zeros((SUBLANES, LANES), jnp.float32)
            for j in range(SUBLANES):
                start = jnp.where(row == j, carry, start)
                carry = prod[q][j:j + 1, :] * carry + loc[q][j:j + 1, :]
            h_in.append(start)
            hcarry[:, q * LANES:(q + 1) * LANES] = carry
        yield FFN_PIECES_AT[5]
        for m in range(SUB_LEN):
            for q in range(nq):
                h_in[q] = abuf[q, tile(m), :] * h_in[q] + bbuf[q, tile(m), :]
                hbuf[q, pl.ds(m, SUBLANES, stride=SUB_PITCH), :] = h_in[q]
        h_lru = jnp.concatenate(
            [jnp.concatenate([hbuf[q, j * SUB_PITCH:j * SUB_PITCH + SUB_LEN, :] for j in range(SUBLANES)], axis=0)
             for q in range(nq)], axis=-1)

        yield FFN_PIECES_AT[1]
        n_groups = len(POOL_WINDOWS)
        level = pbuf[...]
        sums = []
        for g, w in enumerate(POOL_WINDOWS):
            assert w == 2 << g and w - 1 <= POOL_TAIL
            rest = level[:, (0 if g == 0 else POOL_GROUP):]
            rest = rest + pltpu.roll(rest, w // 2, axis=0)
            sums.append(rest[POOL_TAIL:, 0:POOL_GROUP])
            level = rest
        head_pos = (s * T + 1 + lax.broadcasted_iota(jnp.int32, (POOL_TAIL, POOL_GROUP), 0)).astype(jnp.float32)
        d_parts = []
        for g, w in enumerate(POOL_WINDOWS):
            ug = pbuf[POOL_TAIL:POOL_TAIL + T, g * POOL_GROUP:(g + 1) * POOL_GROUP]
            head = sums[g][0:POOL_TAIL] / jnp.minimum(head_pos, float(w))
            mean = jnp.concatenate([head, sums[g][POOL_TAIL:] * (1.0 / w)], axis=0)
            d_parts.append((mean - ug).astype(jnp.bfloat16))
        y_parts = []
        for p in range(n_groups // 2):
            d_pair = jnp.concatenate(d_parts[2 * p:2 * p + 2], axis=-1)
            y_parts.append(_dot(d_pair, small_w_ref[p, :, 2 * half:3 * half]))
        y_pool = jnp.concatenate(y_parts, axis=-1) * pool_scale_ref[...]
        mix_pool = _rms(y_pool, gn_pool_ref[...]).astype(jnp.bfloat16)

        yield FFN_PIECES_AT[6]
        y_lru = _gelu_tanh_times(u_gate, h_lru)
        mix_lru = _rms(y_lru, gn_lru_ref[...]).astype(jnp.bfloat16)

        mix = jnp.concatenate([mix_pool, mix_lru], axis=-1)
        h1_new = x_ref[...] + _dot(mix, w_out_ref[...])
        yield FFN_PIECES_AT[7]
        h1buf[slot] = h1_new
        n2buf[slot] = _rms(h1_new, ln2_ref[...]).astype(jnp.bfloat16)

        pbuf[0:POOL_TAIL, :] = pbuf[T:T + POOL_TAIL, :]
        lbuf[0:CONV_TAIL, :] = lbuf[T:T + CONV_TAIL, :]

    def make_ffn():
        acts = {}
        ffn_parts = []
        pending = _ffn_sequence()

        def ffn(n):
            for _ in range(n):
                op, c = pending.pop(0)
                if op == "u":
                    c0, c1 = _ff_chunk_cols()[c]
                    n2 = n2buf[prev]
                    gu = _dot(n2, wgu_ref[:, 2 * c0:2 * c1])
                    half_g, up = gu[:, :c1 - c0], gu[:, c1 - c0:]
                    acts[c] = (half_g * (jnp.tanh(half_g) + 1.0) * up).astype(jnp.bfloat16)
                else:
                    group = range(c, min(c + FF_DOWN_GROUP, N_FF_CHUNKS))
                    act = jnp.concatenate([acts.pop(k) for k in group], axis=-1)
                    ffn_parts.append(_dot(act, wd_ref[c * FF_CHUNK:min((group[-1] + 1) * FF_CHUNK, D_FF), :]))

        def finish():
            ffn(len(pending))
            ffn_out = ffn_parts[0]
            for part in ffn_parts[1:]:
                ffn_out = ffn_out + part
            o_ref[...] = _rms(h1buf[prev] + ffn_out, lnf_ref[...])

        return ffn, finish

    @pl.when(i == 0)
    def _():
        for _ in mixer():
            pass

    @pl.when(jnp.logical_and(i > 0, i < n_chunks))
    def _():
        ffn, finish = make_ffn()
        for n in mixer():
            ffn(n)
        finish()

    @pl.when(i == n_chunks)
    def _():
        _, finish = make_ffn()
        finish()


def kernel(x, ln1_g, w_in, pool_w, pool_scale, conv_w, conv_b, w_a, b_a, w_i, b_i, lam, gn_pool_g,
           gn_lru_g, w_out, ln2_g, w_ffn_gate, w_ffn_up, w_ffn_down, lnf_g):
    B, S, D = x.shape
    assert D == D_MODEL and S % SEQ_TILE == 0 and ln1_g.shape[0] == 1
    bf16 = jnp.bfloat16
    vmem = pl.BlockSpec(memory_space=pltpu.VMEM)
    hbm = pl.BlockSpec(memory_space=pl.ANY)
    nq = LRU_WIDTH // LANES
    scan_rows = SUBLANES * SUB_PITCH
    cps = S // SEQ_TILE
    n_chunks = B * cps

    def in_index(i):
        c = jnp.minimum(i, n_chunks - 1)
        return (c // cps, c % cps, 0)

    def out_index(i):
        c = jnp.maximum(i - 1, 0)
        return (c // cps, c % cps, 0)

    out = pl.pallas_call(
        functools.partial(_layer_kernel, chunks_per_seq=cps, n_chunks=n_chunks),
        grid=(n_chunks + 1,),
        in_specs=[pl.BlockSpec((None, SEQ_TILE, D), in_index)] + [
            hbm if k in HBM_WEIGHT_OPERANDS else vmem for k in range(1, 20)],
        out_specs=pl.BlockSpec((None, SEQ_TILE, D), out_index),
        out_shape=jax.ShapeDtypeStruct((B, S, D), jnp.float32),
        scratch_shapes=[
            pltpu.VMEM((POOL_TAIL + SEQ_TILE, POOL_WIDTH), jnp.float32),
            pltpu.VMEM((CONV_TAIL + SEQ_TILE, LRU_WIDTH), jnp.float32),
            pltpu.VMEM((nq, SEQ_TILE, LANES), jnp.float32),
            pltpu.VMEM((nq, SEQ_TILE, LANES), jnp.float32),
            pltpu.VMEM((nq, scan_rows, LANES), jnp.float32),
            pltpu.VMEM((1, LRU_WIDTH), jnp.float32),
            pltpu.VMEM((2, SEQ_TILE, D), jnp.float32),
            pltpu.VMEM((2, SEQ_TILE, D), jnp.bfloat16),
            pltpu.VMEM((D, IN_WIDTH), bf16),
            pltpu.VMEM((D, D), bf16),
            pltpu.VMEM((D, 2 * D_FF), bf16),
            pltpu.VMEM((D_FF, D), bf16),
            pltpu.VMEM((2, LRU_WIDTH // 2, LRU_WIDTH + LRU_WIDTH // 2), bf16),
            pltpu.VMEM((WEIGHT_STAGE_SLOTS, WEIGHT_STAGE_ROWS, D_FF), jnp.float32),
            pltpu.SemaphoreType.DMA((WEIGHT_STAGE_SLOTS,)),
        ],
        compiler_params=pltpu.CompilerParams(
            dimension_semantics=("arbitrary",),
            vmem_limit_bytes=VMEM_LIMIT_BYTES),
        name="hymba_layer",
    )(x, ln1_g, w_in[0], pool_w, pool_scale, conv_w, conv_b, w_a, b_a, w_i, b_i, lam, gn_pool_g, gn_lru_g,
      w_out[0], ln2_g, w_ffn_gate[0], w_ffn_up[0], w_ffn_down[0], lnf_g.reshape(1, D))
    return out
```

```python
import functools
import math

import jax
import jax.numpy as jnp
from jax import lax
from jax.experimental import pallas as pl
from jax.experimental.pallas import tpu as pltpu

D_MODEL = 1024
POOL_WINDOWS = (2, 4, 8, 16)
POOL_GROUP = 128
POOL_WIDTH = 512
LRU_WIDTH = 512
LRU_HEADS = 8
LRU_HEAD_DIM = 64
CONV_WIDTH = 4
LRU_C = 8.0
D_FF = 2816
IN_WIDTH = POOL_WIDTH + 2 * LRU_WIDTH
EPS = 1e-6

LANES = 128
SUBLANES = 8
SEQ_TILE = 512
POOL_TAIL = 16
CONV_TAIL = 8
SUB_LEN = SEQ_TILE // SUBLANES
SUB_PITCH = SUB_LEN + SUBLANES
FF_CHUNK = 512
N_FF_CHUNKS = -(-D_FF // FF_CHUNK)
FF_DOWN_GROUP = 3
GATE_BLOCKS = 1
FFN_PIECES_AT = (1, 1, 1, (1,), 1, 1, 1, 1)
HBM_WEIGHT_OPERANDS = (2, 14, 16, 17, 18)
WEIGHT_STAGE_ROWS = 128
WEIGHT_STAGE_SLOTS = 6
VMEM_LIMIT_BYTES = 58 * 1024 * 1024


def _ffn_sequence():
    ops = []
    for c in range(N_FF_CHUNKS):
        ops.append(("u", c))
        if c % FF_DOWN_GROUP == 1 and c > FF_DOWN_GROUP:
            ops.append(("d", c - 1 - FF_DOWN_GROUP))
    last = (N_FF_CHUNKS - 1) // FF_DOWN_GROUP * FF_DOWN_GROUP
    return ops + [("d", last)]


def _ff_chunk_cols():
    return tuple((c * FF_CHUNK, min((c + 1) * FF_CHUNK, D_FF)) for c in range(N_FF_CHUNKS))


def _rms(x, g):
    ms = jnp.mean(x * x, axis=-1, keepdims=True)
    return x * lax.rsqrt(ms + EPS) * g


def _sigmoid(x):
    return 0.5 * (jnp.tanh(0.5 * x) + 1.0)


def _gelu_tanh_times(x, h):
    c = math.sqrt(2.0 / math.pi)
    t = jnp.tanh(x * ((x * x) * (0.044715 * c) + c))
    return (h * x) * (0.5 * t + 0.5)


def _dot(a, b):
    return jnp.dot(a, b, preferred_element_type=jnp.float32)


def _load_weights_bf16(weights, stage, sems):
    chunks = [(src, dst, scale, col_map or ((0, src.shape[1], 0),), r * WEIGHT_STAGE_ROWS)
              for src, dst, scale, col_map in weights for r in range(src.shape[0] // WEIGHT_STAGE_ROWS)]

    def copy(k):
        src, _, _, _, row0 = chunks[k]
        slot = k % WEIGHT_STAGE_SLOTS
        return pltpu.make_async_copy(src.at[pl.ds(row0, WEIGHT_STAGE_ROWS), :],
                                     stage.at[slot, :, pl.ds(0, src.shape[1])], sems.at[slot])

    ahead = WEIGHT_STAGE_SLOTS - 1
    for k in range(min(ahead, len(chunks))):
        copy(k).start(priority=k % 2)
    for k, (_, dst, scale, col_map, row0) in enumerate(chunks):
        if k + ahead < len(chunks):
            copy(k + ahead).start(priority=(k + ahead) % 2)
        copy(k).wait()
        for c0, c1, d0 in col_map:
            w = stage[k % WEIGHT_STAGE_SLOTS, :, c0:c1]
            dst[row0:row0 + WEIGHT_STAGE_ROWS, d0:d0 + c1 - c0] = (
                w if scale is None else w * scale).astype(jnp.bfloat16)


def _build_small_weights(pool_w_ref, w_a_ref, w_i_ref, small_w_ref):
    half = LRU_WIDTH // 2
    heads = half // LRU_HEAD_DIM
    bf16 = jnp.bfloat16
    c_idx = lax.broadcasted_iota(jnp.int32, (LRU_HEAD_DIM, half), 0)
    j_idx = lax.broadcasted_iota(jnp.int32, (LRU_HEAD_DIM, half), 1)
    spread = (lax.rem(j_idx, LRU_HEAD_DIM) == c_idx).astype(bf16)
    col_head = j_idx // LRU_HEAD_DIM
    for q in range(2):
        for part, w_ref in enumerate((w_a_ref, w_i_ref)):
            for h in range(heads):
                rep = _dot(w_ref[0, heads * q + h].astype(bf16), spread)
                blk = jnp.where(col_head == h, rep, 0.0).astype(bf16)
                small_w_ref[q, h * LRU_HEAD_DIM:(h + 1) * LRU_HEAD_DIM, part * half:(part + 1) * half] = blk
        small_w_ref[q, :, 2 * half:3 * half] = jnp.zeros((half, half), bf16)
        for g in range(2):
            rows = slice(g * POOL_GROUP, (g + 1) * POOL_GROUP)
            cols = slice(2 * half + g * POOL_GROUP, 2 * half + (g + 1) * POOL_GROUP)
            small_w_ref[q, rows, cols] = pool_w_ref[0, 2 * q + g].astype(bf16)


def _layer_kernel(x_ref, ln1_ref, w_in_hbm, pool_w_ref, pool_scale_ref, conv_w_ref, conv_b_ref,
                  w_a_ref, b_a_ref, w_i_ref, b_i_ref, lam_ref, gn_pool_ref, gn_lru_ref, w_out_hbm,
                  ln2_ref, wg_hbm, wu_hbm, wd_hbm, lnf_ref, o_ref,
                  pbuf, lbuf, abuf, bbuf, hbuf, hcarry, h1buf, n2buf,
                  w_in_ref, w_out_ref, wgu_ref, wd_ref, small_w_ref, stage, stage_sems,
                  *, chunks_per_seq, n_chunks):
    i = pl.program_id(0)
    s = lax.rem(i, chunks_per_seq)
    T = SEQ_TILE
    half = LRU_WIDTH // 2

    @pl.when(i == 0)
    def _():
        g_map = tuple((c0, c1, 2 * c0) for c0, c1 in _ff_chunk_cols())
        u_map = tuple((c0, c1, c0 + c1) for c0, c1 in _ff_chunk_cols())
        _load_weights_bf16(((w_in_hbm, w_in_ref, None, None), (w_out_hbm, w_out_ref, None, None),
                            (wg_hbm, wgu_ref, 0.5, g_map), (wu_hbm, wgu_ref, None, u_map),
                            (wd_hbm, wd_ref, None, None)), stage, stage_sems)
        _build_small_weights(pool_w_ref, w_a_ref, w_i_ref, small_w_ref)

    slot = lax.rem(i, 2)
    prev = 1 - slot

    def mixer():
        @pl.when(s == 0)
        def _():
            pbuf[0:POOL_TAIL, :] = jnp.zeros((POOL_TAIL, POOL_WIDTH), jnp.float32)
            lbuf[0:CONV_TAIL, :] = jnp.zeros((CONV_TAIL, LRU_WIDTH), jnp.float32)
            hcarry[...] = jnp.zeros_like(hcarry)

        yield FFN_PIECES_AT[0]
        nb = _rms(x_ref[...], ln1_ref[...]).astype(jnp.bfloat16)

        lbuf[CONV_TAIL:CONV_TAIL + T, :] = _dot(nb, w_in_ref[:, POOL_WIDTH:POOL_WIDTH + LRU_WIDTH])
        u_gate = _dot(nb, w_in_ref[:, POOL_WIDTH + LRU_WIDTH:])

        yield FFN_PIECES_AT[2]
        assert CONV_WIDTH == 4
        u_ext = lbuf[...]
        u_prev = pltpu.roll(u_ext, 1, axis=0)
        tap = lambda k: conv_w_ref[0, k:k + 1, :]
        older = tap(1) * u_ext + tap(0) * u_prev
        xc = (conv_b_ref[...] + tap(3) * u_ext + tap(2) * u_prev + pltpu.roll(older, 2, axis=0))[CONV_TAIL:]
        xcb = xc.astype(jnp.bfloat16)
        neg_lam = -lam_ref[...]
        softplus = jnp.maximum(neg_lam, 0.0) + jnp.log1p(jnp.exp(-jnp.abs(neg_lam)))
        k_row = (0.5 * LRU_C) * softplus

        nq = LRU_WIDTH // LANES
        tiles_per_sub = SUB_LEN // SUBLANES
        blk_rows = T // GATE_BLOCKS
        for blk in range(GATE_BLOCKS):
            rows_blk = slice(blk * blk_rows, (blk + 1) * blk_rows)
            z = [_dot(xcb[rows_blk, q * half:(q + 1) * half], small_w_ref[q, :, 0:2 * half])
                 for q in range(2)]
            yield FFN_PIECES_AT[3][blk]
            za = jnp.concatenate([z[0][:, :half], z[1][:, :half]], axis=-1) + b_a_ref[...]
            zi = jnp.concatenate([z[0][:, half:], z[1][:, half:]], axis=-1) + b_i_ref[...]
            neg_log_a = jnp.tanh(0.5 * za) * k_row + k_row
            a = jnp.exp(-neg_log_a)
            one_minus_a2 = jnp.tanh(neg_log_a) * (a * a + 1.0)
            mult = jnp.where(one_minus_a2 > 0.0, one_minus_a2 * lax.rsqrt(one_minus_a2), 0.0)
            bterm = mult * (_sigmoid(zi) * xc[rows_blk, :])
            for q in range(nq):
                lanes = slice(q * LANES, (q + 1) * LANES)
                for t in range(blk_rows // SUBLANES):
                    j, m0 = divmod(blk * (blk_rows // SUBLANES) + t, tiles_per_sub)
                    dst = pl.ds(m0 * SUBLANES * SUBLANES + j, SUBLANES, stride=SUBLANES)
                    rows = slice(t * SUBLANES, (t + 1) * SUBLANES)
                    abuf[q, dst, :] = a[rows, lanes]
                    bbuf[q, dst, :] = bterm[rows, lanes]

        def tile(m):
            return slice(m * SUBLANES, (m + 1) * SUBLANES)

        yield FFN_PIECES_AT[4]
        prod = [jnp.ones((SUBLANES, LANES), jnp.float32) for _ in range(nq)]
        loc = [jnp.zeros((SUBLANES, LANES), jnp.float32) for _ in range(nq)]
        for m in range(SUB_LEN):
            for q in range(nq):
                am = abuf[q, tile(m), :]
                loc[q] = am * loc[q] + bbuf[q, tile(m), :]
                prod[q] = am * prod[q]
        row = lax.broadcasted_iota(jnp.int32, (SUBLANES, LANES), 0)
        h_in = []
        for q in range(nq):
            carry = hcarry[:, q * LANES:(q + 1) * LANES]
            start = jnp.zeros((SUBLANES, LANES), jnp.float32)
            for j in range(SUBLANES):
                start = jnp.where(row == j, carry, start)
                carry = prod[q][j:j + 1, :] * carry + loc[q][j:j + 1, :]
            h_in.append(start)
            hcarry[:, q * LANES:(q + 1) * LANES] = carry
        yield FFN_PIECES_AT[5]
        for m in range(SUB_LEN):
            for q in range(nq):
                h_in[q] = abuf[q, tile(m), :] * h_in[q] + bbuf[q, tile(m), :]
                hbuf[q, pl.ds(m, SUBLANES, stride=SUB_PITCH), :] = h_in[q]
        h_lru = jnp.concatenate(
            [jnp.concatenate([hbuf[q, j * SUB_PITCH:j * SUB_PITCH + SUB_LEN, :] for j in range(SUBLANES)], axis=0)
             for q in range(nq)], axis=-1)

        pbuf[POOL_TAIL:POOL_TAIL + T, :] = _dot(nb, w_in_ref[:, 0:POOL_WIDTH])
        yield FFN_PIECES_AT[1]
        n_groups = len(POOL_WINDOWS)
        level = pbuf[...]
        sums = []
        for g, w in enumerate(POOL_WINDOWS):
            assert w == 2 << g and w - 1 <= POOL_TAIL
            rest = level[:, (0 if g == 0 else POOL_GROUP):]
            rest = rest + pltpu.roll(rest, w // 2, axis=0)
            sums.append(rest[POOL_TAIL:, 0:POOL_GROUP])
            level = rest
        head_pos = (s * T + 1 + lax.broadcasted_iota(jnp.int32, (POOL_TAIL, POOL_GROUP), 0)).astype(jnp.float32)
        d_parts = []
        for g, w in enumerate(POOL_WINDOWS):
            ug = pbuf[POOL_TAIL:POOL_TAIL + T, g * POOL_GROUP:(g + 1) * POOL_GROUP]
            head = sums[g][0:POOL_TAIL] / jnp.minimum(head_pos, float(w))
            mean = jnp.concatenate([head, sums[g][POOL_TAIL:] * (1.0 / w)], axis=0)
            d_parts.append((mean - ug).astype(jnp.bfloat16))
        y_parts = []
        for p in range(n_groups // 2):
            d_pair = jnp.concatenate(d_parts[2 * p:2 * p + 2], axis=-1)
            y_parts.append(_dot(d_pair, small_w_ref[p, :, 2 * half:3 * half]))
        y_pool = jnp.concatenate(y_parts, axis=-1) * pool_scale_ref[...]
        mix_pool = _rms(y_pool, gn_pool_ref[...]).astype(jnp.bfloat16)

        yield FFN_PIECES_AT[6]
        y_lru = _gelu_tanh_times(u_gate, h_lru)
        mix_lru = _rms(y_lru, gn_lru_ref[...]).astype(jnp.bfloat16)

        mix = jnp.concatenate([mix_pool, mix_lru], axis=-1)
        h1_new = x_ref[...] + _dot(mix, w_out_ref[...])
        yield FFN_PIECES_AT[7]
        h1buf[slot] = h1_new
        n2buf[slot] = _rms(h1_new, ln2_ref[...]).astype(jnp.bfloat16)

        pbuf[0:POOL_TAIL, :] = pbuf[T:T + POOL_TAIL, :]
        lbuf[0:CONV_TAIL, :] = lbuf[T:T + CONV_TAIL, :]

    def make_ffn():
        acts = {}
        ffn_parts = []
        pending = _ffn_sequence()

        def ffn(n):
            for _ in range(n):
                op, c = pending.pop(0)
                if op == "u":
                    c0, c1 = _ff_chunk_cols()[c]
                    n2 = n2buf[prev]
                    gu = _dot(n2, wgu_ref[:, 2 * c0:2 * c1])
                    half_g, up = gu[:, :c1 - c0], gu[:, c1 - c0:]
                    acts[c] = (half_g * (jnp.tanh(half_g) + 1.0) * up).astype(jnp.bfloat16)
                else:
                    group = range(c, min(c + FF_DOWN_GROUP, N_FF_CHUNKS))
                    act = jnp.concatenate([acts.pop(k) for k in group], axis=-1)
                    ffn_parts.append(_dot(act, wd_ref[c * FF_CHUNK:min((group[-1] + 1) * FF_CHUNK, D_FF), :]))

        def finish():
            ffn(len(pending))
            ffn_out = ffn_parts[0]
            for part in ffn_parts[1:]:
                ffn_out = ffn_out + part
            o_ref[...] = _rms(h1buf[prev] + ffn_out, lnf_ref[...])

        return ffn, finish

    @pl.when(i == 0)
    def _():
        for _ in mixer():
            pass

    @pl.when(jnp.logical_and(i > 0, i < n_chunks))
    def _():
        ffn, finish = make_ffn()
        for n in mixer():
            ffn(n)
        finish()

    @pl.when(i == n_chunks)
    def _():
        _, finish = make_ffn()
        finish()


def kernel(x, ln1_g, w_in, pool_w, pool_scale, conv_w, conv_b, w_a, b_a, w_i, b_i, lam, gn_pool_g,
           gn_lru_g, w_out, ln2_g, w_ffn_gate, w_ffn_up, w_ffn_down, lnf_g):
    B, S, D = x.shape
    assert D == D_MODEL and S % SEQ_TILE == 0 and ln1_g.shape[0] == 1
    bf16 = jnp.bfloat16
    vmem = pl.BlockSpec(memory_space=pltpu.VMEM)
    hbm = pl.BlockSpec(memory_space=pl.ANY)
    nq = LRU_WIDTH // LANES
    scan_rows = SUBLANES * SUB_PITCH
    cps = S // SEQ_TILE
    n_chunks = B * cps

    def in_index(i):
        c = jnp.minimum(i, n_chunks - 1)
        return (c // cps, c % cps, 0)

    def out_index(i):
        c = jnp.maximum(i - 1, 0)
        return (c // cps, c % cps, 0)

    out = pl.pallas_call(
        functools.partial(_layer_kernel, chunks_per_seq=cps, n_chunks=n_chunks),
        grid=(n_chunks + 1,),
        in_specs=[pl.BlockSpec((None, SEQ_TILE, D), in_index)] + [
            hbm if k in HBM_WEIGHT_OPERANDS else vmem for k in range(1, 20)],
        out_specs=pl.BlockSpec((None, SEQ_TILE, D), out_index),
        out_shape=jax.ShapeDtypeStruct((B, S, D), jnp.float32),
        scratch_shapes=[
            pltpu.VMEM((POOL_TAIL + SEQ_TILE, POOL_WIDTH), jnp.float32),
            pltpu.VMEM((CONV_TAIL + SEQ_TILE, LRU_WIDTH), jnp.float32),
            pltpu.VMEM((nq, SEQ_TILE, LANES), jnp.float32),
            pltpu.VMEM((nq, SEQ_TILE, LANES), jnp.float32),
            pltpu.VMEM((nq, scan_rows, LANES), jnp.float32),
            pltpu.VMEM((1, LRU_WIDTH), jnp.float32),
            pltpu.VMEM((2, SEQ_TILE, D), jnp.float32),
            pltpu.VMEM((2, SEQ_TILE, D), jnp.bfloat16),
            pltpu.VMEM((D, IN_WIDTH), bf16),
            pltpu.VMEM((D, D), bf16),
            pltpu.VMEM((D, 2 * D_FF), bf16),
            pltpu.VMEM((D_FF, D), bf16),
            pltpu.VMEM((2, LRU_WIDTH // 2, LRU_WIDTH + LRU_WIDTH // 2), bf16),
            pltpu.VMEM((WEIGHT_STAGE_SLOTS, WEIGHT_STAGE_ROWS, D_FF), jnp.float32),
            pltpu.SemaphoreType.DMA((WEIGHT_STAGE_SLOTS,)),
        ],
        compiler_params=pltpu.CompilerParams(
            dimension_semantics=("arbitrary",),
            vmem_limit_bytes=VMEM_LIMIT_BYTES),
        name="hymba_layer",
    )(x, ln1_g, w_in[0], pool_w, pool_scale, conv_w, conv_b, w_a, b_a, w_i, b_i, lam, gn_pool_g, gn_lru_g,
      w_out[0], ln2_g, w_ffn_gate[0], w_ffn_up[0], w_ffn_down[0], lnf_g.reshape(1, D))
    return out
```

```python
import functools
import math

import jax
import jax.numpy as jnp
from jax import lax
from jax.experimental import pallas as pl
from jax.experimental.pallas import tpu as pltpu

D_MODEL = 1024
POOL_WINDOWS = (2, 4, 8, 16)
POOL_GROUP = 128
POOL_WIDTH = 512
LRU_WIDTH = 512
LRU_HEADS = 8
LRU_HEAD_DIM = 64
CONV_WIDTH = 4
LRU_C = 8.0
D_FF = 2816
IN_WIDTH = POOL_WIDTH + 2 * LRU_WIDTH
EPS = 1e-6

LANES = 128
SUBLANES = 8
SEQ_TILE = 512
POOL_TAIL = 16
CONV_TAIL = 8
SUB_LEN = SEQ_TILE // SUBLANES
SUB_PITCH = SUB_LEN + SUBLANES
FF_CHUNK = 512
N_FF_CHUNKS = -(-D_FF // FF_CHUNK)
FF_DOWN_GROUP = 3
GATE_BLOCKS = 1
FFN_PIECES_AT = (1, 2, 1, (1,), 1, 1, 0, 1)
HBM_WEIGHT_OPERANDS = (2, 14, 16, 17, 18)
WEIGHT_STAGE_ROWS = 128
WEIGHT_STAGE_SLOTS = 6
VMEM_LIMIT_BYTES = 58 * 1024 * 1024


def _ffn_sequence():
    ops = []
    for c in range(N_FF_CHUNKS):
        ops.append(("u", c))
        if c % FF_DOWN_GROUP == 0 and c > 0:
            ops.append(("d", c - FF_DOWN_GROUP))
    last = (N_FF_CHUNKS - 1) // FF_DOWN_GROUP * FF_DOWN_GROUP
    return ops + [("d", last)]


def _ff_chunk_cols():
    return tuple((c * FF_CHUNK, min((c + 1) * FF_CHUNK, D_FF)) for c in range(N_FF_CHUNKS))


def _rms(x, g):
    ms = jnp.mean(x * x, axis=-1, keepdims=True)
    return x * lax.rsqrt(ms + EPS) * g


def _sigmoid(x):
    return 0.5 * (jnp.tanh(0.5 * x) + 1.0)


def _gelu_tanh_times(x, h):
    c = math.sqrt(2.0 / math.pi)
    t = jnp.tanh(x * ((x * x) * (0.044715 * c) + c))
    return (h * x) * (0.5 * t + 0.5)


def _dot(a, b):
    return jnp.dot(a, b, preferred_element_type=jnp.float32)


def _load_weights_bf16(weights, stage, sems):
    chunks = [(src, dst, scale, col_map or ((0, src.shape[1], 0),), r * WEIGHT_STAGE_ROWS)
              for src, dst, scale, col_map in weights for r in range(src.shape[0] // WEIGHT_STAGE_ROWS)]

    def copy(k):
        src, _, _, _, row0 = chunks[k]
        slot = k % WEIGHT_STAGE_SLOTS
        return pltpu.make_async_copy(src.at[pl.ds(row0, WEIGHT_STAGE_ROWS), :],
                                     stage.at[slot, :, pl.ds(0, src.shape[1])], sems.at[slot])

    ahead = WEIGHT_STAGE_SLOTS - 1
    for k in range(min(ahead, len(chunks))):
        copy(k).start(priority=k % 2)
    for k, (_, dst, scale, col_map, row0) in enumerate(chunks):
        if k + ahead < len(chunks):
            copy(k + ahead).start(priority=(k + ahead) % 2)
        copy(k).wait()
        for c0, c1, d0 in col_map:
            w = stage[k % WEIGHT_STAGE_SLOTS, :, c0:c1]
            dst[row0:row0 + WEIGHT_STAGE_ROWS, d0:d0 + c1 - c0] = (
                w if scale is None else w * scale).astype(jnp.bfloat16)


def _build_small_weights(pool_w_ref, w_a_ref, w_i_ref, small_w_ref):
    half = LRU_WIDTH // 2
    heads = half // LRU_HEAD_DIM
    bf16 = jnp.bfloat16
    c_idx = lax.broadcasted_iota(jnp.int32, (LRU_HEAD_DIM, half), 0)
    j_idx = lax.broadcasted_iota(jnp.int32, (LRU_HEAD_DIM, half), 1)
    spread = (lax.rem(j_idx, LRU_HEAD_DIM) == c_idx).astype(bf16)
    col_head = j_idx // LRU_HEAD_DIM
    for q in range(2):
        for part, w_ref in enumerate((w_a_ref, w_i_ref)):
            for h in range(heads):
                rep = _dot(w_ref[0, heads * q + h].astype(bf16), spread)
                blk = jnp.where(col_head == h, rep, 0.0).astype(bf16)
                small_w_ref[q, h * LRU_HEAD_DIM:(h + 1) * LRU_HEAD_DIM, part * half:(part + 1) * half] = blk
        small_w_ref[q, :, 2 * half:3 * half] = jnp.zeros((half, half), bf16)
        for g in range(2):
            rows = slice(g * POOL_GROUP, (g + 1) * POOL_GROUP)
            cols = slice(2 * half + g * POOL_GROUP, 2 * half + (g + 1) * POOL_GROUP)
            small_w_ref[q, rows, cols] = pool_w_ref[0, 2 * q + g].astype(bf16)


def _layer_kernel(x_ref, ln1_ref, w_in_hbm, pool_w_ref, pool_scale_ref, conv_w_ref, conv_b_ref,
                  w_a_ref, b_a_ref, w_i_ref, b_i_ref, lam_ref, gn_pool_ref, gn_lru_ref, w_out_hbm,
                  ln2_ref, wg_hbm, wu_hbm, wd_hbm, lnf_ref, o_ref,
                  pbuf, lbuf, abuf, bbuf, hbuf, hcarry, h1buf, n2buf,
                  w_in_ref, w_out_ref, wgu_ref, wd_ref, small_w_ref, stage, stage_sems,
                  *, chunks_per_seq, n_chunks):
    i = pl.program_id(0)
    s = lax.rem(i, chunks_per_seq)
    T = SEQ_TILE
    half = LRU_WIDTH // 2

    @pl.when(i == 0)
    def _():
        g_map = tuple((c0, c1, 2 * c0) for c0, c1 in _ff_chunk_cols())
        u_map = tuple((c0, c1, c0 + c1) for c0, c1 in _ff_chunk_cols())
        _load_weights_bf16(((w_in_hbm, w_in_ref, None, None), (w_out_hbm, w_out_ref, None, None),
                            (wg_hbm, wgu_ref, 0.5, g_map), (wu_hbm, wgu_ref, None, u_map),
                            (wd_hbm, wd_ref, None, None)), stage, stage_sems)
        _build_small_weights(pool_w_ref, w_a_ref, w_i_ref, small_w_ref)

    slot = lax.rem(i, 2)
    prev = 1 - slot

    def mixer():
        @pl.when(s == 0)
        def _():
            pbuf[0:POOL_TAIL, :] = jnp.zeros((POOL_TAIL, POOL_WIDTH), jnp.float32)
            lbuf[0:CONV_TAIL, :] = jnp.zeros((CONV_TAIL, LRU_WIDTH), jnp.float32)
            hcarry[...] = jnp.zeros_like(hcarry)

        yield FFN_PIECES_AT[0]
        nb = _rms(x_ref[...], ln1_ref[...]).astype(jnp.bfloat16)

        lbuf[CONV_TAIL:CONV_TAIL + T, :] = _dot(nb, w_in_ref[:, POOL_WIDTH:POOL_WIDTH + LRU_WIDTH])
        u_gate = _dot(nb, w_in_ref[:, POOL_WIDTH + LRU_WIDTH:])

        yield FFN_PIECES_AT[2]
        assert CONV_WIDTH == 4
        u_ext = lbuf[...]
        u_prev = pltpu.roll(u_ext, 1, axis=0)
        tap = lambda k: conv_w_ref[0, k:k + 1, :]
        older = tap(1) * u_ext + tap(0) * u_prev
        xc = (conv_b_ref[...] + tap(3) * u_ext + tap(2) * u_prev + pltpu.roll(older, 2, axis=0))[CONV_TAIL:]
        xcb = xc.astype(jnp.bfloat16)
        neg_lam = -lam_ref[...]
        softplus = jnp.maximum(neg_lam, 0.0) + jnp.log1p(jnp.exp(-jnp.abs(neg_lam)))
        k_row = (0.5 * LRU_C) * softplus

        nq = LRU_WIDTH // LANES
        tiles_per_sub = SUB_LEN // SUBLANES
        blk_rows = T // GATE_BLOCKS
        for blk in range(GATE_BLOCKS):
            rows_blk = slice(blk * blk_rows, (blk + 1) * blk_rows)
            z = [_dot(xcb[rows_blk, q * half:(q + 1) * half], small_w_ref[q, :, 0:2 * half])
                 for q in range(2)]
            yield FFN_PIECES_AT[3][blk]
            za = jnp.concatenate([z[0][:, :half], z[1][:, :half]], axis=-1) + b_a_ref[...]
            zi = jnp.concatenate([z[0][:, half:], z[1][:, half:]], axis=-1) + b_i_ref[...]
            neg_log_a = jnp.tanh(0.5 * za) * k_row + k_row
            a = jnp.exp(-neg_log_a)
            one_minus_a2 = jnp.tanh(neg_log_a) * (a * a + 1.0)
            mult = jnp.where(one_minus_a2 > 0.0, one_minus_a2 * lax.rsqrt(one_minus_a2), 0.0)
            bterm = mult * (_sigmoid(zi) * xc[rows_blk, :])
            for q in range(nq):
                lanes = slice(q * LANES, (q + 1) * LANES)
                for t in range(blk_rows // SUBLANES):
                    j, m0 = divmod(blk * (blk_rows // SUBLANES) + t, tiles_per_sub)
                    dst = pl.ds(m0 * SUBLANES * SUBLANES + j, SUBLANES, stride=SUBLANES)
                    rows = slice(t * SUBLANES, (t + 1) * SUBLANES)
                    abuf[q, dst, :] = a[rows, lanes]
                    bbuf[q, dst, :] = bterm[rows, lanes]

        def tile(m):
            return slice(m * SUBLANES, (m + 1) * SUBLANES)

        yield FFN_PIECES_AT[4]
        prod = [jnp.ones((SUBLANES, LANES), jnp.float32) for _ in range(nq)]
        loc = [jnp.zeros((SUBLANES, LANES), jnp.float32) for _ in range(nq)]
        for m in range(SUB_LEN):
            for q in range(nq):
                am = abuf[q, tile(m), :]
                loc[q] = am * loc[q] + bbuf[q, tile(m), :]
                prod[q] = am * prod[q]
        row = lax.broadcasted_iota(jnp.int32, (SUBLANES, LANES), 0)
        h_in = []
        for q in range(nq):
            carry = hcarry[:, q * LANES:(q + 1) * LANES]
            start = jnp.zeros((SUBLANES, LANES), jnp.float32)
            for j in range(SUBLANES):
                start = jnp.where(row == j, carry, start)
                carry = prod[q][j:j + 1, :] * carry + loc[q][j:j + 1, :]
            h_in.append(start)
            hcarry[:, q * LANES:(q + 1) * LANES] = carry
        yield FFN_PIECES_AT[5]
        for m in range(SUB_LEN):
            for q in range(nq):
                h_in[q] = abuf[q, tile(m), :] * h_in[q] + bbuf[q, tile(m), :]
                hbuf[q, pl.ds(m, SUBLANES, stride=SUB_PITCH), :] = h_in[q]
        h_lru = jnp.concatenate(
            [jnp.concatenate([hbuf[q, j * SUB_PITCH:j * SUB_PITCH + SUB_LEN, :] for j in range(SUBLANES)], axis=0)
             for q in range(nq)], axis=-1)

        pbuf[POOL_TAIL:POOL_TAIL + T, :] = _dot(nb, w_in_ref[:, 0:POOL_WIDTH])
        yield FFN_PIECES_AT[1]
        n_groups = len(POOL_WINDOWS)
        level = pbuf[...]
        sums = []
        for g, w in enumerate(POOL_WINDOWS):
            assert w == 2 << g and w - 1 <= POOL_TAIL
            rest = level[:, (0 if g == 0 else POOL_GROUP):]
            rest = rest + pltpu.roll(rest, w // 2, axis=0)
            sums.append(rest[POOL_TAIL:, 0:POOL_GROUP])
            level = rest
        head_pos = (s * T + 1 + lax.broadcasted_iota(jnp.int32, (POOL_TAIL, POOL_GROUP), 0)).astype(jnp.float32)
        d_parts = []
        for g, w in enumerate(POOL_WINDOWS):
            ug = pbuf[POOL_TAIL:POOL_TAIL + T, g * POOL_GROUP:(g + 1) * POOL_GROUP]
            head = sums[g][0:POOL_TAIL] / jnp.minimum(head_pos, float(w))
            mean = jnp.concatenate([head, sums[g][POOL_TAIL:] * (1.0 / w)], axis=0)
            d_parts.append((mean - ug).astype(jnp.bfloat16))
        y_parts = []
        for p in range(n_groups // 2):
            d_pair = jnp.concatenate(d_parts[2 * p:2 * p + 2], axis=-1)
            y_parts.append(_dot(d_pair, small_w_ref[p, :, 2 * half:3 * half]))
        y_pool = jnp.concatenate(y_parts, axis=-1) * pool_scale_ref[...]
        mix_pool = _rms(y_pool, gn_pool_ref[...]).astype(jnp.bfloat16)

        yield FFN_PIECES_AT[6]
        y_lru = _gelu_tanh_times(u_gate, h_lru)
        mix_lru = _rms(y_lru, gn_lru_ref[...]).astype(jnp.bfloat16)

        mix = jnp.concatenate([mix_pool, mix_lru], axis=-1)
        h1_new = x_ref[...] + _dot(mix, w_out_ref[...])
        yield FFN_PIECES_AT[7]
        h1buf[slot] = h1_new
        n2buf[slot] = _rms(h1_new, ln2_ref[...]).astype(jnp.bfloat16)

        pbuf[0:POOL_TAIL, :] = pbuf[T:T + POOL_TAIL, :]
        lbuf[0:CONV_TAIL, :] = lbuf[T:T + CONV_TAIL, :]

    def make_ffn():
        acts = {}
        ffn_parts = []
        pending = _ffn_sequence()

        def ffn(n):
            for _ in range(n):
                op, c = pending.pop(0)
                if op == "u":
                    c0, c1 = _ff_chunk_cols()[c]
                    n2 = n2buf[prev]
                    gu = _dot(n2, wgu_ref[:, 2 * c0:2 * c1])
                    half_g, up = gu[:, :c1 - c0], gu[:, c1 - c0:]
                    acts[c] = (half_g * (jnp.tanh(half_g) + 1.0) * up).astype(jnp.bfloat16)
                else:
                    group = range(c, min(c + FF_DOWN_GROUP, N_FF_CHUNKS))
                    act = jnp.concatenate([acts.pop(k) for k in group], axis=-1)
                    ffn_parts.append(_dot(act, wd_ref[c * FF_CHUNK:min((group[-1] + 1) * FF_CHUNK, D_FF), :]))

        def finish():
            ffn(len(pending))
            ffn_out = ffn_parts[0]
            for part in ffn_parts[1:]:
                ffn_out = ffn_out + part
            o_ref[...] = _rms(h1buf[prev] + ffn_out, lnf_ref[...])

        return ffn, finish

    @pl.when(i == 0)
    def _():
        for _ in mixer():
            pass

    @pl.when(jnp.logical_and(i > 0, i < n_chunks))
    def _():
        ffn, finish = make_ffn()
        for n in mixer():
            ffn(n)
        finish()

    @pl.when(i == n_chunks)
    def _():
        _, finish = make_ffn()
        finish()


def kernel(x, ln1_g, w_in, pool_w, pool_scale, conv_w, conv_b, w_a, b_a, w_i, b_i, lam, gn_pool_g,
           gn_lru_g, w_out, ln2_g, w_ffn_gate, w_ffn_up, w_ffn_down, lnf_g):
    B, S, D = x.shape
    assert D == D_MODEL and S % SEQ_TILE == 0 and ln1_g.shape[0] == 1
    bf16 = jnp.bfloat16
    vmem = pl.BlockSpec(memory_space=pltpu.VMEM)
    hbm = pl.BlockSpec(memory_space=pl.ANY)
    nq = LRU_WIDTH // LANES
    scan_rows = SUBLANES * SUB_PITCH
    cps = S // SEQ_TILE
    n_chunks = B * cps

    def in_index(i):
        c = jnp.minimum(i, n_chunks - 1)
        return (c // cps, c % cps, 0)

    def out_index(i):
        c = jnp.maximum(i - 1, 0)
        return (c // cps, c % cps, 0)

    out = pl.pallas_call(
        functools.partial(_layer_kernel, chunks_per_seq=cps, n_chunks=n_chunks),
        grid=(n_chunks + 1,),
        in_specs=[pl.BlockSpec((None, SEQ_TILE, D), in_index)] + [
            hbm if k in HBM_WEIGHT_OPERANDS else vmem for k in range(1, 20)],
        out_specs=pl.BlockSpec((None, SEQ_TILE, D), out_index),
        out_shape=jax.ShapeDtypeStruct((B, S, D), jnp.float32),
        scratch_shapes=[
            pltpu.VMEM((POOL_TAIL + SEQ_TILE, POOL_WIDTH), jnp.float32),
            pltpu.VMEM((CONV_TAIL + SEQ_TILE, LRU_WIDTH), jnp.float32),
            pltpu.VMEM((nq, SEQ_TILE, LANES), jnp.float32),
            pltpu.VMEM((nq, SEQ_TILE, LANES), jnp.float32),
            pltpu.VMEM((nq, scan_rows, LANES), jnp.float32),
            pltpu.VMEM((1, LRU_WIDTH), jnp.float32),
            pltpu.VMEM((2, SEQ_TILE, D), jnp.float32),
            pltpu.VMEM((2, SEQ_TILE, D), jnp.bfloat16),
            pltpu.VMEM((D, IN_WIDTH), bf16),
            pltpu.VMEM((D, D), bf16),
            pltpu.VMEM((D, 2 * D_FF), bf16),
            pltpu.VMEM((D_FF, D), bf16),
            pltpu.VMEM((2, LRU_WIDTH // 2, LRU_WIDTH + LRU_WIDTH // 2), bf16),
            pltpu.VMEM((WEIGHT_STAGE_SLOTS, WEIGHT_STAGE_ROWS, D_FF), jnp.float32),
            pltpu.SemaphoreType.DMA((WEIGHT_STAGE_SLOTS,)),
        ],
        compiler_params=pltpu.CompilerParams(
            dimension_semantics=("arbitrary",),
            vmem_limit_bytes=VMEM_LIMIT_BYTES),
        name="hymba_layer",
    )(x, ln1_g, w_in[0], pool_w, pool_scale, conv_w, conv_b, w_a, b_a, w_i, b_i, lam, gn_pool_g, gn_lru_g,
      w_out[0], ln2_g, w_ffn_gate[0], w_ffn_up[0], w_ffn_down[0], lnf_g.reshape(1, D))
    return out
```

```python
import functools
import math

import jax
import jax.numpy as jnp
from jax import lax
from jax.experimental import pallas as pl
from jax.experimental.pallas import tpu as pltpu

D_MODEL = 1024
POOL_WINDOWS = (2, 4, 8, 16)
POOL_GROUP = 128
POOL_WIDTH = 512
LRU_WIDTH = 512
LRU_HEADS = 8
LRU_HEAD_DIM = 64
CONV_WIDTH = 4
LRU_C = 8.0
D_FF = 2816
IN_WIDTH = POOL_WIDTH + 2 * LRU_WIDTH
EPS = 1e-6

LANES = 128
SUBLANES = 8
SEQ_TILE = 512
POOL_TAIL = 16
CONV_TAIL = 8
SUB_LEN = SEQ_TILE // SUBLANES
SUB_PITCH = SUB_LEN + SUBLANES
FF_CHUNK = 512
N_FF_CHUNKS = -(-D_FF // FF_CHUNK)
FF_DOWN_GROUP = 3
GATE_BLOCKS = 1
FFN_PIECES_AT = (2, 1, 1, (1,), 1, 1, 1, 0)
HBM_WEIGHT_OPERANDS = (2, 14, 16, 17, 18)
WEIGHT_STAGE_ROWS = 128
WEIGHT_STAGE_SLOTS = 6
VMEM_LIMIT_BYTES = 58 * 1024 * 1024


def _ffn_sequence():
    ops = []
    for c in range(N_FF_CHUNKS):
        ops.append(("u", c))
        if c % FF_DOWN_GROUP == 0 and c > 0:
            ops.append(("d", c - FF_DOWN_GROUP))
    last = (N_FF_CHUNKS - 1) // FF_DOWN_GROUP * FF_DOWN_GROUP
    return ops + [("d", last)]


def _ff_chunk_cols():
    return tuple((c * FF_CHUNK, min((c + 1) * FF_CHUNK, D_FF)) for c in range(N_FF_CHUNKS))


def _rms(x, g):
    ms = jnp.mean(x * x, axis=-1, keepdims=True)
    return x * lax.rsqrt(ms + EPS) * g


def _sigmoid(x):
    return 0.5 * (jnp.tanh(0.5 * x) + 1.0)


def _gelu_tanh_times(x, h):
    c = math.sqrt(2.0 / math.pi)
    t = jnp.tanh(x * ((x * x) * (0.044715 * c) + c))
    return (h * x) * (0.5 * t + 0.5)


def _dot(a, b):
    return jnp.dot(a, b, preferred_element_type=jnp.float32)


def _load_weights_bf16(weights, stage, sems):
    chunks = [(src, dst, scale, col_map or ((0, src.shape[1], 0),), r * WEIGHT_STAGE_ROWS)
              for src, dst, scale, col_map in weights for r in range(src.shape[0] // WEIGHT_STAGE_ROWS)]

    def copy(k):
        src, _, _, _, row0 = chunks[k]
        slot = k % WEIGHT_STAGE_SLOTS
        return pltpu.make_async_copy(src.at[pl.ds(row0, WEIGHT_STAGE_ROWS), :],
                                     stage.at[slot, :, pl.ds(0, src.shape[1])], sems.at[slot])

    ahead = WEIGHT_STAGE_SLOTS - 1
    for k in range(min(ahead, len(chunks))):
        copy(k).start(priority=k % 2)
    for k, (_, dst, scale, col_map, row0) in enumerate(chunks):
        if k + ahead < len(chunks):
            copy(k + ahead).start(priority=(k + ahead) % 2)
        copy(k).wait()
        for c0, c1, d0 in col_map:
            w = stage[k % WEIGHT_STAGE_SLOTS, :, c0:c1]
            dst[row0:row0 + WEIGHT_STAGE_ROWS, d0:d0 + c1 - c0] = (
                w if scale is None else w * scale).astype(jnp.bfloat16)


def _build_small_weights(pool_w_ref, w_a_ref, w_i_ref, small_w_ref):
    half = LRU_WIDTH // 2
    heads = half // LRU_HEAD_DIM
    bf16 = jnp.bfloat16
    c_idx = lax.broadcasted_iota(jnp.int32, (LRU_HEAD_DIM, half), 0)
    j_idx = lax.broadcasted_iota(jnp.int32, (LRU_HEAD_DIM, half), 1)
    spread = (lax.rem(j_idx, LRU_HEAD_DIM) == c_idx).astype(bf16)
    col_head = j_idx // LRU_HEAD_DIM
    for q in range(2):
        for part, w_ref in enumerate((w_a_ref, w_i_ref)):
            for h in range(heads):
                rep = _dot(w_ref[0, heads * q + h].astype(bf16), spread)
                blk = jnp.where(col_head == h, rep, 0.0).astype(bf16)
                small_w_ref[q, h * LRU_HEAD_DIM:(h + 1) * LRU_HEAD_DIM, part * half:(part + 1) * half] = blk
        small_w_ref[q, :, 2 * half:3 * half] = jnp.zeros((half, half), bf16)
        for g in range(2):
            rows = slice(g * POOL_GROUP, (g + 1) * POOL_GROUP)
            cols = slice(2 * half + g * POOL_GROUP, 2 * half + (g + 1) * POOL_GROUP)
            small_w_ref[q, rows, cols] = pool_w_ref[0, 2 * q + g].astype(bf16)


def _layer_kernel(x_ref, ln1_ref, w_in_hbm, pool_w_ref, pool_scale_ref, conv_w_ref, conv_b_ref,
                  w_a_ref, b_a_ref, w_i_ref, b_i_ref, lam_ref, gn_pool_ref, gn_lru_ref, w_out_hbm,
                  ln2_ref, wg_hbm, wu_hbm, wd_hbm, lnf_ref, o_ref,
                  pbuf, lbuf, abuf, bbuf, hbuf, hcarry, h1buf, n2buf,
                  w_in_ref, w_out_ref, wgu_ref, wd_ref, small_w_ref, stage, stage_sems,
                  *, chunks_per_seq, n_chunks):
    i = pl.program_id(0)
    s = lax.rem(i, chunks_per_seq)
    T = SEQ_TILE
    half = LRU_WIDTH // 2

    @pl.when(i == 0)
    def _():
        g_map = tuple((c0, c1, 2 * c0) for c0, c1 in _ff_chunk_cols())
        u_map = tuple((c0, c1, c0 + c1) for c0, c1 in _ff_chunk_cols())
        _load_weights_bf16(((w_in_hbm, w_in_ref, None, None), (w_out_hbm, w_out_ref, None, None),
                            (wg_hbm, wgu_ref, 0.5, g_map), (wu_hbm, wgu_ref, None, u_map),
                            (wd_hbm, wd_ref, None, None)), stage, stage_sems)
        _build_small_weights(pool_w_ref, w_a_ref, w_i_ref, small_w_ref)

    slot = lax.rem(i, 2)
    prev = 1 - slot

    def mixer():
        @pl.when(s == 0)
        def _():
            pbuf[0:POOL_TAIL, :] = jnp.zeros((POOL_TAIL, POOL_WIDTH), jnp.float32)
            lbuf[0:CONV_TAIL, :] = jnp.zeros((CONV_TAIL, LRU_WIDTH), jnp.float32)
            hcarry[...] = jnp.zeros_like(hcarry)

        yield FFN_PIECES_AT[0]
        nb = _rms(x_ref[...], ln1_ref[...]).astype(jnp.bfloat16)

        lbuf[CONV_TAIL:CONV_TAIL + T, :] = _dot(nb, w_in_ref[:, POOL_WIDTH:POOL_WIDTH + LRU_WIDTH])
        u_gate = _dot(nb, w_in_ref[:, POOL_WIDTH + LRU_WIDTH:])

        yield FFN_PIECES_AT[2]
        assert CONV_WIDTH == 4
        u_ext = lbuf[...]
        u_prev = pltpu.roll(u_ext, 1, axis=0)
        tap = lambda k: conv_w_ref[0, k:k + 1, :]
        older = tap(1) * u_ext + tap(0) * u_prev
        xc = (conv_b_ref[...] + tap(3) * u_ext + tap(2) * u_prev + pltpu.roll(older, 2, axis=0))[CONV_TAIL:]
        xcb = xc.astype(jnp.bfloat16)
        neg_lam = -lam_ref[...]
        softplus = jnp.maximum(neg_lam, 0.0) + jnp.log1p(jnp.exp(-jnp.abs(neg_lam)))
        k_row = (0.5 * LRU_C) * softplus

        nq = LRU_WIDTH // LANES
        tiles_per_sub = SUB_LEN // SUBLANES
        blk_rows = T // GATE_BLOCKS
        for blk in range(GATE_BLOCKS):
            rows_blk = slice(blk * blk_rows, (blk + 1) * blk_rows)
            z = [_dot(xcb[rows_blk, q * half:(q + 1) * half], small_w_ref[q, :, 0:2 * half])
                 for q in range(2)]
            yield FFN_PIECES_AT[3][blk]
            za = jnp.concatenate([z[0][:, :half], z[1][:, :half]], axis=-1) + b_a_ref[...]
            zi = jnp.concatenate([z[0][:, half:], z[1][:, half:]], axis=-1) + b_i_ref[...]
            neg_log_a = jnp.tanh(0.5 * za) * k_row + k_row
            a = jnp.exp(-neg_log_a)
            one_minus_a2 = jnp.tanh(neg_log_a) * (a * a + 1.0)
            mult = jnp.where(one_minus_a2 > 0.0, one_minus_a2 * lax.rsqrt(one_minus_a2), 0.0)
            bterm = mult * (_sigmoid(zi) * xc[rows_blk, :])
            for q in range(nq):
                lanes = slice(q * LANES, (q + 1) * LANES)
                for t in range(blk_rows // SUBLANES):
                    j, m0 = divmod(blk * (blk_rows // SUBLANES) + t, tiles_per_sub)
                    dst = pl.ds(m0 * SUBLANES * SUBLANES + j, SUBLANES, stride=SUBLANES)
                    rows = slice(t * SUBLANES, (t + 1) * SUBLANES)
                    abuf[q, dst, :] = a[rows, lanes]
                    bbuf[q, dst, :] = bterm[rows, lanes]

        def tile(m):
            return slice(m * SUBLANES, (m + 1) * SUBLANES)

        yield FFN_PIECES_AT[4]
        prod = [jnp.ones((SUBLANES, LANES), jnp.float32) for _ in range(nq)]
        loc = [jnp.zeros((SUBLANES, LANES), jnp.float32) for _ in range(nq)]
        for m in range(SUB_LEN):
            for q in range(nq):
                am = abuf[q, tile(m), :]
                loc[q] = am * loc[q] + bbuf[q, tile(m), :]
                prod[q] = am * prod[q]
        row = lax.broadcasted_iota(jnp.int32, (SUBLANES, LANES), 0)
        h_in = []
        for q in range(nq):
            carry = hcarry[:, q * LANES:(q + 1) * LANES]
            start = jnp.zeros((SUBLANES, LANES), jnp.float32)
            for j in range(SUBLANES):
                start = jnp.where(row == j, carry, start)
                carry = prod[q][j:j + 1, :] * carry + loc[q][j:j + 1, :]
            h_in.append(start)
            hcarry[:, q * LANES:(q + 1) * LANES] = carry
        yield FFN_PIECES_AT[5]
        for m in range(SUB_LEN):
            for q in range(nq):
                h_in[q] = abuf[q, tile(m), :] * h_in[q] + bbuf[q, tile(m), :]
                hbuf[q, pl.ds(m, SUBLANES, stride=SUB_PITCH), :] = h_in[q]
        h_lru = jnp.concatenate(
            [jnp.concatenate([hbuf[q, j * SUB_PITCH:j * SUB_PITCH + SUB_LEN, :] for j in range(SUBLANES)], axis=0)
             for q in range(nq)], axis=-1)

        pbuf[POOL_TAIL:POOL_TAIL + T, :] = _dot(nb, w_in_ref[:, 0:POOL_WIDTH])
        yield FFN_PIECES_AT[1]
        n_groups = len(POOL_WINDOWS)
        level = pbuf[...]
        sums = []
        for g, w in enumerate(POOL_WINDOWS):
            assert w == 2 << g and w - 1 <= POOL_TAIL
            rest = level[:, (0 if g == 0 else POOL_GROUP):]
            rest = rest + pltpu.roll(rest, w // 2, axis=0)
            sums.append(rest[POOL_TAIL:, 0:POOL_GROUP])
            level = rest
        head_pos = (s * T + 1 + lax.broadcasted_iota(jnp.int32, (POOL_TAIL, POOL_GROUP), 0)).astype(jnp.float32)
        d_parts = []
        for g, w in enumerate(POOL_WINDOWS):
            ug = pbuf[POOL_TAIL:POOL_TAIL + T, g * POOL_GROUP:(g + 1) * POOL_GROUP]
            head = sums[g][0:POOL_TAIL] / jnp.minimum(head_pos, float(w))
            mean = jnp.concatenate([head, sums[g][POOL_TAIL:] * (1.0 / w)], axis=0)
            d_parts.append((mean - ug).astype(jnp.bfloat16))
        y_parts = []
        for p in range(n_groups // 2):
            d_pair = jnp.concatenate(d_parts[2 * p:2 * p + 2], axis=-1)
            y_parts.append(_dot(d_pair, small_w_ref[p, :, 2 * half:3 * half]))
        y_pool = jnp.concatenate(y_parts, axis=-1) * pool_scale_ref[...]
        mix_pool = _rms(y_pool, gn_pool_ref[...]).astype(jnp.bfloat16)

        yield FFN_PIECES_AT[6]
        y_lru = _gelu_tanh_times(u_gate, h_lru)
        mix_lru = _rms(y_lru, gn_lru_ref[...]).astype(jnp.bfloat16)

        mix = jnp.concatenate([mix_pool, mix_lru], axis=-1)
        h1_new = x_ref[...] + _dot(mix, w_out_ref[...])
        yield FFN_PIECES_AT[7]
        h1buf[slot] = h1_new
        n2buf[slot] = _rms(h1_new, ln2_ref[...]).astype(jnp.bfloat16)

        pbuf[0:POOL_TAIL, :] = pbuf[T:T + POOL_TAIL, :]
        lbuf[0:CONV_TAIL, :] = lbuf[T:T + CONV_TAIL, :]

    def make_ffn():
        acts = {}
        ffn_parts = []
        pending = _ffn_sequence()

        def ffn(n):
            for _ in range(n):
                op, c = pending.pop(0)
                if op == "u":
                    c0, c1 = _ff_chunk_cols()[c]
                    n2 = n2buf[prev]
                    gu = _dot(n2, wgu_ref[:, 2 * c0:2 * c1])
                    half_g, up = gu[:, :c1 - c0], gu[:, c1 - c0:]
                    acts[c] = (half_g * (jnp.tanh(half_g) + 1.0) * up).astype(jnp.bfloat16)
                else:
                    group = range(c, min(c + FF_DOWN_GROUP, N_FF_CHUNKS))
                    act = jnp.concatenate([acts.pop(k) for k in group], axis=-1)
                    ffn_parts.append(_dot(act, wd_ref[c * FF_CHUNK:min((group[-1] + 1) * FF_CHUNK, D_FF), :]))

        def finish():
            ffn(len(pending))
            ffn_out = ffn_parts[0]
            for part in ffn_parts[1:]:
                ffn_out = ffn_out + part
            o_ref[...] = _rms(h1buf[prev] + ffn_out, lnf_ref[...])

        return ffn, finish

    @pl.when(i == 0)
    def _():
        for _ in mixer():
            pass

    @pl.when(jnp.logical_and(i > 0, i < n_chunks))
    def _():
        ffn, finish = make_ffn()
        for n in mixer():
            ffn(n)
        finish()

    @pl.when(i == n_chunks)
    def _():
        _, finish = make_ffn()
        finish()


def kernel(x, ln1_g, w_in, pool_w, pool_scale, conv_w, conv_b, w_a, b_a, w_i, b_i, lam, gn_pool_g,
           gn_lru_g, w_out, ln2_g, w_ffn_gate, w_ffn_up, w_ffn_down, lnf_g):
    B, S, D = x.shape
    assert D == D_MODEL and S % SEQ_TILE == 0 and ln1_g.shape[0] == 1
    bf16 = jnp.bfloat16
    vmem = pl.BlockSpec(memory_space=pltpu.VMEM)
    hbm = pl.BlockSpec(memory_space=pl.ANY)
    nq = LRU_WIDTH // LANES
    scan_rows = SUBLANES * SUB_PITCH
    cps = S // SEQ_TILE
    n_chunks = B * cps

    def in_index(i):
        c = jnp.minimum(i, n_chunks - 1)
        return (c // cps, c % cps, 0)

    def out_index(i):
        c = jnp.maximum(i - 1, 0)
        return (c // cps, c % cps, 0)

    out = pl.pallas_call(
        functools.partial(_layer_kernel, chunks_per_seq=cps, n_chunks=n_chunks),
        grid=(n_chunks + 1,),
        in_specs=[pl.BlockSpec((None, SEQ_TILE, D), in_index)] + [
            hbm if k in HBM_WEIGHT_OPERANDS else vmem for k in range(1, 20)],
        out_specs=pl.BlockSpec((None, SEQ_TILE, D), out_index),
        out_shape=jax.ShapeDtypeStruct((B, S, D), jnp.float32),
        scratch_shapes=[
            pltpu.VMEM((POOL_TAIL + SEQ_TILE, POOL_WIDTH), jnp.float32),
            pltpu.VMEM((CONV_TAIL + SEQ_TILE, LRU_WIDTH), jnp.float32),
            pltpu.VMEM((nq, SEQ_TILE, LANES), jnp.float32),
            pltpu.VMEM((nq, SEQ_TILE, LANES), jnp.float32),
            pltpu.VMEM((nq, scan_rows, LANES), jnp.float32),
            pltpu.VMEM((1, LRU_WIDTH), jnp.float32),
            pltpu.VMEM((2, SEQ_TILE, D), jnp.float32),
            pltpu.VMEM((2, SEQ_TILE, D), jnp.bfloat16),
            pltpu.VMEM((D, IN_WIDTH), bf16),
            pltpu.VMEM((D, D), bf16),
            pltpu.VMEM((D, 2 * D_FF), bf16),
            pltpu.VMEM((D_FF, D), bf16),
            pltpu.VMEM((2, LRU_WIDTH // 2, LRU_WIDTH + LRU_WIDTH // 2), bf16),
            pltpu.VMEM((WEIGHT_STAGE_SLOTS, WEIGHT_STAGE_ROWS, D_FF), jnp.float32),
            pltpu.SemaphoreType.DMA((WEIGHT_STAGE_SLOTS,)),
        ],
        compiler_params=pltpu.CompilerParams(
            dimension_semantics=("arbitrary",),
            vmem_limit_bytes=VMEM_LIMIT_BYTES),
        name="hymba_layer",
    )(x, ln1_g, w_in[0], pool_w, pool_scale, conv_w, conv_b, w_a, b_a, w_i, b_i, lam, gn_pool_g, gn_lru_g,
      w_out[0], ln2_g, w_ffn_gate[0], w_ffn_up[0], w_ffn_down[0], lnf_g.reshape(1, D))
    return out
```

```python
import functools
import math

import jax
import jax.numpy as jnp
from jax import lax
from jax.experimental import pallas as pl
from jax.experimental.pallas import tpu as pltpu

D_MODEL = 1024
POOL_WINDOWS = (2, 4, 8, 16)
POOL_GROUP = 128
POOL_WIDTH = 512
LRU_WIDTH = 512
LRU_HEADS = 8
LRU_HEAD_DIM = 64
CONV_WIDTH = 4
LRU_C = 8.0
D_FF = 2816
IN_WIDTH = POOL_WIDTH + 2 * LRU_WIDTH
EPS = 1e-6

LANES = 128
SUBLANES = 8
SEQ_TILE = 512
POOL_TAIL = 16
CONV_TAIL = 8
SUB_LEN = SEQ_TILE // SUBLANES
SUB_PITCH = SUB_LEN + SUBLANES
FF_CHUNK = 512
N_FF_CHUNKS = -(-D_FF // FF_CHUNK)
FF_DOWN_GROUP = 3
GATE_BLOCKS = 1
FFN_PIECES_AT = (1, 1, 2, (0,), 1, 1, 1, 1)
HBM_WEIGHT_OPERANDS = (2, 14, 16, 17, 18)
WEIGHT_STAGE_ROWS = 128
WEIGHT_STAGE_SLOTS = 6
VMEM_LIMIT_BYTES = 58 * 1024 * 1024


def _ffn_sequence():
    ops = []
    for c in range(N_FF_CHUNKS):
        ops.append(("u", c))
        if c % FF_DOWN_GROUP == 0 and c > 0:
            ops.append(("d", c - FF_DOWN_GROUP))
    last = (N_FF_CHUNKS - 1) // FF_DOWN_GROUP * FF_DOWN_GROUP
    return ops + [("d", last)]


def _ff_chunk_cols():
    return tuple((c * FF_CHUNK, min((c + 1) * FF_CHUNK, D_FF)) for c in range(N_FF_CHUNKS))


def _rms(x, g):
    ms = jnp.mean(x * x, axis=-1, keepdims=True)
    return x * lax.rsqrt(ms + EPS) * g


def _sigmoid(x):
    return 0.5 * (jnp.tanh(0.5 * x) + 1.0)


def _gelu_tanh_times(x, h):
    c = math.sqrt(2.0 / math.pi)
    t = jnp.tanh(x * ((x * x) * (0.044715 * c) + c))
    return (h * x) * (0.5 * t + 0.5)


def _dot(a, b):
    return jnp.dot(a, b, preferred_element_type=jnp.float32)


def _load_weights_bf16(weights, stage, sems):
    chunks = [(src, dst, scale, col_map or ((0, src.shape[1], 0),), r * WEIGHT_STAGE_ROWS)
              for src, dst, scale, col_map in weights for r in range(src.shape[0] // WEIGHT_STAGE_ROWS)]

    def copy(k):
        src, _, _, _, row0 = chunks[k]
        slot = k % WEIGHT_STAGE_SLOTS
        return pltpu.make_async_copy(src.at[pl.ds(row0, WEIGHT_STAGE_ROWS), :],
                                     stage.at[slot, :, pl.ds(0, src.shape[1])], sems.at[slot])

    ahead = WEIGHT_STAGE_SLOTS - 1
    for k in range(min(ahead, len(chunks))):
        copy(k).start(priority=k % 2)
    for k, (_, dst, scale, col_map, row0) in enumerate(chunks):
        if k + ahead < len(chunks):
            copy(k + ahead).start(priority=(k + ahead) % 2)
        copy(k).wait()
        for c0, c1, d0 in col_map:
            w = stage[k % WEIGHT_STAGE_SLOTS, :, c0:c1]
            dst[row0:row0 + WEIGHT_STAGE_ROWS, d0:d0 + c1 - c0] = (
                w if scale is None else w * scale).astype(jnp.bfloat16)


def _build_small_weights(pool_w_ref, w_a_ref, w_i_ref, small_w_ref):
    half = LRU_WIDTH // 2
    heads = half // LRU_HEAD_DIM
    bf16 = jnp.bfloat16
    c_idx = lax.broadcasted_iota(jnp.int32, (LRU_HEAD_DIM, half), 0)
    j_idx = lax.broadcasted_iota(jnp.int32, (LRU_HEAD_DIM, half), 1)
    spread = (lax.rem(j_idx, LRU_HEAD_DIM) == c_idx).astype(bf16)
    col_head = j_idx // LRU_HEAD_DIM
    for q in range(2):
        for part, w_ref in enumerate((w_a_ref, w_i_ref)):
            for h in range(heads):
                rep = _dot(w_ref[0, heads * q + h].astype(bf16), spread)
                blk = jnp.where(col_head == h, rep, 0.0).astype(bf16)
                small_w_ref[q, h * LRU_HEAD_DIM:(h + 1) * LRU_HEAD_DIM, part * half:(part + 1) * half] = blk
        small_w_ref[q, :, 2 * half:3 * half] = jnp.zeros((half, half), bf16)
        for g in range(2):
            rows = slice(g * POOL_GROUP, (g + 1) * POOL_GROUP)
            cols = slice(2 * half + g * POOL_GROUP, 2 * half + (g + 1) * POOL_GROUP)
            small_w_ref[q, rows, cols] = pool_w_ref[0, 2 * q + g].astype(bf16)


def _layer_kernel(x_ref, ln1_ref, w_in_hbm, pool_w_ref, pool_scale_ref, conv_w_ref, conv_b_ref,
                  w_a_ref, b_a_ref, w_i_ref, b_i_ref, lam_ref, gn_pool_ref, gn_lru_ref, w_out_hbm,
                  ln2_ref, wg_hbm, wu_hbm, wd_hbm, lnf_ref, o_ref,
                  pbuf, lbuf, abuf, bbuf, hbuf, hcarry, h1buf, n2buf,
                  w_in_ref, w_out_ref, wgu_ref, wd_ref, small_w_ref, stage, stage_sems,
                  *, chunks_per_seq, n_chunks):
    i = pl.program_id(0)
    s = lax.rem(i, chunks_per_seq)
    T = SEQ_TILE
    half = LRU_WIDTH // 2

    @pl.when(i == 0)
    def _():
        g_map = tuple((c0, c1, 2 * c0) for c0, c1 in _ff_chunk_cols())
        u_map = tuple((c0, c1, c0 + c1) for c0, c1 in _ff_chunk_cols())
        _load_weights_bf16(((w_in_hbm, w_in_ref, None, None), (w_out_hbm, w_out_ref, None, None),
                            (wg_hbm, wgu_ref, 0.5, g_map), (wu_hbm, wgu_ref, None, u_map),
                            (wd_hbm, wd_ref, None, None)), stage, stage_sems)
        _build_small_weights(pool_w_ref, w_a_ref, w_i_ref, small_w_ref)

    slot = lax.rem(i, 2)
    prev = 1 - slot

    def mixer():
        @pl.when(s == 0)
        def _():
            pbuf[0:POOL_TAIL, :] = jnp.zeros((POOL_TAIL, POOL_WIDTH), jnp.float32)
            lbuf[0:CONV_TAIL, :] = jnp.zeros((CONV_TAIL, LRU_WIDTH), jnp.float32)
            hcarry[...] = jnp.zeros_like(hcarry)

        yield FFN_PIECES_AT[0]
        nb = _rms(x_ref[...], ln1_ref[...]).astype(jnp.bfloat16)

        lbuf[CONV_TAIL:CONV_TAIL + T, :] = _dot(nb, w_in_ref[:, POOL_WIDTH:POOL_WIDTH + LRU_WIDTH])
        u_gate = _dot(nb, w_in_ref[:, POOL_WIDTH + LRU_WIDTH:])

        yield FFN_PIECES_AT[2]
        assert CONV_WIDTH == 4
        u_ext = lbuf[...]
        u_prev = pltpu.roll(u_ext, 1, axis=0)
        tap = lambda k: conv_w_ref[0, k:k + 1, :]
        older = tap(1) * u_ext + tap(0) * u_prev
        xc = (conv_b_ref[...] + tap(3) * u_ext + tap(2) * u_prev + pltpu.roll(older, 2, axis=0))[CONV_TAIL:]
        xcb = xc.astype(jnp.bfloat16)
        neg_lam = -lam_ref[...]
        softplus = jnp.maximum(neg_lam, 0.0) + jnp.log1p(jnp.exp(-jnp.abs(neg_lam)))
        k_row = (0.5 * LRU_C) * softplus

        nq = LRU_WIDTH // LANES
        tiles_per_sub = SUB_LEN // SUBLANES
        blk_rows = T // GATE_BLOCKS
        for blk in range(GATE_BLOCKS):
            rows_blk = slice(blk * blk_rows, (blk + 1) * blk_rows)
            z = [_dot(xcb[rows_blk, q * half:(q + 1) * half], small_w_ref[q, :, 0:2 * half])
                 for q in range(2)]
            yield FFN_PIECES_AT[3][blk]
            za = jnp.concatenate([z[0][:, :half], z[1][:, :half]], axis=-1) + b_a_ref[...]
            zi = jnp.concatenate([z[0][:, half:], z[1][:, half:]], axis=-1) + b_i_ref[...]
            neg_log_a = jnp.tanh(0.5 * za) * k_row + k_row
            a = jnp.exp(-neg_log_a)
            one_minus_a2 = jnp.tanh(neg_log_a) * (a * a + 1.0)
            mult = jnp.where(one_minus_a2 > 0.0, one_minus_a2 * lax.rsqrt(one_minus_a2), 0.0)
            bterm = mult * (_sigmoid(zi) * xc[rows_blk, :])
            for q in range(nq):
                lanes = slice(q * LANES, (q + 1) * LANES)
                for t in range(blk_rows // SUBLANES):
                    j, m0 = divmod(blk * (blk_rows // SUBLANES) + t, tiles_per_sub)
                    dst = pl.ds(m0 * SUBLANES * SUBLANES + j, SUBLANES, stride=SUBLANES)
                    rows = slice(t * SUBLANES, (t + 1) * SUBLANES)
                    abuf[q, dst, :] = a[rows, lanes]
                    bbuf[q, dst, :] = bterm[rows, lanes]

        def tile(m):
            return slice(m * SUBLANES, (m + 1) * SUBLANES)

        yield FFN_PIECES_AT[4]
        prod = [jnp.ones((SUBLANES, LANES), jnp.float32) for _ in range(nq)]
        loc = [jnp.zeros((SUBLANES, LANES), jnp.float32) for _ in range(nq)]
        for m in range(SUB_LEN):
            for q in range(nq):
                am = abuf[q, tile(m), :]
                loc[q] = am * loc[q] + bbuf[q, tile(m), :]
                prod[q] = am * prod[q]
        row = lax.broadcasted_iota(jnp.int32, (SUBLANES, LANES), 0)
        h_in = []
        for q in range(nq):
            carry = hcarry[:, q * LANES:(q + 1) * LANES]
            start = jnp.zeros((SUBLANES, LANES), jnp.float32)
            for j in range(SUBLANES):
                start = jnp.where(row == j, carry, start)
                carry = prod[q][j:j + 1, :] * carry + loc[q][j:j + 1, :]
            h_in.append(start)
            hcarry[:, q * LANES:(q + 1) * LANES] = carry
        yield FFN_PIECES_AT[5]
        for m in range(SUB_LEN):
            for q in range(nq):
                h_in[q] = abuf[q, tile(m), :] * h_in[q] + bbuf[q, tile(m), :]
                hbuf[q, pl.ds(m, SUBLANES, stride=SUB_PITCH), :] = h_in[q]
        h_lru = jnp.concatenate(
            [jnp.concatenate([hbuf[q, j * SUB_PITCH:j * SUB_PITCH + SUB_LEN, :] for j in range(SUBLANES)], axis=0)
             for q in range(nq)], axis=-1)

        pbuf[POOL_TAIL:POOL_TAIL + T, :] = _dot(nb, w_in_ref[:, 0:POOL_WIDTH])
        yield FFN_PIECES_AT[1]
        n_groups = len(POOL_WINDOWS)
        level = pbuf[...]
        sums = []
        for g, w in enumerate(POOL_WINDOWS):
            assert w == 2 << g and w - 1 <= POOL_TAIL
            rest = level[:, (0 if g == 0 else POOL_GROUP):]
            rest = rest + pltpu.roll(rest, w // 2, axis=0)
            sums.append(rest[POOL_TAIL:, 0:POOL_GROUP])
            level = rest
        head_pos = (s * T + 1 + lax.broadcasted_iota(jnp.int32, (POOL_TAIL, POOL_GROUP), 0)).astype(jnp.float32)
        d_parts = []
        for g, w in enumerate(POOL_WINDOWS):
            ug = pbuf[POOL_TAIL:POOL_TAIL + T, g * POOL_GROUP:(g + 1) * POOL_GROUP]
            head = sums[g][0:POOL_TAIL] / jnp.minimum(head_pos, float(w))
            mean = jnp.concatenate([head, sums[g][POOL_TAIL:] * (1.0 / w)], axis=0)
            d_parts.append((mean - ug).astype(jnp.bfloat16))
        y_parts = []
        for p in range(n_groups // 2):
            d_pair = jnp.concatenate(d_parts[2 * p:2 * p + 2], axis=-1)
            y_parts.append(_dot(d_pair, small_w_ref[p, :, 2 * half:3 * half]))
        y_pool = jnp.concatenate(y_parts, axis=-1) * pool_scale_ref[...]
        mix_pool = _rms(y_pool, gn_pool_ref[...]).astype(jnp.bfloat16)

        yield FFN_PIECES_AT[6]
        y_lru = _gelu_tanh_times(u_gate, h_lru)
        mix_lru = _rms(y_lru, gn_lru_ref[...]).astype(jnp.bfloat16)

        mix = jnp.concatenate([mix_pool, mix_lru], axis=-1)
        h1_new = x_ref[...] + _dot(mix, w_out_ref[...])
        yield FFN_PIECES_AT[7]
        h1buf[slot] = h1_new
        n2buf[slot] = _rms(h1_new, ln2_ref[...]).astype(jnp.bfloat16)

        pbuf[0:POOL_TAIL, :] = pbuf[T:T + POOL_TAIL, :]
        lbuf[0:CONV_TAIL, :] = lbuf[T:T + CONV_TAIL, :]

    def make_ffn():
        acts = {}
        ffn_parts = []
        pending = _ffn_sequence()

        def ffn(n):
            for _ in range(n):
                op, c = pending.pop(0)
                if op == "u":
                    c0, c1 = _ff_chunk_cols()[c]
                    n2 = n2buf[prev]
                    gu = _dot(n2, wgu_ref[:, 2 * c0:2 * c1])
                    half_g, up = gu[:, :c1 - c0], gu[:, c1 - c0:]
                    acts[c] = (half_g * (jnp.tanh(half_g) + 1.0) * up).astype(jnp.bfloat16)
                else:
                    group = range(c, min(c + FF_DOWN_GROUP, N_FF_CHUNKS))
                    act = jnp.concatenate([acts.pop(k) for k in group], axis=-1)
                    ffn_parts.append(_dot(act, wd_ref[c * FF_CHUNK:min((group[-1] + 1) * FF_CHUNK, D_FF), :]))

        def finish():
            ffn(len(pending))
            ffn_out = ffn_parts[0]
            for part in ffn_parts[1:]:
                ffn_out = ffn_out + part
            o_ref[...] = _rms(h1buf[prev] + ffn_out, lnf_ref[...])

        return ffn, finish

    @pl.when(i == 0)
    def _():
        for _ in mixer():
            pass

    @pl.when(jnp.logical_and(i > 0, i < n_chunks))
    def _():
        ffn, finish = make_ffn()
        for n in mixer():
            ffn(n)
        finish()

    @pl.when(i == n_chunks)
    def _():
        _, finish = make_ffn()
        finish()


def kernel(x, ln1_g, w_in, pool_w, pool_scale, conv_w, conv_b, w_a, b_a, w_i, b_i, lam, gn_pool_g,
           gn_lru_g, w_out, ln2_g, w_ffn_gate, w_ffn_up, w_ffn_down, lnf_g):
    B, S, D = x.shape
    assert D == D_MODEL and S % SEQ_TILE == 0 and ln1_g.shape[0] == 1
    bf16 = jnp.bfloat16
    vmem = pl.BlockSpec(memory_space=pltpu.VMEM)
    hbm = pl.BlockSpec(memory_space=pl.ANY)
    nq = LRU_WIDTH // LANES
    scan_rows = SUBLANES * SUB_PITCH
    cps = S // SEQ_TILE
    n_chunks = B * cps

    def in_index(i):
        c = jnp.minimum(i, n_chunks - 1)
        return (c // cps, c % cps, 0)

    def out_index(i):
        c = jnp.maximum(i - 1, 0)
        return (c // cps, c % cps, 0)

    out = pl.pallas_call(
        functools.partial(_layer_kernel, chunks_per_seq=cps, n_chunks=n_chunks),
        grid=(n_chunks + 1,),
        in_specs=[pl.BlockSpec((None, SEQ_TILE, D), in_index)] + [
            hbm if k in HBM_WEIGHT_OPERANDS else vmem for k in range(1, 20)],
        out_specs=pl.BlockSpec((None, SEQ_TILE, D), out_index),
        out_shape=jax.ShapeDtypeStruct((B, S, D), jnp.float32),
        scratch_shapes=[
            pltpu.VMEM((POOL_TAIL + SEQ_TILE, POOL_WIDTH), jnp.float32),
            pltpu.VMEM((CONV_TAIL + SEQ_TILE, LRU_WIDTH), jnp.float32),
            pltpu.VMEM((nq, SEQ_TILE, LANES), jnp.float32),
            pltpu.VMEM((nq, SEQ_TILE, LANES), jnp.float32),
            pltpu.VMEM((nq, scan_rows, LANES), jnp.float32),
            pltpu.VMEM((1, LRU_WIDTH), jnp.float32),
            pltpu.VMEM((2, SEQ_TILE, D), jnp.float32),
            pltpu.VMEM((2, SEQ_TILE, D), jnp.bfloat16),
            pltpu.VMEM((D, IN_WIDTH), bf16),
            pltpu.VMEM((D, D), bf16),
            pltpu.VMEM((D, 2 * D_FF), bf16),
            pltpu.VMEM((D_FF, D), bf16),
            pltpu.VMEM((2, LRU_WIDTH // 2, LRU_WIDTH + LRU_WIDTH // 2), bf16),
            pltpu.VMEM((WEIGHT_STAGE_SLOTS, WEIGHT_STAGE_ROWS, D_FF), jnp.float32),
            pltpu.SemaphoreType.DMA((WEIGHT_STAGE_SLOTS,)),
        ],
        compiler_params=pltpu.CompilerParams(
            dimension_semantics=("arbitrary",),
            vmem_limit_bytes=VMEM_LIMIT_BYTES),
        name="hymba_layer",
    )(x, ln1_g, w_in[0], pool_w, pool_scale, conv_w, conv_b, w_a, b_a, w_i, b_i, lam, gn_pool_g, gn_lru_g,
      w_out[0], ln2_g, w_ffn_gate[0], w_ffn_up[0], w_ffn_down[0], lnf_g.reshape(1, D))
    return out
```
